```python
import functools
import jax, jax.numpy as jnp
from jax import lax
import numpy as np

D_MODEL = 1024
BATCH = 8
SEQ = 4096
DEPTH = 2
DEC_BATCH = 128
DEC_SEQ = 1
PAST_LEN = 16384
PAGE_SIZE = 128

A_HEADS = 4
A_DK = 64
A_DV = 64
CONV_W = 4
B_HEADS = 4
B_DK = 32
B_DV = 64
B_GATE_RANK = 16
B_GATE_TAU = 16.0
C_HEADS = 4
C_NOPE = 128
C_ROPE = 64
C_VDIM = 128
C_Q_LORA = 384
C_KV_LORA = 256
ROPE_THETA = 10000.0
N_MEM = 256
X_HEADS = 4
X_HDIM = D_MODEL // X_HEADS
D_FF = 2816
CHUNK = 64
Q_BLOCK = 128
EPS = 1e-6

A_QK = A_HEADS * A_DK
A_V = A_HEADS * A_DV
A_CONV_CH = 2 * A_QK + A_V
A_IN = A_CONV_CH + A_V + 2 * A_HEADS
B_QK = B_HEADS * B_DK
B_V = B_HEADS * B_DV
B_IN = 2 * B_QK + 2 * B_V + B_GATE_RANK
C_IN = C_Q_LORA + C_KV_LORA + C_ROPE
D_IN = A_IN + B_IN + C_IN
C_V = C_HEADS * C_VDIM
D_MIX = A_V + B_V + C_V
MLA_SCALE = (C_NOPE + C_ROPE) ** -0.5

kernel_name = 'hybrid_gdn_gla_mla_macaron_step'


def rmsnorm(x, w):
    xf = x.astype(jnp.float32)
    y = xf * lax.rsqrt(jnp.mean(xf * xf, axis=-1, keepdims=True) + EPS)
    return (y * w.astype(jnp.float32)).astype(x.dtype)


def l2norm(x):
    xf = x.astype(jnp.float32)
    return xf * lax.rsqrt(jnp.sum(xf * xf, axis=-1, keepdims=True) + EPS)


def swiglu(x, wg, wu, wd):
    return (jax.nn.silu(x @ wg) * (x @ wu)) @ wd


def rope(x, pos):
    half = x.shape[-1] // 2
    inv = ROPE_THETA ** (-jnp.arange(half, dtype=jnp.float32) / half)
    ang = pos.astype(jnp.float32)[:, None] * inv[None, :]
    cos = jnp.cos(ang)[None, :, None, :]
    sin = jnp.sin(ang)[None, :, None, :]
    xf = x.astype(jnp.float32)
    x1, x2 = xf[..., :half], xf[..., half:]
    return jnp.concatenate([x1 * cos - x2 * sin, x1 * sin + x2 * cos], axis=-1).astype(x.dtype)


def to_chunks(a, c):
    b, t = a.shape[:2]
    n = -(-t // c)
    a = jnp.pad(a.astype(jnp.float32), [(0, 0), (0, n * c - t)] + [(0, 0)] * (a.ndim - 2))
    return jnp.moveaxis(a.reshape((b, n, c) + a.shape[2:]), 1, 0)


def from_chunks(a, t):
    a = jnp.moveaxis(a, 0, 1)
    return a.reshape((a.shape[0], -1) + a.shape[3:])[:, :t]


def causal_conv_silu(x, buf, w):
    t = x.shape[1]
    xp = jnp.concatenate([buf.astype(x.dtype), x], axis=1)
    y = xp[:, 0:t] * w[0]
    for j in range(1, CONV_W):
        y = y + xp[:, j:j + t] * w[j]
    return jax.nn.silu(y), xp[:, t:]


def gated_delta_rule(q, k, v, beta, g, s0):
    t = q.shape[1]
    c = min(CHUNK, t)
    incl = jnp.tril(jnp.ones((c, c), bool))
    strict = jnp.tril(jnp.ones((c, c), bool), -1)
    solve = functools.partial(lax.linalg.triangular_solve, left_side=True, lower=True, unit_diagonal=True)

    def step(s, inp):
        qc, kc, vc, bc, gc = inp
        gcum = jnp.moveaxis(jnp.cumsum(gc, axis=1), 1, 2)
        decay = jnp.exp(jnp.where(incl, gcum[..., :, None] - gcum[..., None, :], -jnp.inf))
        qh, kh, vh = (jnp.moveaxis(a, 1, 2) for a in (qc, kc, vc))
        bh = jnp.moveaxis(bc, 1, 2)[..., None]
        kb = kh * bh
        a_mat = jnp.where(strict, jnp.einsum('bhtk,bhik->bhti', kb, kh) * decay, 0.0)
        u = solve(a_mat, vh * bh)
        w = solve(a_mat, kb * jnp.exp(gcum)[..., None])
        r = u - jnp.einsum('bhtk,bhkv->bhtv', w, s)
        o = (jnp.einsum('bhtk,bhkv->bhtv', qh * jnp.exp(gcum)[..., None], s)
             + jnp.einsum('bhti,bhiv->bhtv', jnp.einsum('bhtk,bhik->bhti', qh, kh) * decay, r))
        g_last = gcum[..., -1:]
        s_new = (jnp.exp(g_last)[..., None] * s
                 + jnp.einsum('bhik,bhiv->bhkv', kh * jnp.exp(g_last - gcum)[..., None], r))
        return s_new, jnp.moveaxis(o, 1, 2)

    xs = tuple(to_chunks(a, c) for a in (q, k, v, beta, g))
    s_fin, o = lax.scan(step, s0.astype(jnp.float32), xs)
    return from_chunks(o, t), s_fin


def gla_rule(q, k, v, log_a, s0):
    t = q.shape[1]
    c = min(CHUNK, t)
    incl = jnp.tril(jnp.ones((c, c), bool))[None, :, :, None, None]

    def step(s, inp):
        qc, kc, vc, ac = inp
        b = jnp.cumsum(ac, axis=1)
        decay = jnp.exp(jnp.where(incl, b[:, :, None] - b[:, None, :], -jnp.inf))
        att = jnp.einsum('bthk,btihk,bihk->bhti', qc, decay, kc)
        o = (jnp.einsum('bhti,bihv->bthv', att, vc)
             + jnp.einsum('bthk,bhkv->bthv', qc * jnp.exp(b), s))
        b_last = b[:, -1]
        s_new = (jnp.exp(b_last)[..., None] * s
                 + jnp.einsum('bihk,bihv->bhkv', kc * jnp.exp(b_last[:, None] - b), vc))
        return s_new, o

    xs = tuple(to_chunks(a, c) for a in (q, k, v, log_a))
    s_fin, o = lax.scan(step, s0.astype(jnp.float32), xs)
    return from_chunks(o, t), s_fin


def mla_scores(q_lat, q_pe, ckv, kpe):
    s = jnp.einsum('bqhl,bkl->bhqk', q_lat, ckv) + jnp.einsum('bqhr,bkr->bhqk', q_pe, kpe)
    return s.astype(jnp.float32) * MLA_SCALE


def mla_prompt_attend(q_lat, q_pe, ckv, kpe):
    b, t, h, lr = q_lat.shape
    nblk = t // Q_BLOCK
    qb = jnp.moveaxis(q_lat.reshape(b, nblk, Q_BLOCK, h, lr), 1, 0)
    pb = jnp.moveaxis(q_pe.reshape(b, nblk, Q_BLOCK, h, C_ROPE), 1, 0)
    kpos = jnp.arange(t)

    def block(inp):
        ql, qp, i = inp
        s = mla_scores(ql, qp, ckv, kpe)
        qpos = i * Q_BLOCK + jnp.arange(Q_BLOCK)
        s = jnp.where(kpos[None, :] <= qpos[:, None], s, -jnp.inf)
        p = jax.nn.softmax(s, axis=-1).astype(ckv.dtype)
        return jnp.einsum('bhqk,bkl->bqhl', p, ckv)

    o = lax.map(block, (qb, pb, jnp.arange(nblk)))
    return jnp.moveaxis(o, 0, 1).reshape(b, t, h, lr)


def mla_decode_attend(q_lat, q_pe, ckv_new, kpe_new, ckv_past, kpe_past):
    p_len = ckv_past.shape[1]
    t = q_lat.shape[1]
    causal = jnp.tril(jnp.ones((t, t), bool))
    s = jnp.concatenate([mla_scores(q_lat, q_pe, ckv_past, kpe_past),
                         jnp.where(causal, mla_scores(q_lat, q_pe, ckv_new, kpe_new), -jnp.inf)], axis=-1)
    p = jax.nn.softmax(s, axis=-1).astype(ckv_new.dtype)
    return (jnp.einsum('bhqk,bkl->bqhl', p[..., :p_len], ckv_past)
            + jnp.einsum('bhqk,bkl->bqhl', p[..., p_len:], ckv_new))


def mixer_gdn(u, conv_buf, s0, W):
    b, t, _ = u.shape
    qkv, conv_new = causal_conv_silu(u[..., :A_CONV_CH], conv_buf, W['a_conv_w'])
    z = u[..., A_CONV_CH:A_CONV_CH + A_V].reshape(b, t, A_HEADS, A_DV)
    beta_raw = u[..., A_CONV_CH + A_V:A_CONV_CH + A_V + A_HEADS]
    a_raw = u[..., A_CONV_CH + A_V + A_HEADS:A_IN]
    q = l2norm(qkv[..., :A_QK].reshape(b, t, A_HEADS, A_DK)) * (A_DK ** -0.5)
    k = l2norm(qkv[..., A_QK:2 * A_QK].reshape(b, t, A_HEADS, A_DK))
    v = qkv[..., 2 * A_QK:].reshape(b, t, A_HEADS, A_DV)
    beta = jax.nn.sigmoid(beta_raw.astype(jnp.float32))
    g = -jnp.exp(W['a_log'].astype(jnp.float32)) * jax.nn.softplus(
        a_raw.astype(jnp.float32) + W['a_dt_bias'].astype(jnp.float32))
    o, s_new = gated_delta_rule(q, k, v, beta, g, s0)
    o = rmsnorm(o.astype(u.dtype), W['a_norm']) * jax.nn.silu(z)
    return o.reshape(b, t, A_V), conv_new, s_new.astype(u.dtype)


def mixer_gla(u, s0, W):
    b, t, _ = u.shape
    q = u[..., :B_QK].reshape(b, t, B_HEADS, B_DK) * (B_DK ** -0.5)
    k = u[..., B_QK:2 * B_QK].reshape(b, t, B_HEADS, B_DK)
    v = u[..., 2 * B_QK:2 * B_QK + B_V].reshape(b, t, B_HEADS, B_DV)
    r = u[..., 2 * B_QK + B_V:2 * B_QK + 2 * B_V].reshape(b, t, B_HEADS, B_DV)
    gate_lr = u[..., 2 * B_QK + 2 * B_V:]
    log_a = jax.nn.log_sigmoid((gate_lr @ W['b_gate_w2'] + W['b_gate_bias']).astype(jnp.float32)) / B_GATE_TAU
    o, s_new = gla_rule(q, k, v, log_a.reshape(b, t, B_HEADS, B_DK), s0)
    o = rmsnorm(o.astype(u.dtype), W['b_norm']) * jax.nn.silu(r)
    return o.reshape(b, t, B_V), s_new.astype(u.dtype)


def mixer_mla(u, pos, attend, W):
    b, t, _ = u.shape
    cq = rmsnorm(u[..., :C_Q_LORA], W['c_q_norm'])
    ckv = rmsnorm(u[..., C_Q_LORA:C_Q_LORA + C_KV_LORA], W['c_kv_norm'])
    kpe = rope(u[..., C_Q_LORA + C_KV_LORA:][:, :, None, :], pos)[:, :, 0]
    q = (cq @ W['c_w_uq']).reshape(b, t, C_HEADS, C_NOPE + C_ROPE)
    q_pe = rope(q[..., C_NOPE:], pos)
    q_lat = jnp.einsum('bthn,lhn->bthl', q[..., :C_NOPE], W['c_w_uk'])
    o_lat = attend(q_lat, q_pe, ckv, kpe)
    o = jnp.einsum('bthl,lhv->bthv', o_lat, W['c_w_uv']).reshape(b, t, C_V)
    return o, ckv, kpe


def memory_kv(mem, norm_mem, w_xk, w_xv):
    b, n, _ = mem.shape
    m = rmsnorm(mem, norm_mem)
    return (m @ w_xk).reshape(b, n, X_HEADS, X_HDIM), (m @ w_xv).reshape(b, n, X_HEADS, X_HDIM)


def cross_attention(h, mem_k, mem_v, w_q, w_o):
    b, t, _ = h.shape
    q = (h @ w_q).reshape(b, t, X_HEADS, X_HDIM)
    s = jnp.einsum('bthd,bnhd->bhtn', q, mem_k).astype(jnp.float32) * (X_HDIM ** -0.5)
    p = jax.nn.softmax(s, axis=-1).astype(h.dtype)
    return jnp.einsum('bhtn,bnhd->bthd', p, mem_v).reshape(b, t, D_MODEL) @ w_o


def layer_forward(x, pos, conv_buf, s_delta, s_gla, attend, mem_k, mem_v, W):
    x = x + 0.5 * swiglu(rmsnorm(x, W['norm_ffn1']), W['w_ffn1_gate'], W['w_ffn1_up'], W['w_ffn1_down'])
    u = rmsnorm(x, W['norm_mix']) @ W['w_in']
    o_a, conv_new, sd_new = mixer_gdn(u[..., :A_IN], conv_buf, s_delta, W)
    o_b, sg_new = mixer_gla(u[..., A_IN:A_IN + B_IN], s_gla, W)
    o_c, ckv, kpe = mixer_mla(u[..., A_IN + B_IN:], pos, attend, W)
    x = x + jnp.concatenate([o_a, o_b, o_c], axis=-1) @ W['w_out']
    x = x + cross_attention(rmsnorm(x, W['norm_x']), mem_k, mem_v, W['w_xq'], W['w_xo'])
    x = x + 0.5 * swiglu(rmsnorm(x, W['norm_ffn2']), W['w_ffn2_gate'], W['w_ffn2_up'], W['w_ffn2_down'])
    return x, ckv, kpe, conv_new, sd_new, sg_new


def setup_inputs(seed: int = 0) -> dict:
    key = jax.random.key(seed)
    keys = jax.random.split(key, 64)
    counter = [0]

    def nk():
        counter[0] += 1
        return keys[counter[0] - 1]

    def nrm(shape, scale=1.0):
        return scale * jax.random.normal(nk(), shape, jnp.float32)

    def gain(shape):
        return 1.0 + 0.02 * jax.random.normal(nk(), shape, jnp.float32)

    n_pages = PAST_LEN // PAGE_SIZE
    n_phys = (DEC_BATCH * n_pages * 5) // 4
    page_table = jax.random.permutation(nk(), n_phys)[:DEC_BATCH * n_pages].reshape(DEC_BATCH, n_pages).astype(jnp.int32)
    dt = jnp.exp(jax.random.uniform(nk(), (DEPTH, A_HEADS), jnp.float32, np.log(1e-3), np.log(1e-1)))
    a_log = jnp.log(jax.random.uniform(nk(), (DEPTH, A_HEADS), jnp.float32, 1.0, 16.0))
    L = DEPTH
    return {
        'x_prompt': nrm((BATCH, SEQ, D_MODEL)),
        'x_sample': nrm((DEC_BATCH, DEC_SEQ, D_MODEL)),
        'mem_prompt': nrm((BATCH, N_MEM, D_MODEL)),
        'cache_ckv': nrm((L, n_phys, PAGE_SIZE, C_KV_LORA)),
        'cache_kpe': nrm((L, n_phys, PAGE_SIZE, C_ROPE)),
        'page_table': page_table,
        'state_conv_a': nrm((L, DEC_BATCH, CONV_W - 1, A_CONV_CH)),
        'state_delta': nrm((L, DEC_BATCH, A_HEADS, A_DK, A_DV), 0.3),
        'state_gla': nrm((L, DEC_BATCH, B_HEADS, B_DK, B_DV)),
        'cache_mem_k': nrm((L, DEC_BATCH, N_MEM, X_HEADS, X_HDIM)),
        'cache_mem_v': nrm((L, DEC_BATCH, N_MEM, X_HEADS, X_HDIM)),
        'norm_ffn1': gain((L, D_MODEL)),
        'w_ffn1_gate': nrm((L, D_MODEL, D_FF), D_MODEL ** -0.5),
        'w_ffn1_up': nrm((L, D_MODEL, D_FF), D_MODEL ** -0.5),
        'w_ffn1_down': nrm((L, D_FF, D_MODEL), D_FF ** -0.5),
        'norm_mix': gain((L, D_MODEL)),
        'w_in': nrm((L, D_MODEL, D_IN), D_MODEL ** -0.5),
        'a_conv_w': nrm((L, CONV_W, A_CONV_CH), CONV_W ** -0.5),
        'a_log': a_log,
        'a_dt_bias': dt + jnp.log(-jnp.expm1(-dt)),
        'a_norm': gain((L, A_DV)),
        'b_gate_w2': nrm((L, B_GATE_RANK, B_QK), B_GATE_RANK ** -0.5),
        'b_gate_bias': nrm((L, B_QK), 0.1),
        'b_norm': gain((L, B_DV)),
        'c_q_norm': gain((L, C_Q_LORA)),
        'c_w_uq': nrm((L, C_Q_LORA, C_HEADS * (C_NOPE + C_ROPE)), C_Q_LORA ** -0.5),
        'c_kv_norm': gain((L, C_KV_LORA)),
        'c_w_uk': nrm((L, C_KV_LORA, C_HEADS, C_NOPE), C_KV_LORA ** -0.5),
        'c_w_uv': nrm((L, C_KV_LORA, C_HEADS, C_VDIM), C_KV_LORA ** -0.5),
        'w_out': nrm((L, D_MIX, D_MODEL), D_MIX ** -0.5),
        'norm_x': gain((L, D_MODEL)),
        'norm_mem': gain((L, D_MODEL)),
        'w_xq': nrm((L, D_MODEL, D_MODEL), D_MODEL ** -0.5),
        'w_xk': nrm((L, D_MODEL, D_MODEL), D_MODEL ** -0.5),
        'w_xv': nrm((L, D_MODEL, D_MODEL), D_MODEL ** -0.5),
        'w_xo': nrm((L, D_MODEL, D_MODEL), D_MODEL ** -0.5),
        'norm_ffn2': gain((L, D_MODEL)),
        'w_ffn2_gate': nrm((L, D_MODEL, D_FF), D_MODEL ** -0.5),
        'w_ffn2_up': nrm((L, D_MODEL, D_FF), D_MODEL ** -0.5),
        'w_ffn2_down': nrm((L, D_FF, D_MODEL), D_FF ** -0.5),
        'final_norm': gain((D_MODEL,)),
    }


def reference(x_prompt, x_sample, mem_prompt, cache_ckv, cache_kpe, page_table, state_conv_a, state_delta,
              state_gla, cache_mem_k, cache_mem_v, norm_ffn1, w_ffn1_gate, w_ffn1_up, w_ffn1_down, norm_mix,
              w_in, a_conv_w, a_log, a_dt_bias, a_norm, b_gate_w2, b_gate_bias, b_norm, c_q_norm, c_w_uq,
              c_kv_norm, c_w_uk, c_w_uv, w_out, norm_x, norm_mem, w_xq, w_xk, w_xv, w_xo, norm_ffn2,
              w_ffn2_gate, w_ffn2_up, w_ffn2_down, final_norm):
    stacked = dict(norm_ffn1=norm_ffn1, w_ffn1_gate=w_ffn1_gate, w_ffn1_up=w_ffn1_up, w_ffn1_down=w_ffn1_down,
                   norm_mix=norm_mix, w_in=w_in, a_conv_w=a_conv_w, a_log=a_log, a_dt_bias=a_dt_bias,
                   a_norm=a_norm, b_gate_w2=b_gate_w2, b_gate_bias=b_gate_bias, b_norm=b_norm,
                   c_q_norm=c_q_norm, c_w_uq=c_w_uq, c_kv_norm=c_kv_norm, c_w_uk=c_w_uk, c_w_uv=c_w_uv,
                   w_out=w_out, norm_x=norm_x, norm_mem=norm_mem, w_xq=w_xq, w_xk=w_xk, w_xv=w_xv, w_xo=w_xo,
                   norm_ffn2=norm_ffn2, w_ffn2_gate=w_ffn2_gate, w_ffn2_up=w_ffn2_up, w_ffn2_down=w_ffn2_down)

    def layer_weights(l):
        return {name: arr[l] for name, arr in stacked.items()}

    bp, tp, _ = x_prompt.shape
    pos_p = jnp.arange(tp, dtype=jnp.int32)
    x = x_prompt
    p_ckv, p_kpe, p_conv, p_delta, p_gla, p_mk, p_mv = [], [], [], [], [], [], []
    for l in range(DEPTH):
        W = layer_weights(l)
        mk, mv = memory_kv(mem_prompt, W['norm_mem'], W['w_xk'], W['w_xv'])
        x, ckv, kpe, cb, sd, sg = layer_forward(
            x, pos_p, jnp.zeros((bp, CONV_W - 1, A_CONV_CH), x.dtype),
            jnp.zeros((bp, A_HEADS, A_DK, A_DV), jnp.float32), jnp.zeros((bp, B_HEADS, B_DK, B_DV), jnp.float32),
            mla_prompt_attend, mk, mv, W)
        p_ckv.append(ckv); p_kpe.append(kpe); p_conv.append(cb); p_delta.append(sd); p_gla.append(sg)
        p_mk.append(mk); p_mv.append(mv)
    y_prompt = rmsnorm(x, final_norm)

    bs, ts, _ = x_sample.shape
    past_len = page_table.shape[1] * cache_ckv.shape[2]
    pos_s = past_len + jnp.arange(ts, dtype=jnp.int32)
    x = x_sample
    s_ckv, s_kpe, s_conv, s_delta, s_gla = [], [], [], [], []
    for l in range(DEPTH):
        W = layer_weights(l)
        ckv_past = cache_ckv[l, page_table].reshape(bs, past_len, C_KV_LORA)
        kpe_past = cache_kpe[l, page_table].reshape(bs, past_len, C_ROPE)
        attend = functools.partial(mla_decode_attend, ckv_past=ckv_past, kpe_past=kpe_past)
        x, ckv, kpe, cb, sd, sg = layer_forward(
            x, pos_s, state_conv_a[l], state_delta[l], state_gla[l], attend, cache_mem_k[l], cache_mem_v[l], W)
        s_ckv.append(ckv); s_kpe.append(kpe); s_conv.append(cb); s_delta.append(sd); s_gla.append(sg)
    y_sample = rmsnorm(x, final_norm)

    return (y_prompt, y_sample,
            jnp.stack(p_ckv), jnp.stack(p_kpe), jnp.stack(p_conv), jnp.stack(p_delta), jnp.stack(p_gla),
            jnp.stack(p_mk), jnp.stack(p_mv),
            jnp.stack(s_ckv), jnp.stack(s_kpe), jnp.stack(s_conv), jnp.stack(s_delta), jnp.stack(s_gla))
```

```python
import functools

import jax
import jax.numpy as jnp
import numpy as np
from jax import lax
from jax.experimental import pallas as pl
from jax.experimental.pallas import tpu as pltpu

F32 = jnp.float32
BF16 = jnp.bfloat16

D_MODEL = 1024
A_HEADS, A_DK, A_DV, CONV_W = 4, 64, 64, 4
B_HEADS, B_DK, B_DV, B_GATE_RANK, B_GATE_TAU = 4, 32, 64, 16, 16.0
C_HEADS, C_NOPE, C_ROPE, C_VDIM, C_Q_LORA, C_KV_LORA = 4, 128, 64, 128, 384, 256
ROPE_THETA = 10000.0
X_HEADS, X_HDIM = 4, 256
D_FF = 2816
CHUNK = 64
EPS = 1e-6

A_QK = A_HEADS * A_DK
A_V = A_HEADS * A_DV
A_CONV_CH = 2 * A_QK + A_V
A_IN = A_CONV_CH + A_V + 2 * A_HEADS
B_QK = B_HEADS * B_DK
B_V = B_HEADS * B_DV
B_IN = 2 * B_QK + 2 * B_V + B_GATE_RANK
C_IN = C_Q_LORA + C_KV_LORA + C_ROPE
MLA_SCALE = (C_NOPE + C_ROPE) ** -0.5

LANE = 128
UA_W = A_CONV_CH + A_V + LANE
UB_W = 2 * B_QK + 2 * B_V + LANE
UC_W = C_Q_LORA + C_KV_LORA + LANE
SUB = 16
VMEM_LIMIT = 56 * 1024 * 1024


def _cparams(sem):
    return pltpu.CompilerParams(dimension_semantics=sem, vmem_limit_bytes=VMEM_LIMIT)


def _const_spec(shape):
    nd = len(shape)
    return pl.BlockSpec(shape, lambda *_: (0,) * nd, pipeline_mode=pl.Buffered(1))


def _rms(x, w):
    return x * lax.rsqrt(jnp.mean(x * x, axis=-1, keepdims=True) + EPS) * w


def _silu(x):
    return x * jax.nn.sigmoid(x)


def _bdot(a, b):
    return jnp.dot(a.astype(BF16), b.astype(BF16), preferred_element_type=F32)


def _bdot_nt(a, b):
    return lax.dot_general(a.astype(BF16), b.astype(BF16), (((1,), (1,)), ((), ())),
                           preferred_element_type=F32)


def _bdot_tn(a, b):
    return lax.dot_general(a.astype(BF16), b.astype(BF16), (((0,), (0,)), ((), ())),
                           preferred_element_type=F32)


def _iota(shape, dim):
    return lax.broadcasted_iota(jnp.int32, shape, dim)


def _block_mask(rows, cols, rb, cb):
    return (_iota((rows, cols), 0) // rb) == (_iota((rows, cols), 1) // cb)


def _block_diag(x, mask):
    n = mask.shape[0] // x.shape[0]
    xb = x.astype(BF16)
    return jnp.where(mask, jnp.concatenate([xb] * n, axis=0), jnp.zeros_like(xb[:1, :1]))


def _diag_blocks(m, rb, cb):
    n = m.shape[0] // rb
    lane_blk = _iota((rb, m.shape[1]), 1) // cb
    out = jnp.zeros((rb, m.shape[1]), m.dtype)
    for h in range(n):
        out = jnp.where(lane_blk == h, m[h * rb:(h + 1) * rb, :], out)
    return out


def _seg_sum(x, ones_bd):
    hi = x.astype(BF16)
    lo = (x - hi.astype(F32)).astype(BF16)
    return (jnp.dot(hi, ones_bd, preferred_element_type=F32)
            + jnp.dot(lo, ones_bd, preferred_element_type=F32))


def _expand_heads(x, off, nh, w):
    lane_blk = _iota((x.shape[0], nh * w), 1) // w
    out = jnp.zeros((x.shape[0], nh * w), x.dtype)
    for h in range(nh):
        out = jnp.where(lane_blk == h, x[:, off + h:off + h + 1], out)
    return out


def _cumsum_rows(x):
    n = x.shape[0]
    row = _iota(x.shape, 0)
    s = 1
    while s < n:
        x = x + jnp.where(row >= s, pltpu.roll(x, s, axis=0), 0.0)
        s *= 2
    return x


def _ffn_kernel(x_ref, nw_ref, wg_ref, wu_ref, wd_ref, fn_ref, o_ref, *, final):
    x = x_ref[...]
    xn = _rms(x, nw_ref[...]).astype(BF16)
    g = jnp.dot(xn, wg_ref[...], preferred_element_type=F32)
    u = jnp.dot(xn, wu_ref[...], preferred_element_type=F32)
    h = (_silu(g) * u).astype(BF16)
    y = x + 0.5 * jnp.dot(h, wd_ref[...], preferred_element_type=F32)
    if final:
        y = _rms(y, fn_ref[...])
    o_ref[...] = y


def _ffn(x, nw, wg, wu, wd, fn, *, final, tm):
    m = x.shape[0]
    return pl.pallas_call(
        functools.partial(_ffn_kernel, final=final),
        grid=(m // tm,),
        in_specs=[pl.BlockSpec((tm, D_MODEL), lambda i: (i, 0)),
                  _const_spec((1, D_MODEL)),
                  _const_spec((D_MODEL, D_FF)), _const_spec((D_MODEL, D_FF)),
                  _const_spec((D_FF, D_MODEL)), _const_spec((1, D_MODEL))],
        out_specs=pl.BlockSpec((tm, D_MODEL), lambda i: (i, 0)),
        out_shape=jax.ShapeDtypeStruct((m, D_MODEL), F32),
        compiler_params=_cparams(("parallel",)),
        name="ffn",
    )(x, nw, wg, wu, wd, fn)


def _norm_mm_kernel(x_ref, nw_ref, *refs, n_w):
    xn = _rms(x_ref[...], nw_ref[...]).astype(BF16)
    for w_ref, o_ref in zip(refs[:n_w], refs[n_w:]):
        o_ref[...] = jnp.dot(xn, w_ref[...], preferred_element_type=F32)


def _norm_mm(x, nw, ws, *, tm):
    m, d = x.shape
    return pl.pallas_call(
        functools.partial(_norm_mm_kernel, n_w=len(ws)),
        grid=(m // tm,),
        in_specs=[pl.BlockSpec((tm, d), lambda i: (i, 0)), _const_spec((1, d))]
                 + [_const_spec(w.shape) for w in ws],
        out_specs=[pl.BlockSpec((tm, w.shape[1]), lambda i: (i, 0)) for w in ws],
        out_shape=[jax.ShapeDtypeStruct((m, w.shape[1]), F32) for w in ws],
        compiler_params=_cparams(("parallel",)),
        name="norm_mm",
    )(x, nw, *ws)


def _mm_res_kernel(x_ref, *refs, n_a):
    acc = x_ref[...]
    for a_ref, w_ref in zip(refs[:n_a], refs[n_a:2 * n_a]):
        acc = acc + jnp.dot(a_ref[...].astype(BF16), w_ref[...], preferred_element_type=F32)
    refs[2 * n_a][...] = acc


def _mm_res(x, a_list, w_list, *, tm):
    m, d = x.shape
    n_a = len(a_list)
    return pl.pallas_call(
        functools.partial(_mm_res_kernel, n_a=n_a),
        grid=(m // tm,),
        in_specs=[pl.BlockSpec((tm, d), lambda i: (i, 0))]
                 + [pl.BlockSpec((tm, a.shape[1]), lambda i: (i, 0)) for a in a_list]
                 + [_const_spec(w.shape) for w in w_list],
        out_specs=pl.BlockSpec((tm, d), lambda i: (i, 0)),
        out_shape=jax.ShapeDtypeStruct((m, d), F32),
        compiler_params=_cparams(("parallel",)),
        name="mm_res",
    )(x, *a_list, *w_list)


def _gdn_prompt_kernel(u_ref, cw_ref, alog_ref, dtb_ref, anorm_ref, o_ref, conv_ref, s_out_ref,
                       xbuf, s_scr, *, tblk):
    t = pl.program_id(1)
    hw = A_HEADS * A_DV

    @pl.when(t == 0)
    def _():
        xbuf[0:8, :] = jnp.zeros((8, A_CONV_CH), F32)
        s_scr[...] = jnp.zeros_like(s_scr)

    x = u_ref[0, :, 0:A_CONV_CH]
    xbuf[8:8 + tblk, :] = x
    cw = cw_ref[...]
    y = (cw[0:1] * xbuf[5:5 + tblk, :] + cw[1:2] * xbuf[6:6 + tblk, :]
         + cw[2:3] * xbuf[7:7 + tblk, :] + cw[3:4] * x)
    xbuf[0:8, :] = x[tblk - 8:tblk, :]
    qkv = _silu(y)
    z = u_ref[0, :, A_CONV_CH:A_CONV_CH + A_V]
    ba = u_ref[0, :, A_CONV_CH + A_V:UA_W]

    ones_bd = _block_mask(hw, hw, A_DK, A_DK).astype(BF16)
    q = qkv[:, 0:A_QK]
    k = qkv[:, A_QK:2 * A_QK]
    v = qkv[:, 2 * A_QK:]
    q = q * lax.rsqrt(_seg_sum(q * q, ones_bd) + EPS) * (A_DK ** -0.5)
    k = k * lax.rsqrt(_seg_sum(k * k, ones_bd) + EPS)
    beta = _expand_heads(jax.nn.sigmoid(ba), 0, A_HEADS, A_DV)
    g = _expand_heads(-jnp.exp(alog_ref[...]) * jax.nn.softplus(ba + dtb_ref[...]),
                      A_HEADS, A_HEADS, A_DV)

    c = CHUNK
    bd = _block_mask(hw, hw, c, c)
    row = _iota((c, hw), 0)
    col = _iota((c, hw), 1) % c
    incl = row >= col
    strict = row > col
    s = s_scr[...]
    outs = []
    for ci in range(tblk // c):
        sl = slice(ci * c, (ci + 1) * c)
        qc, kc, vc, bc = q[sl], k[sl], v[sl], beta[sl]
        gcum = _cumsum_rows(g[sl])
        grow = jnp.sum(jnp.where(row == col, gcum, 0.0), axis=0, keepdims=True)
        decay = jnp.exp(jnp.where(incl, gcum - grow, -jnp.inf))
        eg = jnp.exp(gcum)
        kb = kc * bc
        aq = _bdot_nt(jnp.concatenate([kb, qc], axis=0), _block_diag(kc, bd))
        a = jnp.where(strict, aq[:c] * decay, 0.0)
        qk = aq[c:] * decay
        p = -a
        pw = _bdot(a, _block_diag(a, bd))
        n_sq = int(np.log2(c)) - 1
        for n in range(n_sq):
            if n < n_sq - 1:
                both = _bdot(jnp.concatenate([p, pw], axis=0), _block_diag(pw, bd))
                p = p + pw + both[:c]
                pw = both[c:]
            else:
                p = p + pw + _bdot(p, _block_diag(pw, bd))
        vb = vc * bc
        kbg = kb * eg
        uw = _bdot(p, jnp.concatenate([_block_diag(vb, bd), _block_diag(kbg, bd)], axis=1))
        u = vb + uw[:, :hw]
        w = kbg + uw[:, hw:]
        ws = _bdot(jnp.concatenate([w, qc * eg], axis=0), _block_diag(s, bd))
        r = u - ws[:c]
        outs.append(ws[c:] + _bdot(qk, _block_diag(r, bd)))
        g_last = gcum[c - 1:c, :]
        kd = kc * jnp.exp(g_last - gcum)
        s = jnp.exp(g_last) * s + _diag_blocks(_bdot_tn(kd, r), A_DK, A_DV)
    s_scr[...] = s

    o = jnp.concatenate(outs, axis=0)
    o = o * lax.rsqrt(_seg_sum(o * o, ones_bd) * (1.0 / A_DV) + EPS) * anorm_ref[...]
    o_ref[0] = o * _silu(z)

    @pl.when(t == pl.num_programs(1) - 1)
    def _():
        conv_ref[0] = x[tblk - (CONV_W - 1):tblk, :]
        s_out_ref[0] = s


def _gdn_prompt(u_a, cw, alog, dtb, anorm, *, tblk):
    b, t, _ = u_a.shape
    hw = A_HEADS * A_DV
    return pl.pallas_call(
        functools.partial(_gdn_prompt_kernel, tblk=tblk),
        grid=(b, t // tblk),
        in_specs=[pl.BlockSpec((1, tblk, UA_W), lambda i, j: (i, j, 0)),
                  _const_spec((CONV_W, A_CONV_CH)), _const_spec((1, LANE)), _const_spec((1, LANE)),
                  _const_spec((1, hw))],
        out_specs=[pl.BlockSpec((1, tblk, hw), lambda i, j: (i, j, 0)),
                   pl.BlockSpec((1, CONV_W - 1, A_CONV_CH), lambda i, j: (i, 0, 0)),
                   pl.BlockSpec((1, A_DK, hw), lambda i, j: (i, 0, 0))],
        out_shape=[jax.ShapeDtypeStruct((b, t, hw), F32),
                   jax.ShapeDtypeStruct((b, CONV_W - 1, A_CONV_CH), F32),
                   jax.ShapeDtypeStruct((b, A_DK, hw), F32)],
        scratch_shapes=[pltpu.VMEM((tblk + 8, A_CONV_CH), F32), pltpu.VMEM((A_DK, hw), F32)],
        compiler_params=_cparams(("parallel", "arbitrary")),
        name="gdn_prompt",
    )(u_a, cw, alog, dtb, anorm)


def _gla_prompt_kernel(u_ref, w2_ref, gb_ref, bnorm_ref, o_ref, s_out_ref, s_scr, *, tblk):
    t = pl.program_id(1)
    vw = B_HEADS * B_DV

    @pl.when(t == 0)
    def _():
        s_scr[...] = jnp.zeros_like(s_scr)

    u = u_ref[0]
    q = u[:, 0:B_QK] * (B_DK ** -0.5)
    k = u[:, B_QK:2 * B_QK]
    v = u[:, 2 * B_QK:2 * B_QK + B_V]
    rg = u[:, 2 * B_QK + B_V:2 * B_QK + 2 * B_V]
    glr = u[:, 2 * B_QK + 2 * B_V:]
    log_a = jax.nn.log_sigmoid(_bdot(glr, w2_ref[...]) + gb_ref[...]) / B_GATE_TAU

    c = CHUNK
    kmask = _block_mask(vw, B_QK, c, B_DK)
    vmask = _block_mask(vw, vw, c, B_DV)
    smask = _block_mask(vw, B_QK, B_DV, B_DK)
    row_k = _iota((c, B_QK), 0)
    row_s = _iota((SUB, vw), 0)
    col_s = _iota((SUB, vw), 1) % c
    st = s_scr[...]
    outs = []
    for ci in range(tblk // c):
        sl = slice(ci * c, (ci + 1) * c)
        qc, kc, vc = q[sl], k[sl], v[sl]
        b = _cumsum_rows(log_a[sl])
        o = _bdot_nt(qc * jnp.exp(b), st)
        att = []
        for i in range(c // SUB):
            r0 = i * SUB
            bref = b[r0:r0 + 1, :]
            qs = qc[r0:r0 + SUB] * jnp.exp(b[r0:r0 + SUB] - bref)
            ks = kc * jnp.exp(jnp.where(row_k < r0 + SUB, bref - b, -jnp.inf))
            a_i = _bdot_nt(qs, _block_diag(ks, kmask))
            att.append(jnp.where(row_s + r0 >= col_s, a_i, 0.0))
        o = o + _bdot(jnp.concatenate(att, axis=0), _block_diag(vc, vmask))
        outs.append(o)
        b_last = b[c - 1:c, :]
        kd = kc * jnp.exp(b_last - b)
        st = jnp.exp(b_last) * st + jnp.where(smask, _bdot_tn(vc, kd), 0.0)
    s_scr[...] = st

    o = jnp.concatenate(outs, axis=0)
    ones_bd = _block_mask(vw, vw, B_DV, B_DV).astype(BF16)
    o = o * lax.rsqrt(_seg_sum(o * o, ones_bd) * (1.0 / B_DV) + EPS) * bnorm_ref[...]
    o_ref[0] = o * _silu(rg)

    @pl.when(t == pl.num_programs(1) - 1)
    def _():
        s_out_ref[0] = st


def _gla_prompt(u_b, w2, gb, bnorm, *, tblk):
    b, t, _ = u_b.shape
    vw = B_HEADS * B_DV
    return pl.pallas_call(
        functools.partial(_gla_prompt_kernel, tblk=tblk),
        grid=(b, t // tblk),
        in_specs=[pl.BlockSpec((1, tblk, UB_W), lambda i, j: (i, j, 0)),
                  _const_spec((LANE, B_QK)), _const_spec((1, B_QK)), _const_spec((1, vw))],
        out_specs=[pl.BlockSpec((1, tblk, vw), lambda i, j: (i, j, 0)),
                   pl.BlockSpec((1, vw, B_QK), lambda i, j: (i, 0, 0))],
        out_shape=[jax.ShapeDtypeStruct((b, t, vw), F32),
                   jax.ShapeDtypeStruct((b, vw, B_QK), F32)],
        scratch_shapes=[pltpu.VMEM((vw, B_QK), F32)],
        compiler_params=_cparams(("parallel", "arbitrary")),
        name="gla_prompt",
    )(u_b, w2, gb, bnorm)


def _mla_prep_kernel(u_ref, cqn_ref, ckvn_ref, wuq_ref, wuk_ref, ck_ref, sk_ref, cq_ref, sq_ref,
                     ckv_o, kpe_o, ckvb_o, kpeb_o, qlat_o, qpe_o):
    u = u_ref[0]
    cq = _rms(u[:, 0:C_Q_LORA], cqn_ref[...]).astype(BF16)
    ckv = _rms(u[:, C_Q_LORA:C_Q_LORA + C_KV_LORA], ckvn_ref[...])
    half = C_ROPE // 2
    kx = u[:, C_Q_LORA + C_KV_LORA:]
    lane = _iota(kx.shape, 1)
    kswap = jnp.where(lane < half, pltpu.roll(kx, LANE - half, axis=1), pltpu.roll(kx, half, axis=1))
    kpe = (kx * ck_ref[...] + kswap * sk_ref[...])[:, 0:C_ROPE]
    ckv_o[0] = ckv
    kpe_o[0] = kpe
    ckvb_o[0] = ckv.astype(BF16)
    kpeb_o[0] = kpe.astype(BF16)

    qf = jnp.dot(cq, wuq_ref[...], preferred_element_type=F32)
    nope_w = C_HEADS * C_NOPE
    rope_w = C_HEADS * C_ROPE
    qr = qf[:, nope_w:]
    lane_r = _iota(qr.shape, 1) % C_ROPE
    qswap = jnp.where(lane_r < half, pltpu.roll(qr, rope_w - half, axis=1), pltpu.roll(qr, half, axis=1))
    qpe = qr * cq_ref[...] + qswap * sq_ref[...]
    for h in range(C_HEADS):
        qlat_o[0, h] = jnp.dot(qf[:, h * C_NOPE:(h + 1) * C_NOPE].astype(BF16), wuk_ref[h],
                               preferred_element_type=F32).astype(BF16)
        qpe_o[0, h] = qpe[:, h * C_ROPE:(h + 1) * C_ROPE].astype(BF16)


def _mla_prep(u_c, cqn, ckvn, wuq, wuk, tabs, *, tm):
    b, t, _ = u_c.shape
    ck, sk, cq, sq = tabs
    per_pos = ck.shape[0] != 1
    rope_w = C_HEADS * C_ROPE

    def tab_spec(w):
        if per_pos:
            return pl.BlockSpec((tm, w), lambda i, j: (j, 0))
        return _const_spec((1, w))

    def tok_spec(w):
        return pl.BlockSpec((1, tm, w), lambda i, j: (i, j, 0))

    def head_spec(w):
        return pl.BlockSpec((1, C_HEADS, tm, w), lambda i, j: (i, 0, j, 0))

    return pl.pallas_call(
        _mla_prep_kernel,
        grid=(b, t // tm),
        in_specs=[tok_spec(UC_W), _const_spec((1, C_Q_LORA)), _const_spec((1, C_KV_LORA)),
                  _const_spec(wuq.shape), _const_spec(wuk.shape),
                  tab_spec(LANE), tab_spec(LANE), tab_spec(rope_w), tab_spec(rope_w)],
        out_specs=[tok_spec(C_KV_LORA), tok_spec(C_ROPE), tok_spec(C_KV_LORA), tok_spec(C_ROPE),
                   head_spec(C_KV_LORA), head_spec(C_ROPE)],
        out_shape=[jax.ShapeDtypeStruct((b, t, C_KV_LORA), F32),
                   jax.ShapeDtypeStruct((b, t, C_ROPE), F32),
                   jax.ShapeDtypeStruct((b, t, C_KV_LORA), BF16),
                   jax.ShapeDtypeStruct((b, t, C_ROPE), BF16),
                   jax.ShapeDtypeStruct((b, C_HEADS, t, C_KV_LORA), BF16),
                   jax.ShapeDtypeStruct((b, C_HEADS, t, C_ROPE), BF16)],
        compiler_params=_cparams(("parallel", "parallel")),
        name="mla_prep",
    )(u_c, cqn, ckvn, wuq, wuk, ck, sk, cq, sq)


def _mla_flash_kernel(ql_ref, qp_ref, k_ref, p_ref, o_ref, m_scr, l_scr, acc_scr, *, tq, tk):
    i = pl.program_id(1)
    j = pl.program_id(2)
    last_j = (i * tq + tq - 1) // tk
    rows = C_HEADS * tq

    @pl.when(j == 0)
    def _():
        m_scr[...] = jnp.full_like(m_scr, -jnp.inf)
        l_scr[...] = jnp.zeros_like(l_scr)
        acc_scr[...] = jnp.zeros_like(acc_scr)

    @pl.when(j <= last_j)
    def _():
        ql = ql_ref[0].reshape(rows, C_KV_LORA)
        qp = qp_ref[0].reshape(rows, C_ROPE)
        kv = k_ref[0]
        s = (lax.dot_general(ql, kv, (((1,), (1,)), ((), ())), preferred_element_type=F32)
             + lax.dot_general(qp, p_ref[0], (((1,), (1,)), ((), ())), preferred_element_type=F32))
        s = s * MLA_SCALE
        qpos = i * tq + _iota((rows, tk), 0) % tq
        kpos = j * tk + _iota((rows, tk), 1)
        s = jnp.where(kpos <= qpos, s, -jnp.inf)
        m_old = m_scr[...]
        m_new = jnp.maximum(m_old, jnp.max(s, axis=-1, keepdims=True))
        alpha = jnp.exp(m_old - m_new)
        p = jnp.exp(s - m_new)
        l_scr[...] = alpha * l_scr[...] + jnp.sum(p, axis=-1, keepdims=True)
        acc_scr[...] = alpha * acc_scr[...] + jnp.dot(p.astype(BF16), kv, preferred_element_type=F32)
        m_scr[...] = m_new

    @pl.when(j == last_j)
    def _():
        o_ref[0] = (acc_scr[...] / l_scr[...]).reshape(C_HEADS, tq, C_KV_LORA).astype(BF16)


def _mla_flash(qlat, qpe, ckvb, kpeb, *, tq, tk):
    b, _, t, _ = qlat.shape
    rows = C_HEADS * tq

    def kmap(bi, i, j):
        return (bi, jnp.minimum(j, (i * tq + tq - 1) // tk), 0)

    return pl.pallas_call(
        functools.partial(_mla_flash_kernel, tq=tq, tk=tk),
        grid=(b, t // tq, t // tk),
        in_specs=[pl.BlockSpec((1, C_HEADS, tq, C_KV_LORA), lambda bi, i, j: (bi, 0, i, 0)),
                  pl.BlockSpec((1, C_HEADS, tq, C_ROPE), lambda bi, i, j: (bi, 0, i, 0)),
                  pl.BlockSpec((1, tk, C_KV_LORA), kmap),
                  pl.BlockSpec((1, tk, C_ROPE), kmap)],
        out_specs=pl.BlockSpec((1, C_HEADS, tq, C_KV_LORA), lambda bi, i, j: (bi, 0, i, 0)),
        out_shape=jax.ShapeDtypeStruct((b, C_HEADS, t, C_KV_LORA), BF16),
        scratch_shapes=[pltpu.VMEM((rows, 1), F32), pltpu.VMEM((rows, 1), F32),
                        pltpu.VMEM((rows, C_KV_LORA), F32)],
        compiler_params=_cparams(("parallel", "parallel", "arbitrary")),
        name="mla_flash",
    )(qlat, qpe, ckvb, kpeb)


def _mla_oproj_kernel(ol_ref, wuv_ref, o_ref):
    o_ref[0] = jnp.concatenate(
        [jnp.dot(ol_ref[0, h].astype(BF16), wuv_ref[h], preferred_element_type=F32)
         for h in range(C_HEADS)], axis=-1)


def _mla_oproj(olat, wuv, *, tm):
    b, _, t, _ = olat.shape
    return pl.pallas_call(
        _mla_oproj_kernel,
        grid=(b, t // tm),
        in_specs=[pl.BlockSpec((1, C_HEADS, tm, C_KV_LORA), lambda i, j: (i, 0, j, 0)),
                  _const_spec(wuv.shape)],
        out_specs=pl.BlockSpec((1, tm, C_HEADS * C_VDIM), lambda i, j: (i, j, 0)),
        out_shape=jax.ShapeDtypeStruct((b, t, C_HEADS * C_VDIM), F32),
        compiler_params=_cparams(("parallel", "parallel")),
        name="mla_oproj",
    )(olat, wuv)


def _mla_paged_kernel(pt_ref, ql_ref, qp_ref, cn_ref, pn_ref, ckv_hbm, kpe_hbm, o_ref,
                      kbuf, pbuf, sem, m_scr, l_scr, acc_scr, *, layer, ppc, nch, page):
    b = pl.program_id(0)
    c = pl.program_id(1)
    g = b * nch + c
    total = pl.num_programs(0) * nch
    slot = g % 2

    def copies(bb, cc, sl, p):
        pid = pt_ref[bb, cc * ppc + p]
        dst = pl.ds(pl.multiple_of(p * page, page), page)
        return (pltpu.make_async_copy(ckv_hbm.at[layer, pid], kbuf.at[sl, dst, :], sem.at[0, sl]),
                pltpu.make_async_copy(kpe_hbm.at[layer, pid], pbuf.at[sl, dst, :], sem.at[1, sl]))

    def issue(bb, cc, sl):
        def body(p, carry):
            for cp in copies(bb, cc, sl, p):
                cp.start()
            return carry
        lax.fori_loop(0, ppc, body, 0)

    @pl.when(g == 0)
    def _():
        issue(0, 0, 0)

    @pl.when(g + 1 < total)
    def _():
        issue((g + 1) // nch, (g + 1) % nch, 1 - slot)

    def wait_body(p, carry):
        for cp in copies(b, c, slot, p):
            cp.wait()
        return carry
    lax.fori_loop(0, ppc, wait_body, 0)

    @pl.when(c == 0)
    def _():
        m_scr[...] = jnp.full_like(m_scr, -jnp.inf)
        l_scr[...] = jnp.zeros_like(l_scr)
        acc_scr[...] = jnp.zeros_like(acc_scr)

    ql = ql_ref[0]
    qp = qp_ref[0]
    kv = kbuf[slot].astype(BF16)
    pe = pbuf[slot].astype(BF16)
    s = (lax.dot_general(ql, kv, (((1,), (1,)), ((), ())), preferred_element_type=F32)
         + lax.dot_general(qp, pe, (((1,), (1,)), ((), ())), preferred_element_type=F32)) * MLA_SCALE
    m_old = m_scr[...]
    m_new = jnp.maximum(m_old, jnp.max(s, axis=-1, keepdims=True))
    alpha = jnp.exp(m_old - m_new)
    p = jnp.exp(s - m_new)
    l_new = alpha * l_scr[...] + jnp.sum(p, axis=-1, keepdims=True)
    acc_new = alpha * acc_scr[...] + jnp.dot(p.astype(BF16), kv, preferred_element_type=F32)
    m_scr[...] = m_new
    l_scr[...] = l_new
    acc_scr[...] = acc_new

    @pl.when(c == nch - 1)
    def _():
        cn = cn_ref[0].astype(BF16).astype(F32)
        pn = pn_ref[0].astype(BF16).astype(F32)
        s_n = (jnp.sum(ql.astype(F32) * cn, axis=-1, keepdims=True)
               + jnp.sum(qp.astype(F32) * pn, axis=-1, keepdims=True)) * MLA_SCALE
        m_f = jnp.maximum(m_new, s_n)
        a_f = jnp.exp(m_new - m_f)
        p_n = jnp.exp(s_n - m_f)
        l_f = a_f * l_new + p_n
        acc_f = a_f * acc_new + p_n.astype(BF16).astype(F32) * cn
        o_ref[0] = acc_f / l_f


def _mla_paged(page_table, qlat, qpe, ckv_new, kpe_new, cache_ckv, cache_kpe, *, layer, ppc):
    b, hp, _ = qlat.shape
    n_pages = page_table.shape[1]
    page = cache_ckv.shape[2]
    nch = n_pages // ppc
    grid_spec = pltpu.PrefetchScalarGridSpec(
        num_scalar_prefetch=1,
        grid=(b, nch),
        in_specs=[pl.BlockSpec((1, hp, C_KV_LORA), lambda i, j, pt: (i, 0, 0)),
                  pl.BlockSpec((1, hp, C_ROPE), lambda i, j, pt: (i, 0, 0)),
                  pl.BlockSpec((1, 1, C_KV_LORA), lambda i, j, pt: (i, 0, 0)),
                  pl.BlockSpec((1, 1, C_ROPE), lambda i, j, pt: (i, 0, 0)),
                  pl.BlockSpec(memory_space=pl.ANY),
                  pl.BlockSpec(memory_space=pl.ANY)],
        out_specs=pl.BlockSpec((1, hp, C_KV_LORA), lambda i, j, pt: (i, 0, 0)),
        scratch_shapes=[pltpu.VMEM((2, ppc * page, C_KV_LORA), F32),
                        pltpu.VMEM((2, ppc * page, C_ROPE), F32),
                        pltpu.SemaphoreType.DMA((2, 2)),
                        pltpu.VMEM((hp, 1), F32), pltpu.VMEM((hp, 1), F32),
                        pltpu.VMEM((hp, C_KV_LORA), F32)],
    )
    return pl.pallas_call(
        functools.partial(_mla_paged_kernel, layer=layer, ppc=ppc, nch=nch, page=page),
        grid_spec=grid_spec,
        out_shape=jax.ShapeDtypeStruct((b, hp, C_KV_LORA), F32),
        compiler_params=_cparams(("arbitrary", "arbitrary")),
        name="mla_paged",
    )(page_table, qlat, qpe, ckv_new, kpe_new, cache_ckv, cache_kpe)


def _xattn_prompt_kernel(x_ref, nw_ref, wq_ref, wo_ref, mk_ref, mv_ref, o_ref):
    x = x_ref[0]
    xn = _rms(x, nw_ref[...]).astype(BF16)
    q = jnp.dot(xn, wq_ref[...], preferred_element_type=F32).astype(BF16)
    heads = []
    for h in range(X_HEADS):
        sl = slice(h * X_HDIM, (h + 1) * X_HDIM)
        s = lax.dot_general(q[:, sl], mk_ref[0, :, sl], (((1,), (1,)), ((), ())),
                            preferred_element_type=F32) * (X_HDIM ** -0.5)
        e = jnp.exp(s - jnp.max(s, axis=-1, keepdims=True))
        p = (e / jnp.sum(e, axis=-1, keepdims=True)).astype(BF16)
        heads.append(jnp.dot(p, mv_ref[0, :, sl], preferred_element_type=F32).astype(BF16))
    o_ref[0] = x + jnp.dot(jnp.concatenate(heads, axis=-1), wo_ref[...], preferred_element_type=F32)


def _xattn_prompt(x, nw, wq, wo, mk, mv, *, tm):
    b, t, d = x.shape
    n_mem = mk.shape[1]
    return pl.pallas_call(
        _xattn_prompt_kernel,
        grid=(b, t // tm),
        in_specs=[pl.BlockSpec((1, tm, d), lambda i, j: (i, j, 0)), _const_spec((1, d)),
                  _const_spec(wq.shape), _const_spec(wo.shape),
                  pl.BlockSpec((1, n_mem, d), lambda i, j: (i, 0, 0)),
                  pl.BlockSpec((1, n_mem, d), lambda i, j: (i, 0, 0))],
        out_specs=pl.BlockSpec((1, tm, d), lambda i, j: (i, j, 0)),
        out_shape=jax.ShapeDtypeStruct((b, t, d), F32),
        compiler_params=_cparams(("parallel", "parallel")),
        name="xattn_prompt",
    )(x, nw, wq, wo, mk, mv)


def _xattn_decode_kernel(q_ref, mk_ref, mv_ref, o_ref):
    d = X_HEADS * X_HDIM
    hp = 8
    q = q_ref[0]
    head_of_lane = _iota((hp, d), 1) // X_HDIM
    qbd = jnp.where(head_of_lane == _iota((hp, d), 0), q, 0.0)
    s = _bdot_nt(qbd, mk_ref[0]) * (X_HDIM ** -0.5)
    e = jnp.exp(s - jnp.max(s, axis=-1, keepdims=True))
    p = e / jnp.sum(e, axis=-1, keepdims=True)
    o8 = _bdot(p, mv_ref[0])
    o_ref[0] = jnp.sum(jnp.where(head_of_lane == _iota((hp, d), 0), o8, 0.0), axis=0, keepdims=True)


def _xattn_decode(q, mk, mv):
    b, n_mem, d = mk.shape
    return pl.pallas_call(
        _xattn_decode_kernel,
        grid=(b,),
        in_specs=[pl.BlockSpec((1, 1, d), lambda i: (i, 0, 0)),
                  pl.BlockSpec((1, n_mem, d), lambda i: (i, 0, 0)),
                  pl.BlockSpec((1, n_mem, d), lambda i: (i, 0, 0))],
        out_specs=pl.BlockSpec((1, 1, d), lambda i: (i, 0, 0)),
        out_shape=jax.ShapeDtypeStruct((b, 1, d), F32),
        compiler_params=_cparams(("parallel",)),
        name="xattn_decode",
    )(q, mk, mv)


def _columns(rows):
    w = rows[0].shape[1]
    mat = jnp.concatenate(rows + [jnp.zeros((LANE - len(rows), w), F32)], axis=0)
    return mat.T


def _gdn_decode_kernel(u_ref, cs_ref, s_ref, cw_ref, alog_ref, dtb_ref, anorm_ref,
                       o_ref, cs_out_ref, s_out_ref):
    x = u_ref[0, :, 0:A_CONV_CH]
    z = u_ref[0, :, A_CONV_CH:A_CONV_CH + A_V]
    ba = u_ref[0, :, A_CONV_CH + A_V:UA_W]
    cs = cs_ref[0]
    cw = cw_ref[...]
    y = cw[0:1] * cs[0:1] + cw[1:2] * cs[1:2] + cw[2:3] * cs[2:3] + cw[3:4] * x
    cs_out_ref[0] = jnp.concatenate([cs[1:CONV_W - 1], x], axis=0)
    qkv = _silu(y)
    beta = jax.nn.sigmoid(ba)
    g = -jnp.exp(alog_ref[...]) * jax.nn.softplus(ba + dtb_ref[...])
    qs, ks, vs = [], [], []
    for h in range(A_HEADS):
        qh = qkv[:, h * A_DK:(h + 1) * A_DK]
        kh = qkv[:, A_QK + h * A_DK:A_QK + (h + 1) * A_DK]
        qs.append(qh * lax.rsqrt(jnp.sum(qh * qh, axis=-1, keepdims=True) + EPS) * (A_DK ** -0.5))
        ks.append(kh * lax.rsqrt(jnp.sum(kh * kh, axis=-1, keepdims=True) + EPS))
        vs.append(qkv[:, 2 * A_QK + h * A_DV:2 * A_QK + (h + 1) * A_DV])
    cols = _columns(ks + qs)
    outs = []
    for h in range(A_HEADS):
        s = s_ref[0, h]
        kcol = cols[:, h:h + 1]
        qcol = cols[:, A_HEADS + h:A_HEADS + h + 1]
        bh = beta[:, h:h + 1]
        eg = jnp.exp(g[:, A_HEADS + h:A_HEADS + h + 1])
        ks_row = jnp.sum(kcol * s, axis=0, keepdims=True)
        qs_row = jnp.sum(qcol * s, axis=0, keepdims=True)
        r = vs[h] * bh - (bh * eg) * ks_row
        qk = jnp.sum(qs[h] * ks[h], axis=-1, keepdims=True)
        o = eg * qs_row + qk * r
        s_out_ref[0, h] = eg * s + kcol * r
        o = o * lax.rsqrt(jnp.mean(o * o, axis=-1, keepdims=True) + EPS) * anorm_ref[...]
        outs.append(o)
    o_ref[0] = jnp.concatenate(outs, axis=-1) * _silu(z)


def _gdn_decode(u_a, cs, s, cw, alog, dtb, anorm):
    b = u_a.shape[0]
    return pl.pallas_call(
        _gdn_decode_kernel,
        grid=(b,),
        in_specs=[pl.BlockSpec((1, 1, UA_W), lambda i: (i, 0, 0)),
                  pl.BlockSpec((1, CONV_W - 1, A_CONV_CH), lambda i: (i, 0, 0)),
                  pl.BlockSpec((1, A_HEADS, A_DK, A_DV), lambda i: (i, 0, 0, 0)),
                  _const_spec((CONV_W, A_CONV_CH)), _const_spec((1, LANE)), _const_spec((1, LANE)),
                  _const_spec((1, A_DV))],
        out_specs=[pl.BlockSpec((1, 1, A_V), lambda i: (i, 0, 0)),
                   pl.BlockSpec((1, CONV_W - 1, A_CONV_CH), lambda i: (i, 0, 0)),
                   pl.BlockSpec((1, A_HEADS, A_DK, A_DV), lambda i: (i, 0, 0, 0))],
        out_shape=[jax.ShapeDtypeStruct((b, 1, A_V), F32),
                   jax.ShapeDtypeStruct((b, CONV_W - 1, A_CONV_CH), F32),
                   jax.ShapeDtypeStruct((b, A_HEADS, A_DK, A_DV), F32)],
        compiler_params=_cparams(("parallel",)),
        name="gdn_decode",
    )(u_a, cs, s, cw, alog, dtb, anorm)


def _gla_decode_kernel(u_ref, s_ref, w2_ref, gb_ref, bnorm_ref, o_ref, s_out_ref):
    u = u_ref[0]
    q = u[:, 0:B_QK] * (B_DK ** -0.5)
    k = u[:, B_QK:2 * B_QK]
    rg = u[:, 2 * B_QK + B_V:2 * B_QK + 2 * B_V]
    glr = jnp.broadcast_to(u[:, 2 * B_QK + 2 * B_V:], (8, LANE))
    log_a = jax.nn.log_sigmoid(_bdot(glr, w2_ref[...])[0:1] + gb_ref[...]) / B_GATE_TAU
    a = jnp.exp(log_a)
    cols = _columns([k, q * a, a])
    outs = []
    for h in range(B_HEADS):
        rs = slice(h * B_DK, (h + 1) * B_DK)
        s = s_ref[0, h]
        vh = u[:, 2 * B_QK + h * B_DV:2 * B_QK + (h + 1) * B_DV]
        qk = jnp.sum(q[:, rs] * k[:, rs], axis=-1, keepdims=True)
        o = jnp.sum(cols[rs, 1:2] * s, axis=0, keepdims=True) + qk * vh
        s_out_ref[0, h] = cols[rs, 2:3] * s + cols[rs, 0:1] * vh
        o = o * lax.rsqrt(jnp.mean(o * o, axis=-1, keepdims=True) + EPS) * bnorm_ref[...]
        outs.append(o)
    o_ref[0] = jnp.concatenate(outs, axis=-1) * _silu(rg)


def _gla_decode(u_b, s, w2, gb, bnorm):
    b = u_b.shape[0]
    return pl.pallas_call(
        _gla_decode_kernel,
        grid=(b,),
        in_specs=[pl.BlockSpec((1, 1, UB_W), lambda i: (i, 0, 0)),
                  pl.BlockSpec((1, B_HEADS, B_DK, B_DV), lambda i: (i, 0, 0, 0)),
                  _const_spec((LANE, B_QK)), _const_spec((1, B_QK)), _const_spec((1, B_DV))],
        out_specs=[pl.BlockSpec((1, 1, B_V), lambda i: (i, 0, 0)),
                   pl.BlockSpec((1, B_HEADS, B_DK, B_DV), lambda i: (i, 0, 0, 0))],
        out_shape=[jax.ShapeDtypeStruct((b, 1, B_V), F32),
                   jax.ShapeDtypeStruct((b, B_HEADS, B_DK, B_DV), F32)],
        compiler_params=_cparams(("parallel",)),
        name="gla_decode",
    )(u_b, s, w2, gb, bnorm)


def _pad_cols(w, n):
    return jnp.pad(w, ((0, 0), (0, n - w.shape[1])))


def _rope_tables(pos):
    half = C_ROPE // 2
    inv = ROPE_THETA ** (-jnp.arange(half, dtype=F32) / half)
    ang = pos.astype(F32)[:, None] * inv[None, :]
    cos, sin = jnp.cos(ang), jnp.sin(ang)
    zero = jnp.zeros_like(cos)
    ck = jnp.concatenate([cos, cos, zero, zero], axis=-1)
    sk = jnp.concatenate([-sin, sin, zero, zero], axis=-1)
    cq = jnp.tile(jnp.concatenate([cos, cos], axis=-1), (1, C_HEADS))
    sq = jnp.tile(jnp.concatenate([-sin, sin], axis=-1), (1, C_HEADS))
    return ck, sk, cq, sq


def _layer_weights(l, P):
    w_in = P['w_in'][l]
    o_b, o_c = A_IN, A_IN + B_IN
    w_a = jnp.concatenate([w_in[:, :A_CONV_CH + A_V], _pad_cols(w_in[:, A_CONV_CH + A_V:A_IN], LANE)], axis=1)
    w_b = jnp.concatenate([w_in[:, o_b:o_b + 2 * B_QK + 2 * B_V],
                           _pad_cols(w_in[:, o_b + 2 * B_QK + 2 * B_V:o_c], LANE)], axis=1)
    w_c = jnp.concatenate([w_in[:, o_c:o_c + C_Q_LORA + C_KV_LORA],
                           _pad_cols(w_in[:, o_c + C_Q_LORA + C_KV_LORA:], LANE)], axis=1)
    wuq = P['c_w_uq'][l].reshape(C_Q_LORA, C_HEADS, C_NOPE + C_ROPE)
    wuq = jnp.concatenate([wuq[:, :, :C_NOPE].reshape(C_Q_LORA, -1),
                           wuq[:, :, C_NOPE:].reshape(C_Q_LORA, -1)], axis=1)
    w_out = P['w_out'][l]
    row = lambda a: a.reshape(1, -1).astype(F32)
    head_lanes = lambda a: jnp.pad(a.reshape(1, -1).astype(F32), ((0, 0), (A_HEADS, LANE - 2 * A_HEADS)))
    return dict(
        norm_ffn1=row(P['norm_ffn1'][l]), norm_ffn2=row(P['norm_ffn2'][l]),
        ffn1=tuple(P[n][l].astype(BF16) for n in ('w_ffn1_gate', 'w_ffn1_up', 'w_ffn1_down')),
        ffn2=tuple(P[n][l].astype(BF16) for n in ('w_ffn2_gate', 'w_ffn2_up', 'w_ffn2_down')),
        norm_mix=row(P['norm_mix'][l]),
        w_in=(w_a.astype(BF16), w_b.astype(BF16), w_c.astype(BF16)),
        a_conv_w=P['a_conv_w'][l].astype(F32),
        a_log=head_lanes(P['a_log'][l]), a_dt_bias=head_lanes(P['a_dt_bias'][l]),
        a_norm=row(P['a_norm'][l]), a_norm_t=row(jnp.tile(P['a_norm'][l], A_HEADS)),
        b_w2=jnp.pad(P['b_gate_w2'][l], ((0, LANE - B_GATE_RANK), (0, 0))).astype(BF16),
        b_gate_bias=row(P['b_gate_bias'][l]),
        b_norm=row(P['b_norm'][l]), b_norm_t=row(jnp.tile(P['b_norm'][l], B_HEADS)),
        c_q_norm=row(P['c_q_norm'][l]), c_kv_norm=row(P['c_kv_norm'][l]),
        c_w_uq=wuq.astype(BF16),
        c_w_uk=jnp.transpose(P['c_w_uk'][l], (1, 2, 0)).astype(BF16),
        c_w_uv=jnp.transpose(P['c_w_uv'][l], (1, 0, 2)).astype(BF16),
        w_out=(w_out[:A_V].astype(BF16), w_out[A_V:A_V + B_V].astype(BF16), w_out[A_V + B_V:].astype(BF16)),
        norm_x=row(P['norm_x'][l]), norm_mem=row(P['norm_mem'][l]),
        w_xq=P['w_xq'][l].astype(BF16), w_xo=P['w_xo'][l].astype(BF16),
        w_xk=P['w_xk'][l].astype(BF16), w_xv=P['w_xv'][l].astype(BF16),
    )


def _pick(n, pref):
    return pref if n % pref == 0 else n


def kernel(x_prompt, x_sample, mem_prompt, cache_ckv, cache_kpe, page_table, state_conv_a, state_delta, state_gla, cache_mem_k, cache_mem_v, norm_ffn1, w_ffn1_gate, w_ffn1_up, w_ffn1_down, norm_mix, w_in, a_conv_w, a_log, a_dt_bias, a_norm, b_gate_w2, b_gate_bias, b_norm, c_q_norm, c_w_uq, c_kv_norm, c_w_uk, c_w_uv, w_out, norm_x, norm_mem, w_xq, w_xk, w_xv, w_xo, norm_ffn2, w_ffn2_gate, w_ffn2_up, w_ffn2_down, final_norm):
    P = dict(norm_ffn1=norm_ffn1, w_ffn1_gate=w_ffn1_gate, w_ffn1_up=w_ffn1_up, w_ffn1_down=w_ffn1_down,
             norm_mix=norm_mix, w_in=w_in, a_conv_w=a_conv_w, a_log=a_log, a_dt_bias=a_dt_bias,
             a_norm=a_norm, b_gate_w2=b_gate_w2, b_gate_bias=b_gate_bias, b_norm=b_norm,
             c_q_norm=c_q_norm, c_w_uq=c_w_uq, c_kv_norm=c_kv_norm, c_w_uk=c_w_uk, c_w_uv=c_w_uv,
             w_out=w_out, norm_x=norm_x, norm_mem=norm_mem, w_xq=w_xq, w_xk=w_xk, w_xv=w_xv, w_xo=w_xo,
             norm_ffn2=norm_ffn2, w_ffn2_gate=w_ffn2_gate, w_ffn2_up=w_ffn2_up, w_ffn2_down=w_ffn2_down)
    depth = w_in.shape[0]
    W = [_layer_weights(l, P) for l in range(depth)]
    fnorm = final_norm.reshape(1, -1).astype(F32)

    bp, tp, d = x_prompt.shape
    mp = bp * tp
    n_mem = mem_prompt.shape[1]
    tm = _pick(tp, 512)
    tabs_p = _rope_tables(jnp.arange(tp, dtype=jnp.int32))
    x = x_prompt.reshape(mp, d)
    mem = mem_prompt.reshape(bp * n_mem, d)
    p_ckv, p_kpe, p_conv, p_delta, p_gla, p_mk, p_mv = [], [], [], [], [], [], []
    for l in range(depth):
        w = W[l]
        mk, mv = _norm_mm(mem, w['norm_mem'], [w['w_xk'], w['w_xv']], tm=_pick(bp * n_mem, 512))
        x = _ffn(x, w['norm_ffn1'], *w['ffn1'], fnorm, final=False, tm=tm)
        u_a, u_b, u_c = _norm_mm(x, w['norm_mix'], list(w['w_in']), tm=tm)
        o_a, conv_new, sd = _gdn_prompt(u_a.reshape(bp, tp, UA_W), w['a_conv_w'], w['a_log'], w['a_dt_bias'],
                                        w['a_norm_t'], tblk=_pick(tp, 256))
        o_b, sg = _gla_prompt(u_b.reshape(bp, tp, UB_W), w['b_w2'], w['b_gate_bias'], w['b_norm_t'],
                              tblk=_pick(tp, 256))
        ckv, kpe, ckvb, kpeb, qlat, qpe = _mla_prep(u_c.reshape(bp, tp, UC_W), w['c_q_norm'], w['c_kv_norm'],
                                                    w['c_w_uq'], w['c_w_uk'], tabs_p, tm=tm)
        olat = _mla_flash(qlat, qpe, ckvb, kpeb, tq=_pick(tp, 512), tk=_pick(tp, 512))
        o_c = _mla_oproj(olat, w['c_w_uv'], tm=tm)
        x = _mm_res(x, [o_a.reshape(mp, A_V), o_b.reshape(mp, B_V), o_c.reshape(mp, C_HEADS * C_VDIM)],
                    list(w['w_out']), tm=tm)
        x = _xattn_prompt(x.reshape(bp, tp, d), w['norm_x'], w['w_xq'], w['w_xo'],
                          mk.reshape(bp, n_mem, d).astype(BF16), mv.reshape(bp, n_mem, d).astype(BF16),
                          tm=tm).reshape(mp, d)
        x = _ffn(x, w['norm_ffn2'], *w['ffn2'], fnorm, final=(l == depth - 1), tm=tm)
        p_ckv.append(ckv)
        p_kpe.append(kpe)
        p_conv.append(conv_new)
        p_delta.append(sd.reshape(bp, A_DK, A_HEADS, A_DV).transpose(0, 2, 1, 3))
        sg = sg.reshape(bp, B_HEADS, B_DV, B_HEADS, B_DK)
        p_gla.append(jnp.stack([sg[:, h, :, h, :] for h in range(B_HEADS)], axis=1).transpose(0, 1, 3, 2))
        p_mk.append(mk.reshape(bp, n_mem, X_HEADS, X_HDIM))
        p_mv.append(mv.reshape(bp, n_mem, X_HEADS, X_HDIM))
    y_prompt = x.reshape(bp, tp, d)

    bs, ts, _ = x_sample.shape
    n_pages, page = page_table.shape[1], cache_ckv.shape[2]
    past_len = n_pages * page
    tabs_s = _rope_tables(past_len + jnp.arange(ts, dtype=jnp.int32))
    x = x_sample.reshape(bs, d)
    s_ckv, s_kpe, s_conv, s_delta, s_gla = [], [], [], [], []
    for l in range(depth):
        w = W[l]
        x = _ffn(x, w['norm_ffn1'], *w['ffn1'], fnorm, final=False, tm=bs)
        u_a, u_b, u_c = _norm_mm(x, w['norm_mix'], list(w['w_in']), tm=bs)
        o_a, conv_new, sd = _gdn_decode(u_a.reshape(bs, 1, UA_W), state_conv_a[l], state_delta[l],
                                        w['a_conv_w'], w['a_log'], w['a_dt_bias'], w['a_norm'])
        o_b, sg = _gla_decode(u_b.reshape(bs, 1, UB_W), state_gla[l], w['b_w2'], w['b_gate_bias'], w['b_norm'])
        ckv, kpe, _, _, qlat, qpe = _mla_prep(u_c.reshape(1, bs, UC_W), w['c_q_norm'], w['c_kv_norm'],
                                              w['c_w_uq'], w['c_w_uk'], tabs_s, tm=bs)
        pad_heads = lambda a: jnp.pad(a[0].transpose(1, 0, 2), ((0, 0), (0, 8 - C_HEADS), (0, 0)))
        olat = _mla_paged(page_table, pad_heads(qlat), pad_heads(qpe), ckv.reshape(bs, 1, C_KV_LORA),
                          kpe.reshape(bs, 1, C_ROPE), cache_ckv, cache_kpe, layer=l, ppc=_pick(n_pages, 32))
        o_c = _mla_oproj(olat[:, :C_HEADS].transpose(1, 0, 2)[None], w['c_w_uv'], tm=bs)
        x = _mm_res(x, [o_a.reshape(bs, A_V), o_b.reshape(bs, B_V), o_c.reshape(bs, C_HEADS * C_VDIM)],
                    list(w['w_out']), tm=bs)
        (q,) = _norm_mm(x, w['norm_x'], [w['w_xq']], tm=bs)
        att = _xattn_decode(q.reshape(bs, 1, d), cache_mem_k[l].reshape(bs, n_mem, d),
                            cache_mem_v[l].reshape(bs, n_mem, d))
        x = _mm_res(x, [att.reshape(bs, d)], [w['w_xo']], tm=bs)
        x = _ffn(x, w['norm_ffn2'], *w['ffn2'], fnorm, final=(l == depth - 1), tm=bs)
        s_ckv.append(ckv.reshape(bs, ts, C_KV_LORA))
        s_kpe.append(kpe.reshape(bs, ts, C_ROPE))
        s_conv.append(conv_new)
        s_delta.append(sd)
        s_gla.append(sg)
    y_sample = x.reshape(bs, ts, d)

    return (y_prompt, y_sample,
            jnp.stack(p_ckv), jnp.stack(p_kpe), jnp.stack(p_conv), jnp.stack(p_delta), jnp.stack(p_gla),
            jnp.stack(p_mk), jnp.stack(p_mv),
            jnp.stack(s_ckv), jnp.stack(s_kpe), jnp.stack(s_conv), jnp.stack(s_delta), jnp.stack(s_gla))
```

```python
import functools

import jax
import jax.numpy as jnp
import numpy as np
from jax import lax
from jax.experimental import pallas as pl
from jax.experimental.pallas import tpu as pltpu

F32 = jnp.float32
BF16 = jnp.bfloat16

D_MODEL = 1024
A_HEADS, A_DK, A_DV, CONV_W = 4, 64, 64, 4
B_HEADS, B_DK, B_DV, B_GATE_RANK, B_GATE_TAU = 4, 32, 64, 16, 16.0
C_HEADS, C_NOPE, C_ROPE, C_VDIM, C_Q_LORA, C_KV_LORA = 4, 128, 64, 128, 384, 256
ROPE_THETA = 10000.0
X_HEADS, X_HDIM = 4, 256
D_FF = 2816
CHUNK = 64
EPS = 1e-6

A_QK = A_HEADS * A_DK
A_V = A_HEADS * A_DV
A_CONV_CH = 2 * A_QK + A_V
A_IN = A_CONV_CH + A_V + 2 * A_HEADS
B_QK = B_HEADS * B_DK
B_V = B_HEADS * B_DV
B_IN = 2 * B_QK + 2 * B_V + B_GATE_RANK
C_IN = C_Q_LORA + C_KV_LORA + C_ROPE
MLA_SCALE = (C_NOPE + C_ROPE) ** -0.5

LANE = 128
UA_W = A_CONV_CH + A_V + LANE
UB_W = 2 * B_QK + 2 * B_V + LANE
UC_W = C_Q_LORA + C_KV_LORA + LANE
SUB = 16
LOG2E = 1.4426950408889634
VMEM_LIMIT = 56 * 1024 * 1024


def _cparams(sem):
    return pltpu.CompilerParams(dimension_semantics=sem, vmem_limit_bytes=VMEM_LIMIT)


def _const_spec(shape):
    nd = len(shape)
    return pl.BlockSpec(shape, lambda *_: (0,) * nd, pipeline_mode=pl.Buffered(1))


def _rms(x, w):
    return x * lax.rsqrt(jnp.mean(x * x, axis=-1, keepdims=True) + EPS) * w


def _silu(x):
    return x * jax.nn.sigmoid(x)


def _bdot(a, b):
    return jnp.dot(a.astype(BF16), b.astype(BF16), preferred_element_type=F32)


def _bdot_nt(a, b):
    return lax.dot_general(a.astype(BF16), b.astype(BF16), (((1,), (1,)), ((), ())),
                           preferred_element_type=F32)


def _bdot_tn(a, b):
    return lax.dot_general(a.astype(BF16), b.astype(BF16), (((0,), (0,)), ((), ())),
                           preferred_element_type=F32)


def _iota(shape, dim):
    return lax.broadcasted_iota(jnp.int32, shape, dim)


def _block_mask(rows, cols, rb, cb):
    return (_iota((rows, cols), 0) // rb) == (_iota((rows, cols), 1) // cb)


def _block_diag(x, mask):
    n = mask.shape[0] // x.shape[0]
    xb = x.astype(BF16)
    return jnp.where(mask, jnp.concatenate([xb] * n, axis=0), jnp.zeros_like(xb[:1, :1]))


def _diag_blocks(m, rb, cb):
    n = m.shape[0] // rb
    lane_blk = _iota((rb, m.shape[1]), 1) // cb
    out = jnp.zeros((rb, m.shape[1]), m.dtype)
    for h in range(n):
        out = jnp.where(lane_blk == h, m[h * rb:(h + 1) * rb, :], out)
    return out


def _seg_sum(x, ones_bd):
    hi = x.astype(BF16)
    lo = (x - hi.astype(F32)).astype(BF16)
    return (jnp.dot(hi, ones_bd, preferred_element_type=F32)
            + jnp.dot(lo, ones_bd, preferred_element_type=F32))


def _expand_heads(x, off, nh, w):
    lane_blk = _iota((x.shape[0], nh * w), 1) // w
    out = jnp.zeros((x.shape[0], nh * w), x.dtype)
    for h in range(nh):
        out = jnp.where(lane_blk == h, x[:, off + h:off + h + 1], out)
    return out


def _cumsum_rows(x):
    n = x.shape[0]
    row = _iota(x.shape, 0)
    s = 1
    while s < n:
        x = x + jnp.where(row >= s, pltpu.roll(x, s, axis=0), 0.0)
        s *= 2
    return x


def _ffn_kernel(x_ref, nw_ref, wg_ref, wu_ref, wd_ref, fn_ref, o_ref, *, final):
    x = x_ref[...]
    xn = _rms(x, nw_ref[...]).astype(BF16)
    g = jnp.dot(xn, wg_ref[...], preferred_element_type=F32)
    u = jnp.dot(xn, wu_ref[...], preferred_element_type=F32)
    h = (_silu(g) * u).astype(BF16)
    y = x + 0.5 * jnp.dot(h, wd_ref[...], preferred_element_type=F32)
    if final:
        y = _rms(y, fn_ref[...])
    o_ref[...] = y


def _ffn(x, nw, wg, wu, wd, fn, *, final, tm):
    m = x.shape[0]
    return pl.pallas_call(
        functools.partial(_ffn_kernel, final=final),
        grid=(m // tm,),
        in_specs=[pl.BlockSpec((tm, D_MODEL), lambda i: (i, 0)),
                  _const_spec((1, D_MODEL)),
                  _const_spec((D_MODEL, D_FF)), _const_spec((D_MODEL, D_FF)),
                  _const_spec((D_FF, D_MODEL)), _const_spec((1, D_MODEL))],
        out_specs=pl.BlockSpec((tm, D_MODEL), lambda i: (i, 0)),
        out_shape=jax.ShapeDtypeStruct((m, D_MODEL), F32),
        compiler_params=_cparams(("parallel",)),
        name="ffn",
    )(x, nw, wg, wu, wd, fn)


def _norm_mm_kernel(x_ref, nw_ref, *refs, n_w):
    xn = _rms(x_ref[...], nw_ref[...]).astype(BF16)
    for w_ref, o_ref in zip(refs[:n_w], refs[n_w:]):
        o_ref[...] = jnp.dot(xn, w_ref[...], preferred_element_type=F32)


def _norm_mm(x, nw, ws, *, tm):
    m, d = x.shape
    return pl.pallas_call(
        functools.partial(_norm_mm_kernel, n_w=len(ws)),
        grid=(m // tm,),
        in_specs=[pl.BlockSpec((tm, d), lambda i: (i, 0)), _const_spec((1, d))]
                 + [_const_spec(w.shape) for w in ws],
        out_specs=[pl.BlockSpec((tm, w.shape[1]), lambda i: (i, 0)) for w in ws],
        out_shape=[jax.ShapeDtypeStruct((m, w.shape[1]), F32) for w in ws],
        compiler_params=_cparams(("parallel",)),
        name="norm_mm",
    )(x, nw, *ws)


def _mm_res_kernel(x_ref, *refs, n_a):
    acc = x_ref[...]
    for a_ref, w_ref in zip(refs[:n_a], refs[n_a:2 * n_a]):
        acc = acc + jnp.dot(a_ref[...].astype(BF16), w_ref[...], preferred_element_type=F32)
    refs[2 * n_a][...] = acc


def _mm_res(x, a_list, w_list, *, tm):
    m, d = x.shape
    n_a = len(a_list)
    return pl.pallas_call(
        functools.partial(_mm_res_kernel, n_a=n_a),
        grid=(m // tm,),
        in_specs=[pl.BlockSpec((tm, d), lambda i: (i, 0))]
                 + [pl.BlockSpec((tm, a.shape[1]), lambda i: (i, 0)) for a in a_list]
                 + [_const_spec(w.shape) for w in w_list],
        out_specs=pl.BlockSpec((tm, d), lambda i: (i, 0)),
        out_shape=jax.ShapeDtypeStruct((m, d), F32),
        compiler_params=_cparams(("parallel",)),
        name="mm_res",
    )(x, *a_list, *w_list)


def _gdn_prompt_kernel(u_ref, cw_ref, alog_ref, dtb_ref, anorm_ref, o_ref, conv_ref, s_out_ref,
                       xbuf, s_scr, *, tblk):
    t = pl.program_id(1)
    hw = A_HEADS * A_DV

    @pl.when(t == 0)
    def _():
        xbuf[0:8, :] = jnp.zeros((8, A_CONV_CH), F32)
        s_scr[...] = jnp.zeros_like(s_scr)

    x = u_ref[0, :, 0:A_CONV_CH]
    xbuf[8:8 + tblk, :] = x
    cw = cw_ref[...]
    y = (cw[0:1] * xbuf[5:5 + tblk, :] + cw[1:2] * xbuf[6:6 + tblk, :]
         + cw[2:3] * xbuf[7:7 + tblk, :] + cw[3:4] * x)
    xbuf[0:8, :] = x[tblk - 8:tblk, :]
    qkv = _silu(y)
    z = u_ref[0, :, A_CONV_CH:A_CONV_CH + A_V]
    ba = u_ref[0, :, A_CONV_CH + A_V:UA_W]

    ones_bd = _block_mask(hw, hw, A_DK, A_DK).astype(BF16)
    q = qkv[:, 0:A_QK]
    k = qkv[:, A_QK:2 * A_QK]
    v = qkv[:, 2 * A_QK:]
    q = q * lax.rsqrt(_seg_sum(q * q, ones_bd) + EPS) * (A_DK ** -0.5)
    k = k * lax.rsqrt(_seg_sum(k * k, ones_bd) + EPS)
    beta = _expand_heads(jax.nn.sigmoid(ba), 0, A_HEADS, A_DV)
    g = _expand_heads(-jnp.exp(alog_ref[...]) * jax.nn.softplus(ba + dtb_ref[...]),
                      A_HEADS, A_HEADS, A_DV)

    c = CHUNK
    bd = _block_mask(hw, hw, c, c)
    row = _iota((c, hw), 0)
    col = _iota((c, hw), 1) % c
    incl = row >= col
    strict = row > col
    s = s_scr[...]
    outs = []
    for ci in range(tblk // c):
        sl = slice(ci * c, (ci + 1) * c)
        qc, kc, vc, bc = q[sl], k[sl], v[sl], beta[sl]
        gcum = _cumsum_rows(g[sl])
        grow = jnp.sum(jnp.where(row == col, gcum, 0.0), axis=0, keepdims=True)
        decay = jnp.exp(jnp.where(incl, gcum - grow, -jnp.inf))
        eg = jnp.exp(gcum)
        kb = kc * bc
        aq = _bdot_nt(jnp.concatenate([kb, qc], axis=0), _block_diag(kc, bd))
        a = jnp.where(strict, aq[:c] * decay, 0.0)
        qk = aq[c:] * decay
        p = -a
        pw = _bdot(a, _block_diag(a, bd))
        n_sq = int(np.log2(c)) - 1
        for n in range(n_sq):
            if n < n_sq - 1:
                both = _bdot(jnp.concatenate([p, pw], axis=0), _block_diag(pw, bd))
                p = p + pw + both[:c]
                pw = both[c:]
            else:
                p = p + pw + _bdot(p, _block_diag(pw, bd))
        vb = vc * bc
        kbg = kb * eg
        uw = _bdot(p, jnp.concatenate([_block_diag(vb, bd), _block_diag(kbg, bd)], axis=1))
        u = vb + uw[:, :hw]
        w = kbg + uw[:, hw:]
        ws = _bdot(jnp.concatenate([w, qc * eg], axis=0), _block_diag(s, bd))
        r = u - ws[:c]
        outs.append(ws[c:] + _bdot(qk, _block_diag(r, bd)))
        g_last = gcum[c - 1:c, :]
        kd = kc * jnp.exp(g_last - gcum)
        s = jnp.exp(g_last) * s + _diag_blocks(_bdot_tn(kd, r), A_DK, A_DV)
    s_scr[...] = s

    o = jnp.concatenate(outs, axis=0)
    o = o * lax.rsqrt(_seg_sum(o * o, ones_bd) * (1.0 / A_DV) + EPS) * anorm_ref[...]
    o_ref[0] = o * _silu(z)

    @pl.when(t == pl.num_programs(1) - 1)
    def _():
        conv_ref[0] = x[tblk - (CONV_W - 1):tblk, :]
        s_out_ref[0] = s


def _gdn_prompt(u_a, cw, alog, dtb, anorm, *, tblk):
    b, t, _ = u_a.shape
    hw = A_HEADS * A_DV
    return pl.pallas_call(
        functools.partial(_gdn_prompt_kernel, tblk=tblk),
        grid=(b, t // tblk),
        in_specs=[pl.BlockSpec((1, tblk, UA_W), lambda i, j: (i, j, 0)),
                  _const_spec((CONV_W, A_CONV_CH)), _const_spec((1, LANE)), _const_spec((1, LANE)),
                  _const_spec((1, hw))],
        out_specs=[pl.BlockSpec((1, tblk, hw), lambda i, j: (i, j, 0)),
                   pl.BlockSpec((1, CONV_W - 1, A_CONV_CH), lambda i, j: (i, 0, 0)),
                   pl.BlockSpec((1, A_DK, hw), lambda i, j: (i, 0, 0))],
        out_shape=[jax.ShapeDtypeStruct((b, t, hw), F32),
                   jax.ShapeDtypeStruct((b, CONV_W - 1, A_CONV_CH), F32),
                   jax.ShapeDtypeStruct((b, A_DK, hw), F32)],
        scratch_shapes=[pltpu.VMEM((tblk + 8, A_CONV_CH), F32), pltpu.VMEM((A_DK, hw), F32)],
        compiler_params=_cparams(("parallel", "arbitrary")),
        name="gdn_prompt",
    )(u_a, cw, alog, dtb, anorm)


def _gla_prompt_kernel(u_ref, w2_ref, gb_ref, bnorm_ref, o_ref, s_out_ref, s_scr, *, tblk):
    t = pl.program_id(1)
    vw = B_HEADS * B_DV

    @pl.when(t == 0)
    def _():
        s_scr[...] = jnp.zeros_like(s_scr)

    u = u_ref[0]
    q = u[:, 0:B_QK] * (B_DK ** -0.5)
    k = u[:, B_QK:2 * B_QK]
    v = u[:, 2 * B_QK:2 * B_QK + B_V]
    rg = u[:, 2 * B_QK + B_V:2 * B_QK + 2 * B_V]
    glr = u[:, 2 * B_QK + 2 * B_V:]
    log_a = jax.nn.log_sigmoid(_bdot(glr, w2_ref[...]) + gb_ref[...]) / B_GATE_TAU

    c = CHUNK
    kmask = _block_mask(vw, B_QK, c, B_DK)
    vmask = _block_mask(vw, vw, c, B_DV)
    smask = _block_mask(vw, B_QK, B_DV, B_DK)
    row_k = _iota((c, B_QK), 0)
    row_s = _iota((SUB, vw), 0)
    col_s = _iota((SUB, vw), 1) % c
    st = s_scr[...]
    outs = []
    for ci in range(tblk // c):
        sl = slice(ci * c, (ci + 1) * c)
        qc, kc, vc = q[sl], k[sl], v[sl]
        b = _cumsum_rows(log_a[sl])
        o = _bdot_nt(qc * jnp.exp(b), st)
        att = []
        for i in range(c // SUB):
            r0 = i * SUB
            bref = b[r0:r0 + 1, :]
            qs = qc[r0:r0 + SUB] * jnp.exp(b[r0:r0 + SUB] - bref)
            ks = kc * jnp.exp(jnp.where(row_k < r0 + SUB, bref - b, -jnp.inf))
            a_i = _bdot_nt(qs, _block_diag(ks, kmask))
            att.append(jnp.where(row_s + r0 >= col_s, a_i, 0.0))
        o = o + _bdot(jnp.concatenate(att, axis=0), _block_diag(vc, vmask))
        outs.append(o)
        b_last = b[c - 1:c, :]
        kd = kc * jnp.exp(b_last - b)
        st = jnp.exp(b_last) * st + jnp.where(smask, _bdot_tn(vc, kd), 0.0)
    s_scr[...] = st

    o = jnp.concatenate(outs, axis=0)
    ones_bd = _block_mask(vw, vw, B_DV, B_DV).astype(BF16)
    o = o * lax.rsqrt(_seg_sum(o * o, ones_bd) * (1.0 / B_DV) + EPS) * bnorm_ref[...]
    o_ref[0] = o * _silu(rg)

    @pl.when(t == pl.num_programs(1) - 1)
    def _():
        s_out_ref[0] = st


def _gla_prompt(u_b, w2, gb, bnorm, *, tblk):
    b, t, _ = u_b.shape
    vw = B_HEADS * B_DV
    return pl.pallas_call(
        functools.partial(_gla_prompt_kernel, tblk=tblk),
        grid=(b, t // tblk),
        in_specs=[pl.BlockSpec((1, tblk, UB_W), lambda i, j: (i, j, 0)),
                  _const_spec((LANE, B_QK)), _const_spec((1, B_QK)), _const_spec((1, vw))],
        out_specs=[pl.BlockSpec((1, tblk, vw), lambda i, j: (i, j, 0)),
                   pl.BlockSpec((1, vw, B_QK), lambda i, j: (i, 0, 0))],
        out_shape=[jax.ShapeDtypeStruct((b, t, vw), F32),
                   jax.ShapeDtypeStruct((b, vw, B_QK), F32)],
        scratch_shapes=[pltpu.VMEM((vw, B_QK), F32)],
        compiler_params=_cparams(("parallel", "arbitrary")),
        name="gla_prompt",
    )(u_b, w2, gb, bnorm)


def _mla_prep_kernel(u_ref, cqn_ref, ckvn_ref, wuq_ref, wuk_ref, ck_ref, sk_ref, cq_ref, sq_ref,
                     ckv_o, kpe_o, ckvb_o, kpeb_o, qlat_o, qpe_o):
    u = u_ref[0]
    cq = _rms(u[:, 0:C_Q_LORA], cqn_ref[...]).astype(BF16)
    ckv = _rms(u[:, C_Q_LORA:C_Q_LORA + C_KV_LORA], ckvn_ref[...])
    half = C_ROPE // 2
    kx = u[:, C_Q_LORA + C_KV_LORA:]
    lane = _iota(kx.shape, 1)
    kswap = jnp.where(lane < half, pltpu.roll(kx, LANE - half, axis=1), pltpu.roll(kx, half, axis=1))
    kpe = (kx * ck_ref[...] + kswap * sk_ref[...])[:, 0:C_ROPE]
    ckv_o[0] = ckv
    kpe_o[0] = kpe
    ckvb_o[0] = ckv.astype(BF16)
    kpeb_o[0] = kpe.astype(BF16)

    qf = jnp.dot(cq, wuq_ref[...], preferred_element_type=F32)
    nope_w = C_HEADS * C_NOPE
    rope_w = C_HEADS * C_ROPE
    qr = qf[:, nope_w:]
    lane_r = _iota(qr.shape, 1) % C_ROPE
    qswap = jnp.where(lane_r < half, pltpu.roll(qr, rope_w - half, axis=1), pltpu.roll(qr, half, axis=1))
    qpe = qr * cq_ref[...] + qswap * sq_ref[...]
    for h in range(C_HEADS):
        qlat_o[0, h] = jnp.dot(qf[:, h * C_NOPE:(h + 1) * C_NOPE].astype(BF16), wuk_ref[h],
                               preferred_element_type=F32).astype(BF16)
        qpe_o[0, h] = qpe[:, h * C_ROPE:(h + 1) * C_ROPE].astype(BF16)


def _mla_prep(u_c, cqn, ckvn, wuq, wuk, tabs, *, tm):
    b, t, _ = u_c.shape
    ck, sk, cq, sq = tabs
    per_pos = ck.shape[0] != 1
    rope_w = C_HEADS * C_ROPE

    def tab_spec(w):
        if per_pos:
            return pl.BlockSpec((tm, w), lambda i, j: (j, 0))
        return _const_spec((1, w))

    def tok_spec(w):
        return pl.BlockSpec((1, tm, w), lambda i, j: (i, j, 0))

    def head_spec(w):
        return pl.BlockSpec((1, C_HEADS, tm, w), lambda i, j: (i, 0, j, 0))

    return pl.pallas_call(
        _mla_prep_kernel,
        grid=(b, t // tm),
        in_specs=[tok_spec(UC_W), _const_spec((1, C_Q_LORA)), _const_spec((1, C_KV_LORA)),
                  _const_spec(wuq.shape), _const_spec(wuk.shape),
                  tab_spec(LANE), tab_spec(LANE), tab_spec(rope_w), tab_spec(rope_w)],
        out_specs=[tok_spec(C_KV_LORA), tok_spec(C_ROPE), tok_spec(C_KV_LORA), tok_spec(C_ROPE),
                   head_spec(C_KV_LORA), head_spec(C_ROPE)],
        out_shape=[jax.ShapeDtypeStruct((b, t, C_KV_LORA), F32),
                   jax.ShapeDtypeStruct((b, t, C_ROPE), F32),
                   jax.ShapeDtypeStruct((b, t, C_KV_LORA), BF16),
                   jax.ShapeDtypeStruct((b, t, C_ROPE), BF16),
                   jax.ShapeDtypeStruct((b, C_HEADS, t, C_KV_LORA), BF16),
                   jax.ShapeDtypeStruct((b, C_HEADS, t, C_ROPE), BF16)],
        compiler_params=_cparams(("parallel", "parallel")),
        name="mla_prep",
    )(u_c, cqn, ckvn, wuq, wuk, ck, sk, cq, sq)


def _mla_flash_kernel(qi_ref, kj_ref, ql_ref, qp_ref, k_ref, p_ref, o_ref,
                      m_scr, l_scr, acc_scr, *, tq, tk):
    g = pl.program_id(1)
    i = qi_ref[g]
    j = kj_ref[g]
    last_j = (i * tq + tq - 1) // tk
    c2 = MLA_SCALE * LOG2E

    @pl.when(j == 0)
    def _():
        m_scr[...] = jnp.full_like(m_scr, -jnp.inf)
        l_scr[...] = jnp.zeros_like(l_scr)
        acc_scr[...] = jnp.zeros_like(acc_scr)

    def step(masked):
        kv = k_ref[0]
        pe = p_ref[0]
        if masked:
            visible = j * tk + _iota((tq, tk), 1) <= i * tq + _iota((tq, tk), 0)
        for h in range(C_HEADS):
            s = (lax.dot_general(ql_ref[0, h], kv, (((1,), (1,)), ((), ())), preferred_element_type=F32)
                 + lax.dot_general(qp_ref[0, h], pe, (((1,), (1,)), ((), ())), preferred_element_type=F32))
            if masked:
                s = jnp.where(visible, s, -jnp.inf)
            m_old = m_scr[h]
            m_new = jnp.maximum(m_old, jnp.max(s, axis=-1, keepdims=True))
            alpha = jnp.exp2((m_old - m_new) * c2)
            p = jnp.exp2(s * c2 - m_new * c2)
            l_scr[h] = alpha * l_scr[h] + jnp.sum(p, axis=-1, keepdims=True)
            m_scr[h] = m_new
            acc_scr[h] = alpha * acc_scr[h] + jnp.dot(p.astype(BF16), kv, preferred_element_type=F32)

    has_masked = j * tk + tk - 1 > i * tq

    @pl.when(has_masked)
    def _():
        step(True)

    @pl.when(jnp.logical_not(has_masked))
    def _():
        step(False)

    @pl.when(j == last_j)
    def _():
        o_ref[0] = (acc_scr[...] / l_scr[...]).astype(BF16)


def _mla_flash(qlat, qpe, ckvb, kpeb, *, tq, tk):
    b, _, t, _ = qlat.shape
    pairs = [(i, j) for i in range(t // tq) for j in range((i * tq + tq - 1) // tk + 1)]
    qi = jnp.asarray([p[0] for p in pairs], jnp.int32)
    kj = jnp.asarray([p[1] for p in pairs], jnp.int32)
    grid_spec = pltpu.PrefetchScalarGridSpec(
        num_scalar_prefetch=2,
        grid=(b, len(pairs)),
        in_specs=[pl.BlockSpec((1, C_HEADS, tq, C_KV_LORA), lambda bi, g, qi, kj: (bi, 0, qi[g], 0)),
                  pl.BlockSpec((1, C_HEADS, tq, C_ROPE), lambda bi, g, qi, kj: (bi, 0, qi[g], 0)),
                  pl.BlockSpec((1, tk, C_KV_LORA), lambda bi, g, qi, kj: (bi, kj[g], 0)),
                  pl.BlockSpec((1, tk, C_ROPE), lambda bi, g, qi, kj: (bi, kj[g], 0))],
        out_specs=pl.BlockSpec((1, C_HEADS, tq, C_KV_LORA), lambda bi, g, qi, kj: (bi, 0, qi[g], 0)),
        scratch_shapes=[pltpu.VMEM((C_HEADS, tq, 1), F32), pltpu.VMEM((C_HEADS, tq, 1), F32),
                        pltpu.VMEM((C_HEADS, tq, C_KV_LORA), F32)],
    )
    return pl.pallas_call(
        functools.partial(_mla_flash_kernel, tq=tq, tk=tk),
        grid_spec=grid_spec,
        out_shape=jax.ShapeDtypeStruct((b, C_HEADS, t, C_KV_LORA), BF16),
        compiler_params=_cparams(("parallel", "arbitrary")),
        name="mla_flash",
    )(qi, kj, qlat, qpe, ckvb, kpeb)


def _mla_oproj_kernel(ol_ref, wuv_ref, o_ref):
    o_ref[0] = jnp.concatenate(
        [jnp.dot(ol_ref[0, h].astype(BF16), wuv_ref[h], preferred_element_type=F32)
         for h in range(C_HEADS)], axis=-1)


def _mla_oproj(olat, wuv, *, tm):
    b, _, t, _ = olat.shape
    return pl.pallas_call(
        _mla_oproj_kernel,
        grid=(b, t // tm),
        in_specs=[pl.BlockSpec((1, C_HEADS, tm, C_KV_LORA), lambda i, j: (i, 0, j, 0)),
                  _const_spec(wuv.shape)],
        out_specs=pl.BlockSpec((1, tm, C_HEADS * C_VDIM), lambda i, j: (i, j, 0)),
        out_shape=jax.ShapeDtypeStruct((b, t, C_HEADS * C_VDIM), F32),
        compiler_params=_cparams(("parallel", "parallel")),
        name="mla_oproj",
    )(olat, wuv)


def _mla_paged_kernel(pt_ref, ql_ref, qp_ref, cn_ref, pn_ref, ckv_hbm, kpe_hbm, o_ref,
                      kbuf, pbuf, sem, m_scr, l_scr, acc_scr, *, layer, ppc, nch, page):
    b = pl.program_id(0)
    c = pl.program_id(1)
    g = b * nch + c
    total = pl.num_programs(0) * nch
    slot = g % 2

    def copies(bb, cc, sl, p):
        pid = pt_ref[bb, cc * ppc + p]
        dst = pl.ds(pl.multiple_of(p * page, page), page)
        return (pltpu.make_async_copy(ckv_hbm.at[layer, pid], kbuf.at[sl, dst, :], sem.at[0, sl]),
                pltpu.make_async_copy(kpe_hbm.at[layer, pid], pbuf.at[sl, :, dst], sem.at[1, sl]))

    def issue(bb, cc, sl):
        def body(p, carry):
            for cp in copies(bb, cc, sl, p):
                cp.start()
            return carry
        lax.fori_loop(0, ppc, body, 0)

    @pl.when(g == 0)
    def _():
        issue(0, 0, 0)

    @pl.when(g + 1 < total)
    def _():
        issue((g + 1) // nch, (g + 1) % nch, 1 - slot)

    def wait_body(p, carry):
        for cp in copies(b, c, slot, p):
            cp.wait()
        return carry
    lax.fori_loop(0, ppc, wait_body, 0)

    @pl.when(c == 0)
    def _():
        m_scr[...] = jnp.full_like(m_scr, -jnp.inf)
        l_scr[...] = jnp.zeros_like(l_scr)
        acc_scr[...] = jnp.zeros_like(acc_scr)

    ql = ql_ref[0]
    qp = qp_ref[0]
    kv = kbuf[slot].astype(BF16)
    pe = pbuf[slot].astype(BF16)
    s = (lax.dot_general(ql, kv, (((1,), (1,)), ((), ())), preferred_element_type=F32)
         + jnp.dot(qp, pe, preferred_element_type=F32)) * MLA_SCALE
    m_old = m_scr[...]
    m_new = jnp.maximum(m_old, jnp.max(s, axis=-1, keepdims=True))
    alpha = jnp.exp(m_old - m_new)
    p = jnp.exp(s - m_new)
    l_new = alpha * l_scr[...] + jnp.sum(p, axis=-1, keepdims=True)
    acc_new = alpha * acc_scr[...] + jnp.dot(p.astype(BF16), kv, preferred_element_type=F32)
    m_scr[...] = m_new
    l_scr[...] = l_new
    acc_scr[...] = acc_new

    @pl.when(c == nch - 1)
    def _():
        cn = cn_ref[0].astype(BF16).astype(F32)
        pn = pn_ref[0].astype(BF16).astype(F32)
        s_n = (jnp.sum(ql.astype(F32) * cn, axis=-1, keepdims=True)
               + jnp.sum(qp.astype(F32) * pn, axis=-1, keepdims=True)) * MLA_SCALE
        m_f = jnp.maximum(m_new, s_n)
        a_f = jnp.exp(m_new - m_f)
        p_n = jnp.exp(s_n - m_f)
        l_f = a_f * l_new + p_n
        acc_f = a_f * acc_new + p_n.astype(BF16).astype(F32) * cn
        o_ref[0] = acc_f / l_f


def _mla_paged(page_table, qlat, qpe, ckv_new, kpe_new, cache_ckv, cache_kpe, *, layer, ppc):
    b, hp, _ = qlat.shape
    n_pages = page_table.shape[1]
    page = cache_ckv.shape[2]
    nch = n_pages // ppc
    grid_spec = pltpu.PrefetchScalarGridSpec(
        num_scalar_prefetch=1,
        grid=(b, nch),
        in_specs=[pl.BlockSpec((1, hp, C_KV_LORA), lambda i, j, pt: (i, 0, 0)),
                  pl.BlockSpec((1, hp, C_ROPE), lambda i, j, pt: (i, 0, 0)),
                  pl.BlockSpec((1, 1, C_KV_LORA), lambda i, j, pt: (i, 0, 0)),
                  pl.BlockSpec((1, 1, C_ROPE), lambda i, j, pt: (i, 0, 0)),
                  pl.BlockSpec(memory_space=pl.ANY),
                  pl.BlockSpec(memory_space=pl.ANY)],
        out_specs=pl.BlockSpec((1, hp, C_KV_LORA), lambda i, j, pt: (i, 0, 0)),
        scratch_shapes=[pltpu.VMEM((2, ppc * page, C_KV_LORA), F32),
                        pltpu.VMEM((2, C_ROPE, ppc * page), F32),
                        pltpu.SemaphoreType.DMA((2, 2)),
                        pltpu.VMEM((hp, 1), F32), pltpu.VMEM((hp, 1), F32),
                        pltpu.VMEM((hp, C_KV_LORA), F32)],
    )
    return pl.pallas_call(
        functools.partial(_mla_paged_kernel, layer=layer, ppc=ppc, nch=nch, page=page),
        grid_spec=grid_spec,
        out_shape=jax.ShapeDtypeStruct((b, hp, C_KV_LORA), F32),
        compiler_params=_cparams(("arbitrary", "arbitrary")),
        name="mla_paged",
    )(page_table, qlat, qpe, ckv_new, kpe_new, cache_ckv, cache_kpe)


def _xattn_prompt_kernel(x_ref, nw_ref, wq_ref, wo_ref, mk_ref, mv_ref, o_ref):
    x = x_ref[0]
    xn = _rms(x, nw_ref[...]).astype(BF16)
    q = jnp.dot(xn, wq_ref[...], preferred_element_type=F32).astype(BF16)
    heads = []
    for h in range(X_HEADS):
        sl = slice(h * X_HDIM, (h + 1) * X_HDIM)
        s = lax.dot_general(q[:, sl], mk_ref[0, :, sl], (((1,), (1,)), ((), ())),
                            preferred_element_type=F32) * (X_HDIM ** -0.5)
        e = jnp.exp(s - jnp.max(s, axis=-1, keepdims=True))
        p = (e / jnp.sum(e, axis=-1, keepdims=True)).astype(BF16)
        heads.append(jnp.dot(p, mv_ref[0, :, sl], preferred_element_type=F32).astype(BF16))
    o_ref[0] = x + jnp.dot(jnp.concatenate(heads, axis=-1), wo_ref[...], preferred_element_type=F32)


def _xattn_prompt(x, nw, wq, wo, mk, mv, *, tm):
    b, t, d = x.shape
    n_mem = mk.shape[1]
    return pl.pallas_call(
        _xattn_prompt_kernel,
        grid=(b, t // tm),
        in_specs=[pl.BlockSpec((1, tm, d), lambda i, j: (i, j, 0)), _const_spec((1, d)),
                  _const_spec(wq.shape), _const_spec(wo.shape),
                  pl.BlockSpec((1, n_mem, d), lambda i, j: (i, 0, 0)),
                  pl.BlockSpec((1, n_mem, d), lambda i, j: (i, 0, 0))],
        out_specs=pl.BlockSpec((1, tm, d), lambda i, j: (i, j, 0)),
        out_shape=jax.ShapeDtypeStruct((b, t, d), F32),
        compiler_params=_cparams(("parallel", "parallel")),
        name="xattn_prompt",
    )(x, nw, wq, wo, mk, mv)


def _xattn_decode_kernel(q_ref, mk_ref, mv_ref, o_ref):
    nt = X_HDIM // LANE
    grp = nt * X_HEADS
    n_rows = mk_ref.shape[2]
    qrow = q_ref[0]
    qm = jnp.concatenate([qrow[:, h * X_HDIM + t * LANE:h * X_HDIM + (t + 1) * LANE]
                          for t in range(nt) for h in range(X_HEADS)], axis=0)
    sel = _iota((grp, n_rows), 0) == _iota((grp, n_rows), 1) % grp
    s_all = _bdot_nt(qm, mk_ref[0, 0])
    part = jnp.sum(jnp.where(sel, s_all, 0.0), axis=0, keepdims=True)
    lane = _iota((1, n_rows), 1)
    s = part
    for t in range(1, nt):
        s = s + jnp.where(lane % grp < X_HEADS, pltpu.roll(part, n_rows - t * X_HEADS, axis=1),
                          pltpu.roll(part, t * X_HEADS, axis=1))
    s = s * (X_HDIM ** -0.5)
    p = jnp.zeros_like(s)
    for h in range(X_HEADS):
        mine = lane % X_HEADS == h
        e = jnp.exp(s - jnp.max(jnp.where(mine, s, -jnp.inf), axis=-1, keepdims=True))
        den = jnp.sum(jnp.where(lane % grp == h, e, 0.0), axis=-1, keepdims=True)
        p = jnp.where(mine, e / den, p)
    o = _bdot(jnp.where(sel, p, 0.0), mv_ref[0, 0])
    o_ref[0] = jnp.concatenate([o[t * X_HEADS + h:t * X_HEADS + h + 1]
                                for h in range(X_HEADS) for t in range(nt)], axis=-1)


def _mem_rows(mem):
    dep, b, n_mem, nh, hd = mem.shape
    nt = hd // LANE
    return mem.reshape(dep, b, n_mem, nh, nt, LANE).transpose(0, 1, 2, 4, 3, 5).reshape(dep, b, n_mem * nt * nh, LANE)


def _xattn_decode(q, mk_rows, mv_rows, *, layer):
    _, b, n_rows, _ = mk_rows.shape
    d = X_HEADS * X_HDIM
    assert X_HDIM == 2 * LANE
    mem_spec = pl.BlockSpec((1, 1, n_rows, LANE), lambda i: (layer, i, 0, 0))
    return pl.pallas_call(
        _xattn_decode_kernel,
        grid=(b,),
        in_specs=[pl.BlockSpec((1, 1, d), lambda i: (i, 0, 0)), mem_spec, mem_spec],
        out_specs=pl.BlockSpec((1, 1, d), lambda i: (i, 0, 0)),
        out_shape=jax.ShapeDtypeStruct((b, 1, d), F32),
        compiler_params=_cparams(("parallel",)),
        name="xattn_decode",
    )(q, mk_rows, mv_rows)


def _columns(rows):
    w = rows[0].shape[1]
    mat = jnp.concatenate(rows + [jnp.zeros((LANE - len(rows), w), F32)], axis=0)
    return mat.T


def _gdn_decode_kernel(u_ref, cs_ref, s_ref, cw_ref, alog_ref, dtb_ref, anorm_ref,
                       o_ref, cs_out_ref, s_out_ref):
    x = u_ref[0, :, 0:A_CONV_CH]
    z = u_ref[0, :, A_CONV_CH:A_CONV_CH + A_V]
    ba = u_ref[0, :, A_CONV_CH + A_V:UA_W]
    cs = cs_ref[0, 0]
    cw = cw_ref[...]
    y = cw[0:1] * cs[0:1] + cw[1:2] * cs[1:2] + cw[2:3] * cs[2:3] + cw[3:4] * x
    cs_out_ref[0] = jnp.concatenate([cs[1:CONV_W - 1], x], axis=0)
    qkv = _silu(y)
    beta = jax.nn.sigmoid(ba)
    g = -jnp.exp(alog_ref[...]) * jax.nn.softplus(ba + dtb_ref[...])
    qs, ks, vs = [], [], []
    for h in range(A_HEADS):
        qh = qkv[:, h * A_DK:(h + 1) * A_DK]
        kh = qkv[:, A_QK + h * A_DK:A_QK + (h + 1) * A_DK]
        qs.append(qh * lax.rsqrt(jnp.sum(qh * qh, axis=-1, keepdims=True) + EPS) * (A_DK ** -0.5))
        ks.append(kh * lax.rsqrt(jnp.sum(kh * kh, axis=-1, keepdims=True) + EPS))
        vs.append(qkv[:, 2 * A_QK + h * A_DV:2 * A_QK + (h + 1) * A_DV])
    cols = _columns(ks + qs)
    outs = []
    for h in range(A_HEADS):
        s = s_ref[0, 0, h]
        kcol = cols[:, h:h + 1]
        qcol = cols[:, A_HEADS + h:A_HEADS + h + 1]
        bh = beta[:, h:h + 1]
        eg = jnp.exp(g[:, A_HEADS + h:A_HEADS + h + 1])
        ks_row = jnp.sum(kcol * s, axis=0, keepdims=True)
        qs_row = jnp.sum(qcol * s, axis=0, keepdims=True)
        r = vs[h] * bh - (bh * eg) * ks_row
        qk = jnp.sum(qs[h] * ks[h], axis=-1, keepdims=True)
        o = eg * qs_row + qk * r
        s_out_ref[0, h] = eg * s + kcol * r
        o = o * lax.rsqrt(jnp.mean(o * o, axis=-1, keepdims=True) + EPS) * anorm_ref[...]
        outs.append(o)
    o_ref[0] = jnp.concatenate(outs, axis=-1) * _silu(z)


def _gdn_decode(u_a, cs, s, cw, alog, dtb, anorm, *, layer):
    b = u_a.shape[0]
    return pl.pallas_call(
        _gdn_decode_kernel,
        grid=(b,),
        in_specs=[pl.BlockSpec((1, 1, UA_W), lambda i: (i, 0, 0)),
                  pl.BlockSpec((1, 1, CONV_W - 1, A_CONV_CH), lambda i: (layer, i, 0, 0)),
                  pl.BlockSpec((1, 1, A_HEADS, A_DK, A_DV), lambda i: (layer, i, 0, 0, 0)),
                  _const_spec((CONV_W, A_CONV_CH)), _const_spec((1, LANE)), _const_spec((1, LANE)),
                  _const_spec((1, A_DV))],
        out_specs=[pl.BlockSpec((1, 1, A_V), lambda i: (i, 0, 0)),
                   pl.BlockSpec((1, CONV_W - 1, A_CONV_CH), lambda i: (i, 0, 0)),
                   pl.BlockSpec((1, A_HEADS, A_DK, A_DV), lambda i: (i, 0, 0, 0))],
        out_shape=[jax.ShapeDtypeStruct((b, 1, A_V), F32),
                   jax.ShapeDtypeStruct((b, CONV_W - 1, A_CONV_CH), F32),
                   jax.ShapeDtypeStruct((b, A_HEADS, A_DK, A_DV), F32)],
        compiler_params=_cparams(("parallel",)),
        name="gdn_decode",
    )(u_a, cs, s, cw, alog, dtb, anorm)


def _gla_decode_kernel(u_ref, s_ref, w2_ref, gb_ref, bnorm_ref, o_ref, s_out_ref):
    u = u_ref[0]
    q = u[:, 0:B_QK] * (B_DK ** -0.5)
    k = u[:, B_QK:2 * B_QK]
    rg = u[:, 2 * B_QK + B_V:2 * B_QK + 2 * B_V]
    glr = jnp.broadcast_to(u[:, 2 * B_QK + 2 * B_V:], (8, LANE))
    log_a = jax.nn.log_sigmoid(_bdot(glr, w2_ref[...])[0:1] + gb_ref[...]) / B_GATE_TAU
    a = jnp.exp(log_a)
    cols = _columns([k, q * a, a])
    outs = []
    for h in range(B_HEADS):
        rs = slice(h * B_DK, (h + 1) * B_DK)
        s = s_ref[0, 0, h]
        vh = u[:, 2 * B_QK + h * B_DV:2 * B_QK + (h + 1) * B_DV]
        qk = jnp.sum(q[:, rs] * k[:, rs], axis=-1, keepdims=True)
        o = jnp.sum(cols[rs, 1:2] * s, axis=0, keepdims=True) + qk * vh
        s_out_ref[0, h] = cols[rs, 2:3] * s + cols[rs, 0:1] * vh
        o = o * lax.rsqrt(jnp.mean(o * o, axis=-1, keepdims=True) + EPS) * bnorm_ref[...]
        outs.append(o)
    o_ref[0] = jnp.concatenate(outs, axis=-1) * _silu(rg)


def _gla_decode(u_b, s, w2, gb, bnorm, *, layer):
    b = u_b.shape[0]
    return pl.pallas_call(
        _gla_decode_kernel,
        grid=(b,),
        in_specs=[pl.BlockSpec((1, 1, UB_W), lambda i: (i, 0, 0)),
                  pl.BlockSpec((1, 1, B_HEADS, B_DK, B_DV), lambda i: (layer, i, 0, 0, 0)),
                  _const_spec((LANE, B_QK)), _const_spec((1, B_QK)), _const_spec((1, B_DV))],
        out_specs=[pl.BlockSpec((1, 1, B_V), lambda i: (i, 0, 0)),
                   pl.BlockSpec((1, B_HEADS, B_DK, B_DV), lambda i: (i, 0, 0, 0))],
        out_shape=[jax.ShapeDtypeStruct((b, 1, B_V), F32),
                   jax.ShapeDtypeStruct((b, B_HEADS, B_DK, B_DV), F32)],
        compiler_params=_cparams(("parallel",)),
        name="gla_decode",
    )(u_b, s, w2, gb, bnorm)


def _pad_cols(w, n):
    return jnp.pad(w, ((0, 0), (0, n - w.shape[1])))


def _rope_tables(pos):
    half = C_ROPE // 2
    inv = ROPE_THETA ** (-jnp.arange(half, dtype=F32) / half)
    ang = pos.astype(F32)[:, None] * inv[None, :]
    cos, sin = jnp.cos(ang), jnp.sin(ang)
    zero = jnp.zeros_like(cos)
    ck = jnp.concatenate([cos, cos, zero, zero], axis=-1)
    sk = jnp.concatenate([-sin, sin, zero, zero], axis=-1)
    cq = jnp.tile(jnp.concatenate([cos, cos], axis=-1), (1, C_HEADS))
    sq = jnp.tile(jnp.concatenate([-sin, sin], axis=-1), (1, C_HEADS))
    return ck, sk, cq, sq


def _layer_weights(l, P):
    w_in = P['w_in'][l]
    o_b, o_c = A_IN, A_IN + B_IN
    w_a = jnp.concatenate([w_in[:, :A_CONV_CH + A_V], _pad_cols(w_in[:, A_CONV_CH + A_V:A_IN], LANE)], axis=1)
    w_b = jnp.concatenate([w_in[:, o_b:o_b + 2 * B_QK + 2 * B_V],
                           _pad_cols(w_in[:, o_b + 2 * B_QK + 2 * B_V:o_c], LANE)], axis=1)
    w_c = jnp.concatenate([w_in[:, o_c:o_c + C_Q_LORA + C_KV_LORA],
                           _pad_cols(w_in[:, o_c + C_Q_LORA + C_KV_LORA:], LANE)], axis=1)
    wuq = P['c_w_uq'][l].reshape(C_Q_LORA, C_HEADS, C_NOPE + C_ROPE)
    wuq = jnp.concatenate([wuq[:, :, :C_NOPE].reshape(C_Q_LORA, -1),
                           wuq[:, :, C_NOPE:].reshape(C_Q_LORA, -1)], axis=1)
    w_out = P['w_out'][l]
    row = lambda a: a.reshape(1, -1).astype(F32)
    head_lanes = lambda a: jnp.pad(a.reshape(1, -1).astype(F32), ((0, 0), (A_HEADS, LANE - 2 * A_HEADS)))
    return dict(
        norm_ffn1=row(P['norm_ffn1'][l]), norm_ffn2=row(P['norm_ffn2'][l]),
        ffn1=tuple(P[n][l].astype(BF16) for n in ('w_ffn1_gate', 'w_ffn1_up', 'w_ffn1_down')),
        ffn2=tuple(P[n][l].astype(BF16) for n in ('w_ffn2_gate', 'w_ffn2_up', 'w_ffn2_down')),
        norm_mix=row(P['norm_mix'][l]),
        w_in=(w_a.astype(BF16), w_b.astype(BF16), w_c.astype(BF16)),
        a_conv_w=P['a_conv_w'][l].astype(F32),
        a_log=head_lanes(P['a_log'][l]), a_dt_bias=head_lanes(P['a_dt_bias'][l]),
        a_norm=row(P['a_norm'][l]), a_norm_t=row(jnp.tile(P['a_norm'][l], A_HEADS)),
        b_w2=jnp.pad(P['b_gate_w2'][l], ((0, LANE - B_GATE_RANK), (0, 0))).astype(BF16),
        b_gate_bias=row(P['b_gate_bias'][l]),
        b_norm=row(P['b_norm'][l]), b_norm_t=row(jnp.tile(P['b_norm'][l], B_HEADS)),
        c_q_norm=row(P['c_q_norm'][l]), c_kv_norm=row(P['c_kv_norm'][l]),
        c_w_uq=wuq.astype(BF16),
        c_w_uk=jnp.transpose(P['c_w_uk'][l], (1, 2, 0)).astype(BF16),
        c_w_uv=jnp.transpose(P['c_w_uv'][l], (1, 0, 2)).astype(BF16),
        w_out=(w_out[:A_V].astype(BF16), w_out[A_V:A_V + B_V].astype(BF16), w_out[A_V + B_V:].astype(BF16)),
        norm_x=row(P['norm_x'][l]), norm_mem=row(P['norm_mem'][l]),
        w_xq=P['w_xq'][l].astype(BF16), w_xo=P['w_xo'][l].astype(BF16),
        w_xk=P['w_xk'][l].astype(BF16), w_xv=P['w_xv'][l].astype(BF16),
    )


def _pick(n, pref):
    return pref if n % pref == 0 else n


def kernel(x_prompt, x_sample, mem_prompt, cache_ckv, cache_kpe, page_table, state_conv_a, state_delta, state_gla, cache_mem_k, cache_mem_v, norm_ffn1, w_ffn1_gate, w_ffn1_up, w_ffn1_down, norm_mix, w_in, a_conv_w, a_log, a_dt_bias, a_norm, b_gate_w2, b_gate_bias, b_norm, c_q_norm, c_w_uq, c_kv_norm, c_w_uk, c_w_uv, w_out, norm_x, norm_mem, w_xq, w_xk, w_xv, w_xo, norm_ffn2, w_ffn2_gate, w_ffn2_up, w_ffn2_down, final_norm):
    P = dict(norm_ffn1=norm_ffn1, w_ffn1_gate=w_ffn1_gate, w_ffn1_up=w_ffn1_up, w_ffn1_down=w_ffn1_down,
             norm_mix=norm_mix, w_in=w_in, a_conv_w=a_conv_w, a_log=a_log, a_dt_bias=a_dt_bias,
             a_norm=a_norm, b_gate_w2=b_gate_w2, b_gate_bias=b_gate_bias, b_norm=b_norm,
             c_q_norm=c_q_norm, c_w_uq=c_w_uq, c_kv_norm=c_kv_norm, c_w_uk=c_w_uk, c_w_uv=c_w_uv,
             w_out=w_out, norm_x=norm_x, norm_mem=norm_mem, w_xq=w_xq, w_xk=w_xk, w_xv=w_xv, w_xo=w_xo,
             norm_ffn2=norm_ffn2, w_ffn2_gate=w_ffn2_gate, w_ffn2_up=w_ffn2_up, w_ffn2_down=w_ffn2_down)
    depth = w_in.shape[0]
    W = [_layer_weights(l, P) for l in range(depth)]
    fnorm = final_norm.reshape(1, -1).astype(F32)

    bp, tp, d = x_prompt.shape
    mp = bp * tp
    n_mem = mem_prompt.shape[1]
    tm = _pick(tp, 512)
    tabs_p = _rope_tables(jnp.arange(tp, dtype=jnp.int32))
    x = x_prompt.reshape(mp, d)
    mem = mem_prompt.reshape(bp * n_mem, d)
    p_ckv, p_kpe, p_conv, p_delta, p_gla, p_mk, p_mv = [], [], [], [], [], [], []
    for l in range(depth):
        w = W[l]
        mk, mv = _norm_mm(mem, w['norm_mem'], [w['w_xk'], w['w_xv']], tm=_pick(bp * n_mem, 512))
        x = _ffn(x, w['norm_ffn1'], *w['ffn1'], fnorm, final=False, tm=tm)
        u_a, u_b, u_c = _norm_mm(x, w['norm_mix'], list(w['w_in']), tm=tm)
        o_a, conv_new, sd = _gdn_prompt(u_a.reshape(bp, tp, UA_W), w['a_conv_w'], w['a_log'], w['a_dt_bias'],
                                        w['a_norm_t'], tblk=_pick(tp, 256))
        o_b, sg = _gla_prompt(u_b.reshape(bp, tp, UB_W), w['b_w2'], w['b_gate_bias'], w['b_norm_t'],
                              tblk=_pick(tp, 256))
        ckv, kpe, ckvb, kpeb, qlat, qpe = _mla_prep(u_c.reshape(bp, tp, UC_W), w['c_q_norm'], w['c_kv_norm'],
                                                    w['c_w_uq'], w['c_w_uk'], tabs_p, tm=tm)
        olat = _mla_flash(qlat, qpe, ckvb, kpeb, tq=_pick(tp, 512), tk=_pick(tp, 512))
        o_c = _mla_oproj(olat, w['c_w_uv'], tm=tm)
        x = _mm_res(x, [o_a.reshape(mp, A_V), o_b.reshape(mp, B_V), o_c.reshape(mp, C_HEADS * C_VDIM)],
                    list(w['w_out']), tm=tm)
        x = _xattn_prompt(x.reshape(bp, tp, d), w['norm_x'], w['w_xq'], w['w_xo'],
                          mk.reshape(bp, n_mem, d).astype(BF16), mv.reshape(bp, n_mem, d).astype(BF16),
                          tm=tm).reshape(mp, d)
        x = _ffn(x, w['norm_ffn2'], *w['ffn2'], fnorm, final=(l == depth - 1), tm=tm)
        p_ckv.append(ckv)
        p_kpe.append(kpe)
        p_conv.append(conv_new)
        p_delta.append(sd.reshape(bp, A_DK, A_HEADS, A_DV).transpose(0, 2, 1, 3))
        sg = sg.reshape(bp, B_HEADS, B_DV, B_HEADS, B_DK)
        p_gla.append(jnp.stack([sg[:, h, :, h, :] for h in range(B_HEADS)], axis=1).transpose(0, 1, 3, 2))
        p_mk.append(mk.reshape(bp, n_mem, X_HEADS, X_HDIM))
        p_mv.append(mv.reshape(bp, n_mem, X_HEADS, X_HDIM))
    y_prompt = x.reshape(bp, tp, d)

    bs, ts, _ = x_sample.shape
    n_pages, page = page_table.shape[1], cache_ckv.shape[2]
    past_len = n_pages * page
    tabs_s = _rope_tables(past_len + jnp.arange(ts, dtype=jnp.int32))
    cache_kpe_t = jnp.swapaxes(cache_kpe, 2, 3)
    mem_k_rows, mem_v_rows = _mem_rows(cache_mem_k), _mem_rows(cache_mem_v)
    x = x_sample.reshape(bs, d)
    s_ckv, s_kpe, s_conv, s_delta, s_gla = [], [], [], [], []
    for l in range(depth):
        w = W[l]
        x = _ffn(x, w['norm_ffn1'], *w['ffn1'], fnorm, final=False, tm=bs)
        u_a, u_b, u_c = _norm_mm(x, w['norm_mix'], list(w['w_in']), tm=bs)
        o_a, conv_new, sd = _gdn_decode(u_a.reshape(bs, 1, UA_W), state_conv_a, state_delta,
                                        w['a_conv_w'], w['a_log'], w['a_dt_bias'], w['a_norm'], layer=l)
        o_b, sg = _gla_decode(u_b.reshape(bs, 1, UB_W), state_gla, w['b_w2'], w['b_gate_bias'], w['b_norm'],
                              layer=l)
        ckv, kpe, _, _, qlat, qpe = _mla_prep(u_c.reshape(1, bs, UC_W), w['c_q_norm'], w['c_kv_norm'],
                                              w['c_w_uq'], w['c_w_uk'], tabs_s, tm=bs)
        pad_heads = lambda a: jnp.pad(a[0].transpose(1, 0, 2), ((0, 0), (0, 8 - C_HEADS), (0, 0)))
        olat = _mla_paged(page_table, pad_heads(qlat), pad_heads(qpe), ckv.reshape(bs, 1, C_KV_LORA),
                          kpe.reshape(bs, 1, C_ROPE), cache_ckv, cache_kpe_t, layer=l, ppc=_pick(n_pages, 32))
        o_c = _mla_oproj(olat[:, :C_HEADS].transpose(1, 0, 2)[None], w['c_w_uv'], tm=bs)
        x = _mm_res(x, [o_a.reshape(bs, A_V), o_b.reshape(bs, B_V), o_c.reshape(bs, C_HEADS * C_VDIM)],
                    list(w['w_out']), tm=bs)
        (q,) = _norm_mm(x, w['norm_x'], [w['w_xq']], tm=bs)
        att = _xattn_decode(q.reshape(bs, 1, d), mem_k_rows, mem_v_rows, layer=l)
        x = _mm_res(x, [att.reshape(bs, d)], [w['w_xo']], tm=bs)
        x = _ffn(x, w['norm_ffn2'], *w['ffn2'], fnorm, final=(l == depth - 1), tm=bs)
        s_ckv.append(ckv.reshape(bs, ts, C_KV_LORA))
        s_kpe.append(kpe.reshape(bs, ts, C_ROPE))
        s_conv.append(conv_new)
        s_delta.append(sd)
        s_gla.append(sg)
    y_sample = x.reshape(bs, ts, d)

    return (y_prompt, y_sample,
            jnp.stack(p_ckv), jnp.stack(p_kpe), jnp.stack(p_conv), jnp.stack(p_delta), jnp.stack(p_gla),
            jnp.stack(p_mk), jnp.stack(p_mv),
            jnp.stack(s_ckv), jnp.stack(s_kpe), jnp.stack(s_conv), jnp.stack(s_delta), jnp.stack(s_gla))
```

```python
import functools

import jax
import jax.numpy as jnp
import numpy as np
from jax import lax
from jax.experimental import pallas as pl
from jax.experimental.pallas import tpu as pltpu

F32 = jnp.float32
BF16 = jnp.bfloat16

D_MODEL = 1024
A_HEADS, A_DK, A_DV, CONV_W = 4, 64, 64, 4
B_HEADS, B_DK, B_DV, B_GATE_RANK, B_GATE_TAU = 4, 32, 64, 16, 16.0
C_HEADS, C_NOPE, C_ROPE, C_VDIM, C_Q_LORA, C_KV_LORA = 4, 128, 64, 128, 384, 256
ROPE_THETA = 10000.0
X_HEADS, X_HDIM = 4, 256
D_FF = 2816
CHUNK = 64
EPS = 1e-6

A_QK = A_HEADS * A_DK
A_V = A_HEADS * A_DV
A_CONV_CH = 2 * A_QK + A_V
A_IN = A_CONV_CH + A_V + 2 * A_HEADS
B_QK = B_HEADS * B_DK
B_V = B_HEADS * B_DV
B_IN = 2 * B_QK + 2 * B_V + B_GATE_RANK
C_IN = C_Q_LORA + C_KV_LORA + C_ROPE
MLA_SCALE = (C_NOPE + C_ROPE) ** -0.5

LANE = 128
UA_W = A_CONV_CH + A_V + LANE
UB_W = 2 * B_QK + 2 * B_V + LANE
UC_W = C_Q_LORA + C_KV_LORA + LANE
SUB = 16
LOG2E = 1.4426950408889634
VMEM_LIMIT = 56 * 1024 * 1024


def _cparams(sem):
    return pltpu.CompilerParams(dimension_semantics=sem, vmem_limit_bytes=VMEM_LIMIT)


def _const_spec(shape):
    nd = len(shape)
    return pl.BlockSpec(shape, lambda *_: (0,) * nd, pipeline_mode=pl.Buffered(1))


def _rms(x, w):
    return x * lax.rsqrt(jnp.mean(x * x, axis=-1, keepdims=True) + EPS) * w


def _silu(x):
    return x * jax.nn.sigmoid(x)


def _bdot(a, b):
    return jnp.dot(a.astype(BF16), b.astype(BF16), preferred_element_type=F32)


def _bdot_nt(a, b):
    return lax.dot_general(a.astype(BF16), b.astype(BF16), (((1,), (1,)), ((), ())),
                           preferred_element_type=F32)


def _bdot_tn(a, b):
    return lax.dot_general(a.astype(BF16), b.astype(BF16), (((0,), (0,)), ((), ())),
                           preferred_element_type=F32)


def _iota(shape, dim):
    return lax.broadcasted_iota(jnp.int32, shape, dim)


def _block_mask(rows, cols, rb, cb):
    return (_iota((rows, cols), 0) // rb) == (_iota((rows, cols), 1) // cb)


def _block_diag(x, mask):
    n = mask.shape[0] // x.shape[0]
    xb = x.astype(BF16)
    return jnp.where(mask, jnp.concatenate([xb] * n, axis=0), jnp.zeros_like(xb[:1, :1]))


def _diag_blocks(m, rb, cb):
    n = m.shape[0] // rb
    lane_blk = _iota((rb, m.shape[1]), 1) // cb
    out = jnp.zeros((rb, m.shape[1]), m.dtype)
    for h in range(n):
        out = jnp.where(lane_blk == h, m[h * rb:(h + 1) * rb, :], out)
    return out


def _seg_sum(x, ones_bd):
    hi = x.astype(BF16)
    lo = (x - hi.astype(F32)).astype(BF16)
    return (jnp.dot(hi, ones_bd, preferred_element_type=F32)
            + jnp.dot(lo, ones_bd, preferred_element_type=F32))


def _expand_heads(x, off, nh, w):
    lane_blk = _iota((x.shape[0], nh * w), 1) // w
    out = jnp.zeros((x.shape[0], nh * w), x.dtype)
    for h in range(nh):
        out = jnp.where(lane_blk == h, x[:, off + h:off + h + 1], out)
    return out


def _cumsum_rows(x):
    n = x.shape[0]
    row = _iota(x.shape, 0)
    s = 1
    while s < n:
        x = x + jnp.where(row >= s, pltpu.roll(x, s, axis=0), 0.0)
        s *= 2
    return x


def _ffn_kernel(x_ref, nw_ref, wg_ref, wu_ref, wd_ref, fn_ref, o_ref, *, final):
    x = x_ref[...]
    xn = _rms(x, nw_ref[...]).astype(BF16)
    g = jnp.dot(xn, wg_ref[...], preferred_element_type=F32)
    u = jnp.dot(xn, wu_ref[...], preferred_element_type=F32)
    h = (_silu(g) * u).astype(BF16)
    y = x + 0.5 * jnp.dot(h, wd_ref[...], preferred_element_type=F32)
    if final:
        y = _rms(y, fn_ref[...])
    o_ref[...] = y


def _ffn(x, nw, wg, wu, wd, fn, *, final, tm):
    m = x.shape[0]
    return pl.pallas_call(
        functools.partial(_ffn_kernel, final=final),
        grid=(m // tm,),
        in_specs=[pl.BlockSpec((tm, D_MODEL), lambda i: (i, 0)),
                  _const_spec((1, D_MODEL)),
                  _const_spec((D_MODEL, D_FF)), _const_spec((D_MODEL, D_FF)),
                  _const_spec((D_FF, D_MODEL)), _const_spec((1, D_MODEL))],
        out_specs=pl.BlockSpec((tm, D_MODEL), lambda i: (i, 0)),
        out_shape=jax.ShapeDtypeStruct((m, D_MODEL), F32),
        compiler_params=_cparams(("parallel",)),
        name="ffn",
    )(x, nw, wg, wu, wd, fn)


def _norm_mm_kernel(x_ref, nw_ref, *refs, n_w):
    xn = _rms(x_ref[...], nw_ref[...]).astype(BF16)
    for w_ref, o_ref in zip(refs[:n_w], refs[n_w:]):
        o_ref[...] = jnp.dot(xn, w_ref[...], preferred_element_type=F32)


def _norm_mm(x, nw, ws, *, tm):
    m, d = x.shape
    return pl.pallas_call(
        functools.partial(_norm_mm_kernel, n_w=len(ws)),
        grid=(m // tm,),
        in_specs=[pl.BlockSpec((tm, d), lambda i: (i, 0)), _const_spec((1, d))]
                 + [_const_spec(w.shape) for w in ws],
        out_specs=[pl.BlockSpec((tm, w.shape[1]), lambda i: (i, 0)) for w in ws],
        out_shape=[jax.ShapeDtypeStruct((m, w.shape[1]), F32) for w in ws],
        compiler_params=_cparams(("parallel",)),
        name="norm_mm",
    )(x, nw, *ws)


def _mm_res_kernel(x_ref, *refs, n_a):
    acc = x_ref[...]
    for a_ref, w_ref in zip(refs[:n_a], refs[n_a:2 * n_a]):
        acc = acc + jnp.dot(a_ref[...].astype(BF16), w_ref[...], preferred_element_type=F32)
    refs[2 * n_a][...] = acc


def _mm_res(x, a_list, w_list, *, tm):
    m, d = x.shape
    n_a = len(a_list)
    return pl.pallas_call(
        functools.partial(_mm_res_kernel, n_a=n_a),
        grid=(m // tm,),
        in_specs=[pl.BlockSpec((tm, d), lambda i: (i, 0))]
                 + [pl.BlockSpec((tm, a.shape[1]), lambda i: (i, 0)) for a in a_list]
                 + [_const_spec(w.shape) for w in w_list],
        out_specs=pl.BlockSpec((tm, d), lambda i: (i, 0)),
        out_shape=jax.ShapeDtypeStruct((m, d), F32),
        compiler_params=_cparams(("parallel",)),
        name="mm_res",
    )(x, *a_list, *w_list)


def _gdn_prompt_kernel(u_ref, cw_ref, alog_ref, dtb_ref, anorm_ref, o_ref, conv_ref, s_out_ref,
                       xbuf, s_scr, *, tblk, nb):
    t = pl.program_id(1)
    hw = A_HEADS * A_DV
    c = CHUNK
    nc = tblk // c

    @pl.when(t == 0)
    def _():
        for bi in range(nb):
            xbuf[bi, 0:8, :] = jnp.zeros((8, A_CONV_CH), F32)
        s_scr[...] = jnp.zeros_like(s_scr)

    cw = cw_ref[...]
    ones_bd = _block_mask(hw, hw, A_DK, A_DK).astype(BF16)
    xs, zs, qs, ks, vs, betas, gs = [], [], [], [], [], [], []
    for bi in range(nb):
        x = u_ref[bi, :, 0:A_CONV_CH]
        xbuf[bi, 8:8 + tblk, :] = x
        y = (cw[0:1] * xbuf[bi, 5:5 + tblk, :] + cw[1:2] * xbuf[bi, 6:6 + tblk, :]
             + cw[2:3] * xbuf[bi, 7:7 + tblk, :] + cw[3:4] * x)
        xbuf[bi, 0:8, :] = x[tblk - 8:tblk, :]
        qkv = _silu(y)
        ba = u_ref[bi, :, A_CONV_CH + A_V:UA_W]
        q = qkv[:, 0:A_QK]
        k = qkv[:, A_QK:2 * A_QK]
        xs.append(x)
        zs.append(u_ref[bi, :, A_CONV_CH:A_CONV_CH + A_V])
        qs.append(q * lax.rsqrt(_seg_sum(q * q, ones_bd) + EPS) * (A_DK ** -0.5))
        ks.append(k * lax.rsqrt(_seg_sum(k * k, ones_bd) + EPS))
        vs.append(qkv[:, 2 * A_QK:])
        betas.append(_expand_heads(jax.nn.sigmoid(ba), 0, A_HEADS, A_DV))
        gs.append(_expand_heads(-jnp.exp(alog_ref[...]) * jax.nn.softplus(ba + dtb_ref[...]),
                                A_HEADS, A_HEADS, A_DV))

    bd = _block_mask(hw, hw, c, c)
    row = _iota((c, hw), 0)
    col = _iota((c, hw), 1) % c
    incl = row >= col
    strict = row > col
    items = [(ci, bi) for ci in range(nc) for bi in range(nb)]
    n_it = range(len(items))

    def chunks(arrs):
        return [arrs[bi][ci * c:(ci + 1) * c] for ci, bi in items]

    qc, kc, vc, bc = chunks(qs), chunks(ks), chunks(vs), chunks(betas)
    gcum = [_cumsum_rows(gi) for gi in chunks(gs)]
    grow = [jnp.sum(jnp.where(row == col, gi, 0.0), axis=0, keepdims=True) for gi in gcum]
    decay = [jnp.exp(jnp.where(incl, gcum[n] - grow[n], -jnp.inf)) for n in n_it]
    eg = [jnp.exp(gi) for gi in gcum]
    kb = [kc[n] * bc[n] for n in n_it]
    aq = [_bdot_nt(jnp.concatenate([kb[n], qc[n]], axis=0), _block_diag(kc[n], bd)) for n in n_it]
    a = [jnp.where(strict, aq[n][:c] * decay[n], 0.0) for n in n_it]
    qk = [aq[n][c:] * decay[n] for n in n_it]
    p = [-ai for ai in a]
    pw = [_bdot(ai, _block_diag(ai, bd)) for ai in a]
    n_sq = int(np.log2(c)) - 1
    for r in range(n_sq):
        if r < n_sq - 1:
            both = [_bdot(jnp.concatenate([p[n], pw[n]], axis=0), _block_diag(pw[n], bd)) for n in n_it]
            p = [p[n] + pw[n] + both[n][:c] for n in n_it]
            pw = [both[n][c:] for n in n_it]
        else:
            p = [p[n] + pw[n] + _bdot(p[n], _block_diag(pw[n], bd)) for n in n_it]
    vb = [vc[n] * bc[n] for n in n_it]
    kbg = [kb[n] * eg[n] for n in n_it]
    uw = [_bdot(p[n], jnp.concatenate([_block_diag(vb[n], bd), _block_diag(kbg[n], bd)], axis=1)) for n in n_it]
    u = [vb[n] + uw[n][:, :hw] for n in n_it]
    wq = [jnp.concatenate([kbg[n] + uw[n][:, hw:], qc[n] * eg[n]], axis=0) for n in n_it]
    g_last = [gi[c - 1:c, :] for gi in gcum]
    kd = [kc[n] * jnp.exp(g_last[n] - gcum[n]) for n in n_it]
    eg_last = [jnp.exp(gl) for gl in g_last]

    s = [s_scr[bi] for bi in range(nb)]
    outs = [[] for _ in range(nb)]
    for n, (ci, bi) in enumerate(items):
        ws = _bdot(wq[n], _block_diag(s[bi], bd))
        r = u[n] - ws[:c]
        outs[bi].append(ws[c:] + _bdot(qk[n], _block_diag(r, bd)))
        s[bi] = eg_last[n] * s[bi] + _diag_blocks(_bdot_tn(kd[n], r), A_DK, A_DV)

    for bi in range(nb):
        s_scr[bi] = s[bi]
        o = jnp.concatenate(outs[bi], axis=0)
        o = o * lax.rsqrt(_seg_sum(o * o, ones_bd) * (1.0 / A_DV) + EPS) * anorm_ref[...]
        o_ref[bi] = o * _silu(zs[bi])

    @pl.when(t == pl.num_programs(1) - 1)
    def _():
        for bi in range(nb):
            conv_ref[bi] = xs[bi][tblk - (CONV_W - 1):tblk, :]
            s_out_ref[bi] = s[bi]


def _gdn_prompt(u_a, cw, alog, dtb, anorm, *, tblk, nb):
    b, t, _ = u_a.shape
    hw = A_HEADS * A_DV
    return pl.pallas_call(
        functools.partial(_gdn_prompt_kernel, tblk=tblk, nb=nb),
        grid=(b // nb, t // tblk),
        in_specs=[pl.BlockSpec((nb, tblk, UA_W), lambda i, j: (i, j, 0)),
                  _const_spec((CONV_W, A_CONV_CH)), _const_spec((1, LANE)), _const_spec((1, LANE)),
                  _const_spec((1, hw))],
        out_specs=[pl.BlockSpec((nb, tblk, hw), lambda i, j: (i, j, 0)),
                   pl.BlockSpec((nb, CONV_W - 1, A_CONV_CH), lambda i, j: (i, 0, 0)),
                   pl.BlockSpec((nb, A_DK, hw), lambda i, j: (i, 0, 0))],
        out_shape=[jax.ShapeDtypeStruct((b, t, hw), F32),
                   jax.ShapeDtypeStruct((b, CONV_W - 1, A_CONV_CH), F32),
                   jax.ShapeDtypeStruct((b, A_DK, hw), F32)],
        scratch_shapes=[pltpu.VMEM((nb, tblk + 8, A_CONV_CH), F32), pltpu.VMEM((nb, A_DK, hw), F32)],
        compiler_params=_cparams(("parallel", "arbitrary")),
        name="gdn_prompt",
    )(u_a, cw, alog, dtb, anorm)


def _gla_prompt_kernel(u_ref, w2_ref, gb_ref, bnorm_ref, o_ref, s_out_ref, s_scr, *, tblk):
    t = pl.program_id(1)
    vw = B_HEADS * B_DV

    @pl.when(t == 0)
    def _():
        s_scr[...] = jnp.zeros_like(s_scr)

    u = u_ref[0]
    q = u[:, 0:B_QK] * (B_DK ** -0.5)
    k = u[:, B_QK:2 * B_QK]
    v = u[:, 2 * B_QK:2 * B_QK + B_V]
    rg = u[:, 2 * B_QK + B_V:2 * B_QK + 2 * B_V]
    glr = u[:, 2 * B_QK + 2 * B_V:]
    log_a = jax.nn.log_sigmoid(_bdot(glr, w2_ref[...]) + gb_ref[...]) / B_GATE_TAU

    c = CHUNK
    kmask = _block_mask(vw, B_QK, c, B_DK)
    vmask = _block_mask(vw, vw, c, B_DV)
    smask = _block_mask(vw, B_QK, B_DV, B_DK)
    row_k = _iota((c, B_QK), 0)
    row_s = _iota((SUB, vw), 0)
    col_s = _iota((SUB, vw), 1) % c
    st = s_scr[...]
    outs = []
    for ci in range(tblk // c):
        sl = slice(ci * c, (ci + 1) * c)
        qc, kc, vc = q[sl], k[sl], v[sl]
        b = _cumsum_rows(log_a[sl])
        o = _bdot_nt(qc * jnp.exp(b), st)
        att = []
        for i in range(c // SUB):
            r0 = i * SUB
            bref = b[r0:r0 + 1, :]
            qs = qc[r0:r0 + SUB] * jnp.exp(b[r0:r0 + SUB] - bref)
            ks = kc * jnp.exp(jnp.where(row_k < r0 + SUB, bref - b, -jnp.inf))
            a_i = _bdot_nt(qs, _block_diag(ks, kmask))
            att.append(jnp.where(row_s + r0 >= col_s, a_i, 0.0))
        o = o + _bdot(jnp.concatenate(att, axis=0), _block_diag(vc, vmask))
        outs.append(o)
        b_last = b[c - 1:c, :]
        kd = kc * jnp.exp(b_last - b)
        st = jnp.exp(b_last) * st + jnp.where(smask, _bdot_tn(vc, kd), 0.0)
    s_scr[...] = st

    o = jnp.concatenate(outs, axis=0)
    ones_bd = _block_mask(vw, vw, B_DV, B_DV).astype(BF16)
    o = o * lax.rsqrt(_seg_sum(o * o, ones_bd) * (1.0 / B_DV) + EPS) * bnorm_ref[...]
    o_ref[0] = o * _silu(rg)

    @pl.when(t == pl.num_programs(1) - 1)
    def _():
        s_out_ref[0] = st


def _gla_prompt(u_b, w2, gb, bnorm, *, tblk):
    b, t, _ = u_b.shape
    vw = B_HEADS * B_DV
    return pl.pallas_call(
        functools.partial(_gla_prompt_kernel, tblk=tblk),
        grid=(b, t // tblk),
        in_specs=[pl.BlockSpec((1, tblk, UB_W), lambda i, j: (i, j, 0)),
                  _const_spec((LANE, B_QK)), _const_spec((1, B_QK)), _const_spec((1, vw))],
        out_specs=[pl.BlockSpec((1, tblk, vw), lambda i, j: (i, j, 0)),
                   pl.BlockSpec((1, vw, B_QK), lambda i, j: (i, 0, 0))],
        out_shape=[jax.ShapeDtypeStruct((b, t, vw), F32),
                   jax.ShapeDtypeStruct((b, vw, B_QK), F32)],
        scratch_shapes=[pltpu.VMEM((vw, B_QK), F32)],
        compiler_params=_cparams(("parallel", "arbitrary")),
        name="gla_prompt",
    )(u_b, w2, gb, bnorm)


def _mla_prep_kernel(u_ref, cqn_ref, ckvn_ref, wuq_ref, wuk_ref, ck_ref, sk_ref, cq_ref, sq_ref,
                     ckv_o, kpe_o, ckvb_o, kpeb_o, qlat_o, qpe_o):
    u = u_ref[0]
    cq = _rms(u[:, 0:C_Q_LORA], cqn_ref[...]).astype(BF16)
    ckv = _rms(u[:, C_Q_LORA:C_Q_LORA + C_KV_LORA], ckvn_ref[...])
    half = C_ROPE // 2
    kx = u[:, C_Q_LORA + C_KV_LORA:]
    lane = _iota(kx.shape, 1)
    kswap = jnp.where(lane < half, pltpu.roll(kx, LANE - half, axis=1), pltpu.roll(kx, half, axis=1))
    kpe = (kx * ck_ref[...] + kswap * sk_ref[...])[:, 0:C_ROPE]
    ckv_o[0] = ckv
    kpe_o[0] = kpe
    ckvb_o[0] = ckv.astype(BF16)
    kpeb_o[0] = kpe.astype(BF16)

    qf = jnp.dot(cq, wuq_ref[...], preferred_element_type=F32)
    nope_w = C_HEADS * C_NOPE
    rope_w = C_HEADS * C_ROPE
    qr = qf[:, nope_w:]
    lane_r = _iota(qr.shape, 1) % C_ROPE
    qswap = jnp.where(lane_r < half, pltpu.roll(qr, rope_w - half, axis=1), pltpu.roll(qr, half, axis=1))
    qpe = qr * cq_ref[...] + qswap * sq_ref[...]
    for h in range(C_HEADS):
        qlat_o[0, h] = jnp.dot(qf[:, h * C_NOPE:(h + 1) * C_NOPE].astype(BF16), wuk_ref[h],
                               preferred_element_type=F32).astype(BF16)
        qpe_o[0, h] = qpe[:, h * C_ROPE:(h + 1) * C_ROPE].astype(BF16)


def _mla_prep(u_c, cqn, ckvn, wuq, wuk, tabs, *, tm):
    b, t, _ = u_c.shape
    ck, sk, cq, sq = tabs
    per_pos = ck.shape[0] != 1
    rope_w = C_HEADS * C_ROPE

    def tab_spec(w):
        if per_pos:
            return pl.BlockSpec((tm, w), lambda i, j: (j, 0))
        return _const_spec((1, w))

    def tok_spec(w):
        return pl.BlockSpec((1, tm, w), lambda i, j: (i, j, 0))

    def head_spec(w):
        return pl.BlockSpec((1, C_HEADS, tm, w), lambda i, j: (i, 0, j, 0))

    return pl.pallas_call(
        _mla_prep_kernel,
        grid=(b, t // tm),
        in_specs=[tok_spec(UC_W), _const_spec((1, C_Q_LORA)), _const_spec((1, C_KV_LORA)),
                  _const_spec(wuq.shape), _const_spec(wuk.shape),
                  tab_spec(LANE), tab_spec(LANE), tab_spec(rope_w), tab_spec(rope_w)],
        out_specs=[tok_spec(C_KV_LORA), tok_spec(C_ROPE), tok_spec(C_KV_LORA), tok_spec(C_ROPE),
                   head_spec(C_KV_LORA), head_spec(C_ROPE)],
        out_shape=[jax.ShapeDtypeStruct((b, t, C_KV_LORA), F32),
                   jax.ShapeDtypeStruct((b, t, C_ROPE), F32),
                   jax.ShapeDtypeStruct((b, t, C_KV_LORA), BF16),
                   jax.ShapeDtypeStruct((b, t, C_ROPE), BF16),
                   jax.ShapeDtypeStruct((b, C_HEADS, t, C_KV_LORA), BF16),
                   jax.ShapeDtypeStruct((b, C_HEADS, t, C_ROPE), BF16)],
        compiler_params=_cparams(("parallel", "parallel")),
        name="mla_prep",
    )(u_c, cqn, ckvn, wuq, wuk, ck, sk, cq, sq)


def _mla_flash_kernel(qi_ref, kj_ref, ql_ref, qp_ref, k_ref, p_ref, o_ref,
                      m_scr, l_scr, acc_scr, *, tq, tk):
    g = pl.program_id(1)
    i = qi_ref[g]
    j = kj_ref[g]
    last_j = (i * tq + tq - 1) // tk
    c2 = MLA_SCALE * LOG2E

    @pl.when(j == 0)
    def _():
        m_scr[...] = jnp.full_like(m_scr, -jnp.inf)
        l_scr[...] = jnp.zeros_like(l_scr)
        acc_scr[...] = jnp.zeros_like(acc_scr)

    def step(masked):
        kv = k_ref[0]
        pe = p_ref[0]
        if masked:
            visible = j * tk + _iota((tq, tk), 1) <= i * tq + _iota((tq, tk), 0)
        for h in range(C_HEADS):
            s = (lax.dot_general(ql_ref[0, h], kv, (((1,), (1,)), ((), ())), preferred_element_type=F32)
                 + lax.dot_general(qp_ref[0, h], pe, (((1,), (1,)), ((), ())), preferred_element_type=F32))
            if masked:
                s = jnp.where(visible, s, -jnp.inf)
            m_old = m_scr[h]
            m_new = jnp.maximum(m_old, jnp.max(s, axis=-1, keepdims=True))
            alpha = jnp.exp2((m_old - m_new) * c2)
            p = jnp.exp2(s * c2 - m_new * c2)
            l_scr[h] = alpha * l_scr[h] + jnp.sum(p, axis=-1, keepdims=True)
            m_scr[h] = m_new
            acc_scr[h] = alpha * acc_scr[h] + jnp.dot(p.astype(BF16), kv, preferred_element_type=F32)

    has_masked = j * tk + tk - 1 > i * tq

    @pl.when(has_masked)
    def _():
        step(True)

    @pl.when(jnp.logical_not(has_masked))
    def _():
        step(False)

    @pl.when(j == last_j)
    def _():
        o_ref[0] = (acc_scr[...] / l_scr[...]).astype(BF16)


def _mla_flash(qlat, qpe, ckvb, kpeb, *, tq, tk):
    b, _, t, _ = qlat.shape
    pairs = [(i, j) for i in range(t // tq) for j in range((i * tq + tq - 1) // tk + 1)]
    qi = jnp.asarray([p[0] for p in pairs], jnp.int32)
    kj = jnp.asarray([p[1] for p in pairs], jnp.int32)
    grid_spec = pltpu.PrefetchScalarGridSpec(
        num_scalar_prefetch=2,
        grid=(b, len(pairs)),
        in_specs=[pl.BlockSpec((1, C_HEADS, tq, C_KV_LORA), lambda bi, g, qi, kj: (bi, 0, qi[g], 0)),
                  pl.BlockSpec((1, C_HEADS, tq, C_ROPE), lambda bi, g, qi, kj: (bi, 0, qi[g], 0)),
                  pl.BlockSpec((1, tk, C_KV_LORA), lambda bi, g, qi, kj: (bi, kj[g], 0)),
                  pl.BlockSpec((1, tk, C_ROPE), lambda bi, g, qi, kj: (bi, kj[g], 0))],
        out_specs=pl.BlockSpec((1, C_HEADS, tq, C_KV_LORA), lambda bi, g, qi, kj: (bi, 0, qi[g], 0)),
        scratch_shapes=[pltpu.VMEM((C_HEADS, tq, 1), F32), pltpu.VMEM((C_HEADS, tq, 1), F32),
                        pltpu.VMEM((C_HEADS, tq, C_KV_LORA), F32)],
    )
    return pl.pallas_call(
        functools.partial(_mla_flash_kernel, tq=tq, tk=tk),
        grid_spec=grid_spec,
        out_shape=jax.ShapeDtypeStruct((b, C_HEADS, t, C_KV_LORA), BF16),
        compiler_params=_cparams(("parallel", "arbitrary")),
        name="mla_flash",
    )(qi, kj, qlat, qpe, ckvb, kpeb)


def _mla_oproj_kernel(ol_ref, wuv_ref, o_ref):
    o_ref[0] = jnp.concatenate(
        [jnp.dot(ol_ref[0, h].astype(BF16), wuv_ref[h], preferred_element_type=F32)
         for h in range(C_HEADS)], axis=-1)


def _mla_oproj(olat, wuv, *, tm):
    b, _, t, _ = olat.shape
    return pl.pallas_call(
        _mla_oproj_kernel,
        grid=(b, t // tm),
        in_specs=[pl.BlockSpec((1, C_HEADS, tm, C_KV_LORA), lambda i, j: (i, 0, j, 0)),
                  _const_spec(wuv.shape)],
        out_specs=pl.BlockSpec((1, tm, C_HEADS * C_VDIM), lambda i, j: (i, j, 0)),
        out_shape=jax.ShapeDtypeStruct((b, t, C_HEADS * C_VDIM), F32),
        compiler_params=_cparams(("parallel", "parallel")),
        name="mla_oproj",
    )(olat, wuv)


def _mla_paged_kernel(pt_ref, ql_ref, qp_ref, cn_ref, pn_ref, ckv_hbm, kpe_hbm, o_ref,
                      kbuf, pbuf, sem, m_scr, l_scr, acc_scr, *, layer, ppc, nch, page):
    b = pl.program_id(0)
    c = pl.program_id(1)
    g = b * nch + c
    total = pl.num_programs(0) * nch
    slot = g % 2

    def copies(bb, cc, sl, p):
        pid = pt_ref[bb, cc * ppc + p]
        dst = pl.ds(pl.multiple_of(p * page, page), page)
        return (pltpu.make_async_copy(ckv_hbm.at[layer, pid], kbuf.at[sl, dst, :], sem.at[0, sl]),
                pltpu.make_async_copy(kpe_hbm.at[layer, pid], pbuf.at[sl, :, dst], sem.at[1, sl]))

    def issue(bb, cc, sl):
        def body(p, carry):
            for cp in copies(bb, cc, sl, p):
                cp.start()
            return carry
        lax.fori_loop(0, ppc, body, 0)

    @pl.when(g == 0)
    def _():
        issue(0, 0, 0)

    @pl.when(g + 1 < total)
    def _():
        issue((g + 1) // nch, (g + 1) % nch, 1 - slot)

    def wait_body(p, carry):
        for cp in copies(b, c, slot, p):
            cp.wait()
        return carry
    lax.fori_loop(0, ppc, wait_body, 0)

    @pl.when(c == 0)
    def _():
        m_scr[...] = jnp.full_like(m_scr, -jnp.inf)
        l_scr[...] = jnp.zeros_like(l_scr)
        acc_scr[...] = jnp.zeros_like(acc_scr)

    ql = ql_ref[0]
    qp = qp_ref[0]
    kv = kbuf[slot].astype(BF16)
    pe = pbuf[slot].astype(BF16)
    s = (lax.dot_general(ql, kv, (((1,), (1,)), ((), ())), preferred_element_type=F32)
         + jnp.dot(qp, pe, preferred_element_type=F32)) * MLA_SCALE
    m_old = m_scr[...]
    m_new = jnp.maximum(m_old, jnp.max(s, axis=-1, keepdims=True))
    alpha = jnp.exp(m_old - m_new)
    p = jnp.exp(s - m_new)
    l_new = alpha * l_scr[...] + jnp.sum(p, axis=-1, keepdims=True)
    acc_new = alpha * acc_scr[...] + jnp.dot(p.astype(BF16), kv, preferred_element_type=F32)
    m_scr[...] = m_new
    l_scr[...] = l_new
    acc_scr[...] = acc_new

    @pl.when(c == nch - 1)
    def _():
        cn = cn_ref[0].astype(BF16).astype(F32)
        pn = pn_ref[0].astype(BF16).astype(F32)
        s_n = (jnp.sum(ql.astype(F32) * cn, axis=-1, keepdims=True)
               + jnp.sum(qp.astype(F32) * pn, axis=-1, keepdims=True)) * MLA_SCALE
        m_f = jnp.maximum(m_new, s_n)
        a_f = jnp.exp(m_new - m_f)
        p_n = jnp.exp(s_n - m_f)
        l_f = a_f * l_new + p_n
        acc_f = a_f * acc_new + p_n.astype(BF16).astype(F32) * cn
        o_ref[0] = acc_f / l_f


def _mla_paged(page_table, qlat, qpe, ckv_new, kpe_new, cache_ckv, cache_kpe, *, layer, ppc):
    b, hp, _ = qlat.shape
    n_pages = page_table.shape[1]
    page = cache_ckv.shape[2]
    nch = n_pages // ppc
    grid_spec = pltpu.PrefetchScalarGridSpec(
        num_scalar_prefetch=1,
        grid=(b, nch),
        in_specs=[pl.BlockSpec((1, hp, C_KV_LORA), lambda i, j, pt: (i, 0, 0)),
                  pl.BlockSpec((1, hp, C_ROPE), lambda i, j, pt: (i, 0, 0)),
                  pl.BlockSpec((1, 1, C_KV_LORA), lambda i, j, pt: (i, 0, 0)),
                  pl.BlockSpec((1, 1, C_ROPE), lambda i, j, pt: (i, 0, 0)),
                  pl.BlockSpec(memory_space=pl.ANY),
                  pl.BlockSpec(memory_space=pl.ANY)],
        out_specs=pl.BlockSpec((1, hp, C_KV_LORA), lambda i, j, pt: (i, 0, 0)),
        scratch_shapes=[pltpu.VMEM((2, ppc * page, C_KV_LORA), F32),
                        pltpu.VMEM((2, C_ROPE, ppc * page), F32),
                        pltpu.SemaphoreType.DMA((2, 2)),
                        pltpu.VMEM((hp, 1), F32), pltpu.VMEM((hp, 1), F32),
                        pltpu.VMEM((hp, C_KV_LORA), F32)],
    )
    return pl.pallas_call(
        functools.partial(_mla_paged_kernel, layer=layer, ppc=ppc, nch=nch, page=page),
        grid_spec=grid_spec,
        out_shape=jax.ShapeDtypeStruct((b, hp, C_KV_LORA), F32),
        compiler_params=_cparams(("arbitrary", "arbitrary")),
        name="mla_paged",
    )(page_table, qlat, qpe, ckv_new, kpe_new, cache_ckv, cache_kpe)


def _xattn_prompt_kernel(x_ref, nw_ref, wq_ref, wo_ref, mk_ref, mv_ref, o_ref):
    x = x_ref[0]
    xn = _rms(x, nw_ref[...]).astype(BF16)
    q = jnp.dot(xn, wq_ref[...], preferred_element_type=F32).astype(BF16)
    heads = []
    for h in range(X_HEADS):
        sl = slice(h * X_HDIM, (h + 1) * X_HDIM)
        s = lax.dot_general(q[:, sl], mk_ref[0, :, sl], (((1,), (1,)), ((), ())),
                            preferred_element_type=F32) * (X_HDIM ** -0.5)
        e = jnp.exp(s - jnp.max(s, axis=-1, keepdims=True))
        p = (e / jnp.sum(e, axis=-1, keepdims=True)).astype(BF16)
        heads.append(jnp.dot(p, mv_ref[0, :, sl], preferred_element_type=F32).astype(BF16))
    o_ref[0] = x + jnp.dot(jnp.concatenate(heads, axis=-1), wo_ref[...], preferred_element_type=F32)


def _xattn_prompt(x, nw, wq, wo, mk, mv, *, tm):
    b, t, d = x.shape
    n_mem = mk.shape[1]
    return pl.pallas_call(
        _xattn_prompt_kernel,
        grid=(b, t // tm),
        in_specs=[pl.BlockSpec((1, tm, d), lambda i, j: (i, j, 0)), _const_spec((1, d)),
                  _const_spec(wq.shape), _const_spec(wo.shape),
                  pl.BlockSpec((1, n_mem, d), lambda i, j: (i, 0, 0)),
                  pl.BlockSpec((1, n_mem, d), lambda i, j: (i, 0, 0))],
        out_specs=pl.BlockSpec((1, tm, d), lambda i, j: (i, j, 0)),
        out_shape=jax.ShapeDtypeStruct((b, t, d), F32),
        compiler_params=_cparams(("parallel", "parallel")),
        name="xattn_prompt",
    )(x, nw, wq, wo, mk, mv)


def _xattn_decode_kernel(q_ref, mk_ref, mv_ref, o_ref):
    for bi in range(q_ref.shape[0]):
        _xattn_decode_one(q_ref, mk_ref, mv_ref, o_ref, bi)


def _xattn_decode_one(q_ref, mk_ref, mv_ref, o_ref, bi):
    nt = X_HDIM // LANE
    grp = nt * X_HEADS
    n_rows = mk_ref.shape[2]
    qrow = q_ref[bi]
    qm = jnp.concatenate([qrow[:, h * X_HDIM + t * LANE:h * X_HDIM + (t + 1) * LANE]
                          for t in range(nt) for h in range(X_HEADS)], axis=0)
    sel = _iota((grp, n_rows), 0) == _iota((grp, n_rows), 1) % grp
    s_all = _bdot_nt(qm, mk_ref[0, bi])
    part = jnp.sum(jnp.where(sel, s_all, 0.0), axis=0, keepdims=True)
    lane = _iota((1, n_rows), 1)
    s = part
    for t in range(1, nt):
        s = s + jnp.where(lane % grp < X_HEADS, pltpu.roll(part, n_rows - t * X_HEADS, axis=1),
                          pltpu.roll(part, t * X_HEADS, axis=1))
    s = s * (X_HDIM ** -0.5)
    p = jnp.zeros_like(s)
    for h in range(X_HEADS):
        mine = lane % X_HEADS == h
        e = jnp.exp(s - jnp.max(jnp.where(mine, s, -jnp.inf), axis=-1, keepdims=True))
        den = jnp.sum(jnp.where(lane % grp == h, e, 0.0), axis=-1, keepdims=True)
        p = jnp.where(mine, e / den, p)
    o = _bdot(jnp.where(sel, p, 0.0), mv_ref[0, bi])
    o_ref[bi] = jnp.concatenate([o[t * X_HEADS + h:t * X_HEADS + h + 1]
                                for h in range(X_HEADS) for t in range(nt)], axis=-1)


def _mem_rows(mem):
    dep, b, n_mem, nh, hd = mem.shape
    nt = hd // LANE
    return mem.reshape(dep, b, n_mem, nh, nt, LANE).transpose(0, 1, 2, 4, 3, 5).reshape(dep, b, n_mem * nt * nh, LANE)


def _xattn_decode(q, mk_rows, mv_rows, *, layer, nb):
    _, b, n_rows, _ = mk_rows.shape
    d = X_HEADS * X_HDIM
    assert X_HDIM == 2 * LANE
    mem_spec = pl.BlockSpec((1, nb, n_rows, LANE), lambda i: (layer, i, 0, 0))
    return pl.pallas_call(
        _xattn_decode_kernel,
        grid=(b // nb,),
        in_specs=[pl.BlockSpec((nb, 1, d), lambda i: (i, 0, 0)), mem_spec, mem_spec],
        out_specs=pl.BlockSpec((nb, 1, d), lambda i: (i, 0, 0)),
        out_shape=jax.ShapeDtypeStruct((b, 1, d), F32),
        compiler_params=_cparams(("parallel",)),
        name="xattn_decode",
    )(q, mk_rows, mv_rows)


def _columns(rows):
    w = rows[0].shape[1]
    mat = jnp.concatenate(rows + [jnp.zeros((LANE - len(rows), w), F32)], axis=0)
    return mat.T


def _gdn_decode_kernel(u_ref, cs_ref, s_ref, cw_ref, alog_ref, dtb_ref, anorm_ref,
                       o_ref, cs_out_ref, s_out_ref):
    for bi in range(u_ref.shape[0]):
        _gdn_decode_one(u_ref, cs_ref, s_ref, cw_ref, alog_ref, dtb_ref, anorm_ref,
                        o_ref, cs_out_ref, s_out_ref, bi)


def _gdn_decode_one(u_ref, cs_ref, s_ref, cw_ref, alog_ref, dtb_ref, anorm_ref,
                    o_ref, cs_out_ref, s_out_ref, bi):
    x = u_ref[bi, :, 0:A_CONV_CH]
    z = u_ref[bi, :, A_CONV_CH:A_CONV_CH + A_V]
    ba = u_ref[bi, :, A_CONV_CH + A_V:UA_W]
    cs = cs_ref[0, bi]
    cw = cw_ref[...]
    y = cw[0:1] * cs[0:1] + cw[1:2] * cs[1:2] + cw[2:3] * cs[2:3] + cw[3:4] * x
    cs_out_ref[bi] = jnp.concatenate([cs[1:CONV_W - 1], x], axis=0)
    qkv = _silu(y)
    beta = jax.nn.sigmoid(ba)
    g = -jnp.exp(alog_ref[...]) * jax.nn.softplus(ba + dtb_ref[...])
    qs, ks, vs = [], [], []
    for h in range(A_HEADS):
        qh = qkv[:, h * A_DK:(h + 1) * A_DK]
        kh = qkv[:, A_QK + h * A_DK:A_QK + (h + 1) * A_DK]
        qs.append(qh * lax.rsqrt(jnp.sum(qh * qh, axis=-1, keepdims=True) + EPS) * (A_DK ** -0.5))
        ks.append(kh * lax.rsqrt(jnp.sum(kh * kh, axis=-1, keepdims=True) + EPS))
        vs.append(qkv[:, 2 * A_QK + h * A_DV:2 * A_QK + (h + 1) * A_DV])
    cols = _columns(ks + qs)
    outs = []
    for h in range(A_HEADS):
        s = s_ref[0, bi, h]
        kcol = cols[:, h:h + 1]
        qcol = cols[:, A_HEADS + h:A_HEADS + h + 1]
        bh = beta[:, h:h + 1]
        eg = jnp.exp(g[:, A_HEADS + h:A_HEADS + h + 1])
        ks_row = jnp.sum(kcol * s, axis=0, keepdims=True)
        qs_row = jnp.sum(qcol * s, axis=0, keepdims=True)
        r = vs[h] * bh - (bh * eg) * ks_row
        qk = jnp.sum(qs[h] * ks[h], axis=-1, keepdims=True)
        o = eg * qs_row + qk * r
        s_out_ref[bi, h] = eg * s + kcol * r
        o = o * lax.rsqrt(jnp.mean(o * o, axis=-1, keepdims=True) + EPS) * anorm_ref[...]
        outs.append(o)
    o_ref[bi] = jnp.concatenate(outs, axis=-1) * _silu(z)


def _gdn_decode(u_a, cs, s, cw, alog, dtb, anorm, *, layer, nb):
    b = u_a.shape[0]
    return pl.pallas_call(
        _gdn_decode_kernel,
        grid=(b // nb,),
        in_specs=[pl.BlockSpec((nb, 1, UA_W), lambda i: (i, 0, 0)),
                  pl.BlockSpec((1, nb, CONV_W - 1, A_CONV_CH), lambda i: (layer, i, 0, 0)),
                  pl.BlockSpec((1, nb, A_HEADS, A_DK, A_DV), lambda i: (layer, i, 0, 0, 0)),
                  _const_spec((CONV_W, A_CONV_CH)), _const_spec((1, LANE)), _const_spec((1, LANE)),
                  _const_spec((1, A_DV))],
        out_specs=[pl.BlockSpec((nb, 1, A_V), lambda i: (i, 0, 0)),
                   pl.BlockSpec((nb, CONV_W - 1, A_CONV_CH), lambda i: (i, 0, 0)),
                   pl.BlockSpec((nb, A_HEADS, A_DK, A_DV), lambda i: (i, 0, 0, 0))],
        out_shape=[jax.ShapeDtypeStruct((b, 1, A_V), F32),
                   jax.ShapeDtypeStruct((b, CONV_W - 1, A_CONV_CH), F32),
                   jax.ShapeDtypeStruct((b, A_HEADS, A_DK, A_DV), F32)],
        compiler_params=_cparams(("parallel",)),
        name="gdn_decode",
    )(u_a, cs, s, cw, alog, dtb, anorm)


def _gla_decode_kernel(u_ref, s_ref, w2_ref, gb_ref, bnorm_ref, o_ref, s_out_ref):
    for bi in range(u_ref.shape[0]):
        _gla_decode_one(u_ref, s_ref, w2_ref, gb_ref, bnorm_ref, o_ref, s_out_ref, bi)


def _gla_decode_one(u_ref, s_ref, w2_ref, gb_ref, bnorm_ref, o_ref, s_out_ref, bi):
    u = u_ref[bi]
    q = u[:, 0:B_QK] * (B_DK ** -0.5)
    k = u[:, B_QK:2 * B_QK]
    rg = u[:, 2 * B_QK + B_V:2 * B_QK + 2 * B_V]
    glr = jnp.broadcast_to(u[:, 2 * B_QK + 2 * B_V:], (8, LANE))
    log_a = jax.nn.log_sigmoid(_bdot(glr, w2_ref[...])[0:1] + gb_ref[...]) / B_GATE_TAU
    a = jnp.exp(log_a)
    cols = _columns([k, q * a, a])
    outs = []
    for h in range(B_HEADS):
        rs = slice(h * B_DK, (h + 1) * B_DK)
        s = s_ref[0, bi, h]
        vh = u[:, 2 * B_QK + h * B_DV:2 * B_QK + (h + 1) * B_DV]
        qk = jnp.sum(q[:, rs] * k[:, rs], axis=-1, keepdims=True)
        o = jnp.sum(cols[rs, 1:2] * s, axis=0, keepdims=True) + qk * vh
        s_out_ref[bi, h] = cols[rs, 2:3] * s + cols[rs, 0:1] * vh
        o = o * lax.rsqrt(jnp.mean(o * o, axis=-1, keepdims=True) + EPS) * bnorm_ref[...]
        outs.append(o)
    o_ref[bi] = jnp.concatenate(outs, axis=-1) * _silu(rg)


def _gla_decode(u_b, s, w2, gb, bnorm, *, layer, nb):
    b = u_b.shape[0]
    return pl.pallas_call(
        _gla_decode_kernel,
        grid=(b // nb,),
        in_specs=[pl.BlockSpec((nb, 1, UB_W), lambda i: (i, 0, 0)),
                  pl.BlockSpec((1, nb, B_HEADS, B_DK, B_DV), lambda i: (layer, i, 0, 0, 0)),
                  _const_spec((LANE, B_QK)), _const_spec((1, B_QK)), _const_spec((1, B_DV))],
        out_specs=[pl.BlockSpec((nb, 1, B_V), lambda i: (i, 0, 0)),
                   pl.BlockSpec((nb, B_HEADS, B_DK, B_DV), lambda i: (i, 0, 0, 0))],
        out_shape=[jax.ShapeDtypeStruct((b, 1, B_V), F32),
                   jax.ShapeDtypeStruct((b, B_HEADS, B_DK, B_DV), F32)],
        compiler_params=_cparams(("parallel",)),
        name="gla_decode",
    )(u_b, s, w2, gb, bnorm)


def _pad_cols(w, n):
    return jnp.pad(w, ((0, 0), (0, n - w.shape[1])))


def _rope_tables(pos):
    half = C_ROPE // 2
    inv = ROPE_THETA ** (-jnp.arange(half, dtype=F32) / half)
    ang = pos.astype(F32)[:, None] * inv[None, :]
    cos, sin = jnp.cos(ang), jnp.sin(ang)
    zero = jnp.zeros_like(cos)
    ck = jnp.concatenate([cos, cos, zero, zero], axis=-1)
    sk = jnp.concatenate([-sin, sin, zero, zero], axis=-1)
    cq = jnp.tile(jnp.concatenate([cos, cos], axis=-1), (1, C_HEADS))
    sq = jnp.tile(jnp.concatenate([-sin, sin], axis=-1), (1, C_HEADS))
    return ck, sk, cq, sq


def _layer_weights(l, P):
    w_in = P['w_in'][l]
    o_b, o_c = A_IN, A_IN + B_IN
    w_a = jnp.concatenate([w_in[:, :A_CONV_CH + A_V], _pad_cols(w_in[:, A_CONV_CH + A_V:A_IN], LANE)], axis=1)
    w_b = jnp.concatenate([w_in[:, o_b:o_b + 2 * B_QK + 2 * B_V],
                           _pad_cols(w_in[:, o_b + 2 * B_QK + 2 * B_V:o_c], LANE)], axis=1)
    w_c = jnp.concatenate([w_in[:, o_c:o_c + C_Q_LORA + C_KV_LORA],
                           _pad_cols(w_in[:, o_c + C_Q_LORA + C_KV_LORA:], LANE)], axis=1)
    wuq = P['c_w_uq'][l].reshape(C_Q_LORA, C_HEADS, C_NOPE + C_ROPE)
    wuq = jnp.concatenate([wuq[:, :, :C_NOPE].reshape(C_Q_LORA, -1),
                           wuq[:, :, C_NOPE:].reshape(C_Q_LORA, -1)], axis=1)
    w_out = P['w_out'][l]
    row = lambda a: a.reshape(1, -1).astype(F32)
    head_lanes = lambda a: jnp.pad(a.reshape(1, -1).astype(F32), ((0, 0), (A_HEADS, LANE - 2 * A_HEADS)))
    return dict(
        norm_ffn1=row(P['norm_ffn1'][l]), norm_ffn2=row(P['norm_ffn2'][l]),
        ffn1=tuple(P[n][l].astype(BF16) for n in ('w_ffn1_gate', 'w_ffn1_up', 'w_ffn1_down')),
        ffn2=tuple(P[n][l].astype(BF16) for n in ('w_ffn2_gate', 'w_ffn2_up', 'w_ffn2_down')),
        norm_mix=row(P['norm_mix'][l]),
        w_in=(w_a.astype(BF16), w_b.astype(BF16), w_c.astype(BF16)),
        a_conv_w=P['a_conv_w'][l].astype(F32),
        a_log=head_lanes(P['a_log'][l]), a_dt_bias=head_lanes(P['a_dt_bias'][l]),
        a_norm=row(P['a_norm'][l]), a_norm_t=row(jnp.tile(P['a_norm'][l], A_HEADS)),
        b_w2=jnp.pad(P['b_gate_w2'][l], ((0, LANE - B_GATE_RANK), (0, 0))).astype(BF16),
        b_gate_bias=row(P['b_gate_bias'][l]),
        b_norm=row(P['b_norm'][l]), b_norm_t=row(jnp.tile(P['b_norm'][l], B_HEADS)),
        c_q_norm=row(P['c_q_norm'][l]), c_kv_norm=row(P['c_kv_norm'][l]),
        c_w_uq=wuq.astype(BF16),
        c_w_uk=jnp.transpose(P['c_w_uk'][l], (1, 2, 0)).astype(BF16),
        c_w_uv=jnp.transpose(P['c_w_uv'][l], (1, 0, 2)).astype(BF16),
        w_out=(w_out[:A_V].astype(BF16), w_out[A_V:A_V + B_V].astype(BF16), w_out[A_V + B_V:].astype(BF16)),
        norm_x=row(P['norm_x'][l]), norm_mem=row(P['norm_mem'][l]),
        w_xq=P['w_xq'][l].astype(BF16), w_xo=P['w_xo'][l].astype(BF16),
        w_xk=P['w_xk'][l].astype(BF16), w_xv=P['w_xv'][l].astype(BF16),
    )


def _pick(n, pref):
    return pref if n % pref == 0 else n


def kernel(x_prompt, x_sample, mem_prompt, cache_ckv, cache_kpe, page_table, state_conv_a, state_delta, state_gla, cache_mem_k, cache_mem_v, norm_ffn1, w_ffn1_gate, w_ffn1_up, w_ffn1_down, norm_mix, w_in, a_conv_w, a_log, a_dt_bias, a_norm, b_gate_w2, b_gate_bias, b_norm, c_q_norm, c_w_uq, c_kv_norm, c_w_uk, c_w_uv, w_out, norm_x, norm_mem, w_xq, w_xk, w_xv, w_xo, norm_ffn2, w_ffn2_gate, w_ffn2_up, w_ffn2_down, final_norm):
    P = dict(norm_ffn1=norm_ffn1, w_ffn1_gate=w_ffn1_gate, w_ffn1_up=w_ffn1_up, w_ffn1_down=w_ffn1_down,
             norm_mix=norm_mix, w_in=w_in, a_conv_w=a_conv_w, a_log=a_log, a_dt_bias=a_dt_bias,
             a_norm=a_norm, b_gate_w2=b_gate_w2, b_gate_bias=b_gate_bias, b_norm=b_norm,
             c_q_norm=c_q_norm, c_w_uq=c_w_uq, c_kv_norm=c_kv_norm, c_w_uk=c_w_uk, c_w_uv=c_w_uv,
             w_out=w_out, norm_x=norm_x, norm_mem=norm_mem, w_xq=w_xq, w_xk=w_xk, w_xv=w_xv, w_xo=w_xo,
             norm_ffn2=norm_ffn2, w_ffn2_gate=w_ffn2_gate, w_ffn2_up=w_ffn2_up, w_ffn2_down=w_ffn2_down)
    depth = w_in.shape[0]
    W = [_layer_weights(l, P) for l in range(depth)]
    fnorm = final_norm.reshape(1, -1).astype(F32)

    bp, tp, d = x_prompt.shape
    mp = bp * tp
    n_mem = mem_prompt.shape[1]
    tm = _pick(tp, 512)
    tabs_p = _rope_tables(jnp.arange(tp, dtype=jnp.int32))
    x = x_prompt.reshape(mp, d)
    mem = mem_prompt.reshape(bp * n_mem, d)
    p_ckv, p_kpe, p_conv, p_delta, p_gla, p_mk, p_mv = [], [], [], [], [], [], []
    for l in range(depth):
        w = W[l]
        mk, mv = _norm_mm(mem, w['norm_mem'], [w['w_xk'], w['w_xv']], tm=_pick(bp * n_mem, 512))
        x = _ffn(x, w['norm_ffn1'], *w['ffn1'], fnorm, final=False, tm=tm)
        u_a, u_b, u_c = _norm_mm(x, w['norm_mix'], list(w['w_in']), tm=tm)
        o_a, conv_new, sd = _gdn_prompt(u_a.reshape(bp, tp, UA_W), w['a_conv_w'], w['a_log'], w['a_dt_bias'],
                                        w['a_norm_t'], tblk=_pick(tp, 256), nb=_pick(bp, 2))
        o_b, sg = _gla_prompt(u_b.reshape(bp, tp, UB_W), w['b_w2'], w['b_gate_bias'], w['b_norm_t'],
                              tblk=_pick(tp, 256))
        ckv, kpe, ckvb, kpeb, qlat, qpe = _mla_prep(u_c.reshape(bp, tp, UC_W), w['c_q_norm'], w['c_kv_norm'],
                                                    w['c_w_uq'], w['c_w_uk'], tabs_p, tm=tm)
        olat = _mla_flash(qlat, qpe, ckvb, kpeb, tq=_pick(tp, 512), tk=_pick(tp, 512))
        o_c = _mla_oproj(olat, w['c_w_uv'], tm=tm)
        x = _mm_res(x, [o_a.reshape(mp, A_V), o_b.reshape(mp, B_V), o_c.reshape(mp, C_HEADS * C_VDIM)],
                    list(w['w_out']), tm=tm)
        x = _xattn_prompt(x.reshape(bp, tp, d), w['norm_x'], w['w_xq'], w['w_xo'],
                          mk.reshape(bp, n_mem, d).astype(BF16), mv.reshape(bp, n_mem, d).astype(BF16),
                          tm=tm).reshape(mp, d)
        x = _ffn(x, w['norm_ffn2'], *w['ffn2'], fnorm, final=(l == depth - 1), tm=tm)
        p_ckv.append(ckv)
        p_kpe.append(kpe)
        p_conv.append(conv_new)
        p_delta.append(sd.reshape(bp, A_DK, A_HEADS, A_DV).transpose(0, 2, 1, 3))
        sg = sg.reshape(bp, B_HEADS, B_DV, B_HEADS, B_DK)
        p_gla.append(jnp.stack([sg[:, h, :, h, :] for h in range(B_HEADS)], axis=1).transpose(0, 1, 3, 2))
        p_mk.append(mk.reshape(bp, n_mem, X_HEADS, X_HDIM))
        p_mv.append(mv.reshape(bp, n_mem, X_HEADS, X_HDIM))
    y_prompt = x.reshape(bp, tp, d)

    bs, ts, _ = x_sample.shape
    n_pages, page = page_table.shape[1], cache_ckv.shape[2]
    past_len = n_pages * page
    tabs_s = _rope_tables(past_len + jnp.arange(ts, dtype=jnp.int32))
    cache_kpe_t = jnp.swapaxes(cache_kpe, 2, 3)
    mem_k_rows, mem_v_rows = _mem_rows(cache_mem_k), _mem_rows(cache_mem_v)
    nb_dec = _pick(bs, 4)
    x = x_sample.reshape(bs, d)
    s_ckv, s_kpe, s_conv, s_delta, s_gla = [], [], [], [], []
    for l in range(depth):
        w = W[l]
        x = _ffn(x, w['norm_ffn1'], *w['ffn1'], fnorm, final=False, tm=bs)
        u_a, u_b, u_c = _norm_mm(x, w['norm_mix'], list(w['w_in']), tm=bs)
        o_a, conv_new, sd = _gdn_decode(u_a.reshape(bs, 1, UA_W), state_conv_a, state_delta,
                                        w['a_conv_w'], w['a_log'], w['a_dt_bias'], w['a_norm'], layer=l,
                                        nb=nb_dec)
        o_b, sg = _gla_decode(u_b.reshape(bs, 1, UB_W), state_gla, w['b_w2'], w['b_gate_bias'], w['b_norm'],
                              layer=l, nb=nb_dec)
        ckv, kpe, _, _, qlat, qpe = _mla_prep(u_c.reshape(1, bs, UC_W), w['c_q_norm'], w['c_kv_norm'],
                                              w['c_w_uq'], w['c_w_uk'], tabs_s, tm=bs)
        pad_heads = lambda a: jnp.pad(a[0].transpose(1, 0, 2), ((0, 0), (0, 8 - C_HEADS), (0, 0)))
        olat = _mla_paged(page_table, pad_heads(qlat), pad_heads(qpe), ckv.reshape(bs, 1, C_KV_LORA),
                          kpe.reshape(bs, 1, C_ROPE), cache_ckv, cache_kpe_t, layer=l, ppc=_pick(n_pages, 64))
        o_c = _mla_oproj(olat[:, :C_HEADS].transpose(1, 0, 2)[None], w['c_w_uv'], tm=bs)
        x = _mm_res(x, [o_a.reshape(bs, A_V), o_b.reshape(bs, B_V), o_c.reshape(bs, C_HEADS * C_VDIM)],
                    list(w['w_out']), tm=bs)
        (q,) = _norm_mm(x, w['norm_x'], [w['w_xq']], tm=bs)
        att = _xattn_decode(q.reshape(bs, 1, d), mem_k_rows, mem_v_rows, layer=l, nb=nb_dec)
        x = _mm_res(x, [att.reshape(bs, d)], [w['w_xo']], tm=bs)
        x = _ffn(x, w['norm_ffn2'], *w['ffn2'], fnorm, final=(l == depth - 1), tm=bs)
        s_ckv.append(ckv.reshape(bs, ts, C_KV_LORA))
        s_kpe.append(kpe.reshape(bs, ts, C_ROPE))
        s_conv.append(conv_new)
        s_delta.append(sd)
        s_gla.append(sg)
    y_sample = x.reshape(bs, ts, d)

    return (y_prompt, y_sample,
            jnp.stack(p_ckv), jnp.stack(p_kpe), jnp.stack(p_conv), jnp.stack(p_delta), jnp.stack(p_gla),
            jnp.stack(p_mk), jnp.stack(p_mv),
            jnp.stack(s_ckv), jnp.stack(s_kpe), jnp.stack(s_conv), jnp.stack(s_delta), jnp.stack(s_gla))
```

```python
import functools

import jax
import jax.numpy as jnp
import numpy as np
from jax import lax
from jax.experimental import pallas as pl
from jax.experimental.pallas import tpu as pltpu

F32 = jnp.float32
BF16 = jnp.bfloat16

D_MODEL = 1024
A_HEADS, A_DK, A_DV, CONV_W = 4, 64, 64, 4
B_HEADS, B_DK, B_DV, B_GATE_RANK, B_GATE_TAU = 4, 32, 64, 16, 16.0
C_HEADS, C_NOPE, C_ROPE, C_VDIM, C_Q_LORA, C_KV_LORA = 4, 128, 64, 128, 384, 256
ROPE_THETA = 10000.0
X_HEADS, X_HDIM = 4, 256
D_FF = 2816
CHUNK = 64
EPS = 1e-6

A_QK = A_HEADS * A_DK
A_V = A_HEADS * A_DV
A_CONV_CH = 2 * A_QK + A_V
A_IN = A_CONV_CH + A_V + 2 * A_HEADS
B_QK = B_HEADS * B_DK
B_V = B_HEADS * B_DV
B_IN = 2 * B_QK + 2 * B_V + B_GATE_RANK
C_IN = C_Q_LORA + C_KV_LORA + C_ROPE
MLA_SCALE = (C_NOPE + C_ROPE) ** -0.5

LANE = 128
UA_W = A_CONV_CH + A_V + LANE
UB_W = 2 * B_QK + 2 * B_V + LANE
UC_W = C_Q_LORA + C_KV_LORA + LANE
SUB = 16
LOG2E = 1.4426950408889634
VMEM_LIMIT = 56 * 1024 * 1024


def _cparams(sem):
    return pltpu.CompilerParams(dimension_semantics=sem, vmem_limit_bytes=VMEM_LIMIT)


def _const_spec(shape):
    nd = len(shape)
    return pl.BlockSpec(shape, lambda *_: (0,) * nd, pipeline_mode=pl.Buffered(1))


def _rms(x, w):
    return x * lax.rsqrt(jnp.mean(x * x, axis=-1, keepdims=True) + EPS) * w


def _silu(x):
    return x * jax.nn.sigmoid(x)


def _bdot(a, b):
    return jnp.dot(a.astype(BF16), b.astype(BF16), preferred_element_type=F32)


def _bdot_nt(a, b):
    return lax.dot_general(a.astype(BF16), b.astype(BF16), (((1,), (1,)), ((), ())),
                           preferred_element_type=F32)


def _bdot_tn(a, b):
    return lax.dot_general(a.astype(BF16), b.astype(BF16), (((0,), (0,)), ((), ())),
                           preferred_element_type=F32)


def _iota(shape, dim):
    return lax.broadcasted_iota(jnp.int32, shape, dim)


def _block_mask(rows, cols, rb, cb):
    return (_iota((rows, cols), 0) // rb) == (_iota((rows, cols), 1) // cb)


def _block_diag(x, mask):
    n = mask.shape[0] // x.shape[0]
    xb = x.astype(BF16)
    return jnp.where(mask, jnp.concatenate([xb] * n, axis=0), jnp.zeros_like(xb[:1, :1]))


def _diag_blocks(m, rb, cb):
    n = m.shape[0] // rb
    lane_blk = _iota((rb, m.shape[1]), 1) // cb
    out = jnp.zeros((rb, m.shape[1]), m.dtype)
    for h in range(n):
        out = jnp.where(lane_blk == h, m[h * rb:(h + 1) * rb, :], out)
    return out


def _seg_sum(x, ones_bd):
    hi = x.astype(BF16)
    lo = (x - hi.astype(F32)).astype(BF16)
    return (jnp.dot(hi, ones_bd, preferred_element_type=F32)
            + jnp.dot(lo, ones_bd, preferred_element_type=F32))


def _expand_heads(x, off, nh, w):
    lane_blk = _iota((x.shape[0], nh * w), 1) // w
    out = jnp.zeros((x.shape[0], nh * w), x.dtype)
    for h in range(nh):
        out = jnp.where(lane_blk == h, x[:, off + h:off + h + 1], out)
    return out


def _cumsum_rows(x):
    n = x.shape[0]
    row = _iota(x.shape, 0)
    s = 1
    while s < n:
        x = x + jnp.where(row >= s, pltpu.roll(x, s, axis=0), 0.0)
        s *= 2
    return x


def _ffn_kernel(x_ref, nw_ref, wg_ref, wu_ref, wd_ref, fn_ref, o_ref, *, final):
    x = x_ref[...]
    xn = _rms(x, nw_ref[...]).astype(BF16)
    g = jnp.dot(xn, wg_ref[...], preferred_element_type=F32)
    u = jnp.dot(xn, wu_ref[...], preferred_element_type=F32)
    h = (_silu(g) * u).astype(BF16)
    y = x + 0.5 * jnp.dot(h, wd_ref[...], preferred_element_type=F32)
    if final:
        y = _rms(y, fn_ref[...])
    o_ref[...] = y


def _ffn(x, nw, wg, wu, wd, fn, *, final, tm):
    m = x.shape[0]
    return pl.pallas_call(
        functools.partial(_ffn_kernel, final=final),
        grid=(m // tm,),
        in_specs=[pl.BlockSpec((tm, D_MODEL), lambda i: (i, 0)),
                  _const_spec((1, D_MODEL)),
                  _const_spec((D_MODEL, D_FF)), _const_spec((D_MODEL, D_FF)),
                  _const_spec((D_FF, D_MODEL)), _const_spec((1, D_MODEL))],
        out_specs=pl.BlockSpec((tm, D_MODEL), lambda i: (i, 0)),
        out_shape=jax.ShapeDtypeStruct((m, D_MODEL), F32),
        compiler_params=_cparams(("parallel",)),
        name="ffn",
    )(x, nw, wg, wu, wd, fn)


def _norm_mm_kernel(x_ref, nw_ref, *refs, n_w):
    xn = _rms(x_ref[...], nw_ref[...]).astype(BF16)
    for w_ref, o_ref in zip(refs[:n_w], refs[n_w:]):
        o_ref[...] = jnp.dot(xn, w_ref[...], preferred_element_type=F32)


def _norm_mm(x, nw, ws, *, tm):
    m, d = x.shape
    return pl.pallas_call(
        functools.partial(_norm_mm_kernel, n_w=len(ws)),
        grid=(m // tm,),
        in_specs=[pl.BlockSpec((tm, d), lambda i: (i, 0)), _const_spec((1, d))]
                 + [_const_spec(w.shape) for w in ws],
        out_specs=[pl.BlockSpec((tm, w.shape[1]), lambda i: (i, 0)) for w in ws],
        out_shape=[jax.ShapeDtypeStruct((m, w.shape[1]), F32) for w in ws],
        compiler_params=_cparams(("parallel",)),
        name="norm_mm",
    )(x, nw, *ws)


def _mm_res_kernel(x_ref, *refs, n_a):
    acc = x_ref[...]
    for a_ref, w_ref in zip(refs[:n_a], refs[n_a:2 * n_a]):
        acc = acc + jnp.dot(a_ref[...].astype(BF16), w_ref[...], preferred_element_type=F32)
    refs[2 * n_a][...] = acc


def _mm_res(x, a_list, w_list, *, tm):
    m, d = x.shape
    n_a = len(a_list)
    return pl.pallas_call(
        functools.partial(_mm_res_kernel, n_a=n_a),
        grid=(m // tm,),
        in_specs=[pl.BlockSpec((tm, d), lambda i: (i, 0))]
                 + [pl.BlockSpec((tm, a.shape[1]), lambda i: (i, 0)) for a in a_list]
                 + [_const_spec(w.shape) for w in w_list],
        out_specs=pl.BlockSpec((tm, d), lambda i: (i, 0)),
        out_shape=jax.ShapeDtypeStruct((m, d), F32),
        compiler_params=_cparams(("parallel",)),
        name="mm_res",
    )(x, *a_list, *w_list)


def _project(x, nw_ref, win_ref):
    return jnp.dot(_rms(x, nw_ref[...]).astype(BF16), win_ref[...], preferred_element_type=F32)


def _gdn_prompt_kernel(x_ref, nw_ref, win_ref, cw_ref, alog_ref, dtb_ref, anorm_ref, o_ref, conv_ref, s_out_ref,
                       xbuf, s_scr, *, tblk, nb):
    t = pl.program_id(1)
    hw = A_HEADS * A_DV
    c = CHUNK
    nc = tblk // c

    @pl.when(t == 0)
    def _():
        for bi in range(nb):
            xbuf[bi, 0:8, :] = jnp.zeros((8, A_CONV_CH), F32)
        s_scr[...] = jnp.zeros_like(s_scr)

    cw = cw_ref[...]
    ones_bd = _block_mask(hw, hw, A_DK, A_DK).astype(BF16)
    xs, zs, qs, ks, vs, betas, gs = [], [], [], [], [], [], []
    for bi in range(nb):
        u = _project(x_ref[bi], nw_ref, win_ref)
        x = u[:, 0:A_CONV_CH]
        xbuf[bi, 8:8 + tblk, :] = x
        y = (cw[0:1] * xbuf[bi, 5:5 + tblk, :] + cw[1:2] * xbuf[bi, 6:6 + tblk, :]
             + cw[2:3] * xbuf[bi, 7:7 + tblk, :] + cw[3:4] * x)
        xbuf[bi, 0:8, :] = x[tblk - 8:tblk, :]
        qkv = _silu(y)
        ba = u[:, A_CONV_CH + A_V:UA_W]
        q = qkv[:, 0:A_QK]
        k = qkv[:, A_QK:2 * A_QK]
        xs.append(x)
        zs.append(u[:, A_CONV_CH:A_CONV_CH + A_V])
        qs.append(q * lax.rsqrt(_seg_sum(q * q, ones_bd) + EPS) * (A_DK ** -0.5))
        ks.append(k * lax.rsqrt(_seg_sum(k * k, ones_bd) + EPS))
        vs.append(qkv[:, 2 * A_QK:])
        betas.append(_expand_heads(jax.nn.sigmoid(ba), 0, A_HEADS, A_DV))
        gs.append(_expand_heads(-jnp.exp(alog_ref[...]) * jax.nn.softplus(ba + dtb_ref[...]),
                                A_HEADS, A_HEADS, A_DV))

    bd = _block_mask(hw, hw, c, c)
    row = _iota((c, hw), 0)
    col = _iota((c, hw), 1) % c
    incl = row >= col
    strict = row > col
    items = [(ci, bi) for ci in range(nc) for bi in range(nb)]
    n_it = range(len(items))

    def chunks(arrs):
        return [arrs[bi][ci * c:(ci + 1) * c] for ci, bi in items]

    qc, kc, vc, bc = chunks(qs), chunks(ks), chunks(vs), chunks(betas)
    gcum = [_cumsum_rows(gi) for gi in chunks(gs)]
    grow = [jnp.sum(jnp.where(row == col, gi, 0.0), axis=0, keepdims=True) for gi in gcum]
    decay = [jnp.exp(jnp.where(incl, gcum[n] - grow[n], -jnp.inf)) for n in n_it]
    eg = [jnp.exp(gi) for gi in gcum]
    kb = [kc[n] * bc[n] for n in n_it]
    aq = [_bdot_nt(jnp.concatenate([kb[n], qc[n]], axis=0), _block_diag(kc[n], bd)) for n in n_it]
    a = [jnp.where(strict, aq[n][:c] * decay[n], 0.0) for n in n_it]
    qk = [aq[n][c:] * decay[n] for n in n_it]
    p = [-ai for ai in a]
    pw = [_bdot(ai, _block_diag(ai, bd)) for ai in a]
    n_sq = int(np.log2(c)) - 1
    for r in range(n_sq):
        if r < n_sq - 1:
            both = [_bdot(jnp.concatenate([p[n], pw[n]], axis=0), _block_diag(pw[n], bd)) for n in n_it]
            p = [p[n] + pw[n] + both[n][:c] for n in n_it]
            pw = [both[n][c:] for n in n_it]
        else:
            p = [p[n] + pw[n] + _bdot(p[n], _block_diag(pw[n], bd)) for n in n_it]
    vb = [vc[n] * bc[n] for n in n_it]
    kbg = [kb[n] * eg[n] for n in n_it]
    uw = [_bdot(p[n], jnp.concatenate([_block_diag(vb[n], bd), _block_diag(kbg[n], bd)], axis=1)) for n in n_it]
    u = [vb[n] + uw[n][:, :hw] for n in n_it]
    wq = [jnp.concatenate([kbg[n] + uw[n][:, hw:], qc[n] * eg[n]], axis=0) for n in n_it]
    g_last = [gi[c - 1:c, :] for gi in gcum]
    kd = [kc[n] * jnp.exp(g_last[n] - gcum[n]) for n in n_it]
    eg_last = [jnp.exp(gl) for gl in g_last]

    s = [s_scr[bi] for bi in range(nb)]
    outs = [[] for _ in range(nb)]
    for n, (ci, bi) in enumerate(items):
        ws = _bdot(wq[n], _block_diag(s[bi], bd))
        r = u[n] - ws[:c]
        outs[bi].append(ws[c:] + _bdot(qk[n], _block_diag(r, bd)))
        s[bi] = eg_last[n] * s[bi] + _diag_blocks(_bdot_tn(kd[n], r), A_DK, A_DV)

    for bi in range(nb):
        s_scr[bi] = s[bi]
        o = jnp.concatenate(outs[bi], axis=0)
        o = o * lax.rsqrt(_seg_sum(o * o, ones_bd) * (1.0 / A_DV) + EPS) * anorm_ref[...]
        o_ref[bi] = o * _silu(zs[bi])

    @pl.when(t == pl.num_programs(1) - 1)
    def _():
        for bi in range(nb):
            conv_ref[bi] = xs[bi][tblk - (CONV_W - 1):tblk, :]
            s_out_ref[bi] = s[bi]


def _gdn_prompt(x, nw, w_a, cw, alog, dtb, anorm, *, tblk, nb):
    b, t, d = x.shape
    hw = A_HEADS * A_DV
    return pl.pallas_call(
        functools.partial(_gdn_prompt_kernel, tblk=tblk, nb=nb),
        grid=(b // nb, t // tblk),
        in_specs=[pl.BlockSpec((nb, tblk, d), lambda i, j: (i, j, 0)),
                  _const_spec((1, d)), _const_spec((d, UA_W)),
                  _const_spec((CONV_W, A_CONV_CH)), _const_spec((1, LANE)), _const_spec((1, LANE)),
                  _const_spec((1, hw))],
        out_specs=[pl.BlockSpec((nb, tblk, hw), lambda i, j: (i, j, 0)),
                   pl.BlockSpec((nb, CONV_W - 1, A_CONV_CH), lambda i, j: (i, 0, 0)),
                   pl.BlockSpec((nb, A_DK, hw), lambda i, j: (i, 0, 0))],
        out_shape=[jax.ShapeDtypeStruct((b, t, hw), F32),
                   jax.ShapeDtypeStruct((b, CONV_W - 1, A_CONV_CH), F32),
                   jax.ShapeDtypeStruct((b, A_DK, hw), F32)],
        scratch_shapes=[pltpu.VMEM((nb, tblk + 8, A_CONV_CH), F32), pltpu.VMEM((nb, A_DK, hw), F32)],
        compiler_params=_cparams(("parallel", "arbitrary")),
        name="gdn_prompt",
    )(x, nw, w_a, cw, alog, dtb, anorm)


def _gla_prompt_kernel(x_ref, nw_ref, win_ref, w2_ref, gb_ref, bnorm_ref, o_ref, s_out_ref, s_scr, *, tblk):
    t = pl.program_id(1)
    vw = B_HEADS * B_DV

    @pl.when(t == 0)
    def _():
        s_scr[...] = jnp.zeros_like(s_scr)

    u = _project(x_ref[0], nw_ref, win_ref)
    q = u[:, 0:B_QK] * (B_DK ** -0.5)
    k = u[:, B_QK:2 * B_QK]
    v = u[:, 2 * B_QK:2 * B_QK + B_V]
    rg = u[:, 2 * B_QK + B_V:2 * B_QK + 2 * B_V]
    glr = u[:, 2 * B_QK + 2 * B_V:]
    log_a = jax.nn.log_sigmoid(_bdot(glr, w2_ref[...]) + gb_ref[...]) / B_GATE_TAU

    c = CHUNK
    kmask = _block_mask(vw, B_QK, c, B_DK)
    vmask = _block_mask(vw, vw, c, B_DV)
    smask = _block_mask(vw, B_QK, B_DV, B_DK)
    row_k = _iota((c, B_QK), 0)
    row_s = _iota((SUB, vw), 0)
    col_s = _iota((SUB, vw), 1) % c
    st = s_scr[...]
    outs = []
    for ci in range(tblk // c):
        sl = slice(ci * c, (ci + 1) * c)
        qc, kc, vc = q[sl], k[sl], v[sl]
        b = _cumsum_rows(log_a[sl])
        o = _bdot_nt(qc * jnp.exp(b), st)
        att = []
        for i in range(c // SUB):
            r0 = i * SUB
            bref = b[r0:r0 + 1, :]
            qs = qc[r0:r0 + SUB] * jnp.exp(b[r0:r0 + SUB] - bref)
            ks = kc * jnp.exp(jnp.where(row_k < r0 + SUB, bref - b, -jnp.inf))
            a_i = _bdot_nt(qs, _block_diag(ks, kmask))
            att.append(jnp.where(row_s + r0 >= col_s, a_i, 0.0))
        o = o + _bdot(jnp.concatenate(att, axis=0), _block_diag(vc, vmask))
        outs.append(o)
        b_last = b[c - 1:c, :]
        kd = kc * jnp.exp(b_last - b)
        st = jnp.exp(b_last) * st + jnp.where(smask, _bdot_tn(vc, kd), 0.0)
    s_scr[...] = st

    o = jnp.concatenate(outs, axis=0)
    ones_bd = _block_mask(vw, vw, B_DV, B_DV).astype(BF16)
    o = o * lax.rsqrt(_seg_sum(o * o, ones_bd) * (1.0 / B_DV) + EPS) * bnorm_ref[...]
    o_ref[0] = o * _silu(rg)

    @pl.when(t == pl.num_programs(1) - 1)
    def _():
        s_out_ref[0] = st


def _gla_prompt(x, nw, w_b, w2, gb, bnorm, *, tblk):
    b, t, d = x.shape
    vw = B_HEADS * B_DV
    return pl.pallas_call(
        functools.partial(_gla_prompt_kernel, tblk=tblk),
        grid=(b, t // tblk),
        in_specs=[pl.BlockSpec((1, tblk, d), lambda i, j: (i, j, 0)),
                  _const_spec((1, d)), _const_spec((d, UB_W)),
                  _const_spec((LANE, B_QK)), _const_spec((1, B_QK)), _const_spec((1, vw))],
        out_specs=[pl.BlockSpec((1, tblk, vw), lambda i, j: (i, j, 0)),
                   pl.BlockSpec((1, vw, B_QK), lambda i, j: (i, 0, 0))],
        out_shape=[jax.ShapeDtypeStruct((b, t, vw), F32),
                   jax.ShapeDtypeStruct((b, vw, B_QK), F32)],
        scratch_shapes=[pltpu.VMEM((vw, B_QK), F32)],
        compiler_params=_cparams(("parallel", "arbitrary")),
        name="gla_prompt",
    )(x, nw, w_b, w2, gb, bnorm)


def _mla_prep_kernel(x_ref, nw_ref, win_ref, cqn_ref, ckvn_ref, wuq_ref, wuk_ref, ck_ref, sk_ref, cq_ref, sq_ref,
                     ckv_o, kpe_o, ckvb_o, kpeb_o, qlat_o, qpe_o):
    u = _project(x_ref[0], nw_ref, win_ref)
    cq = _rms(u[:, 0:C_Q_LORA], cqn_ref[...]).astype(BF16)
    ckv = _rms(u[:, C_Q_LORA:C_Q_LORA + C_KV_LORA], ckvn_ref[...])
    half = C_ROPE // 2
    kx = u[:, C_Q_LORA + C_KV_LORA:]
    lane = _iota(kx.shape, 1)
    kswap = jnp.where(lane < half, pltpu.roll(kx, LANE - half, axis=1), pltpu.roll(kx, half, axis=1))
    kpe = (kx * ck_ref[...] + kswap * sk_ref[...])[:, 0:C_ROPE]
    ckv_o[0] = ckv
    kpe_o[0] = kpe
    ckvb_o[0] = ckv.astype(BF16)
    kpeb_o[0] = kpe.astype(BF16)

    qf = jnp.dot(cq, wuq_ref[...], preferred_element_type=F32)
    nope_w = C_HEADS * C_NOPE
    rope_w = C_HEADS * C_ROPE
    qr = qf[:, nope_w:]
    lane_r = _iota(qr.shape, 1) % C_ROPE
    qswap = jnp.where(lane_r < half, pltpu.roll(qr, rope_w - half, axis=1), pltpu.roll(qr, half, axis=1))
    qpe = qr * cq_ref[...] + qswap * sq_ref[...]
    for h in range(C_HEADS):
        qlat_o[0, h] = jnp.dot(qf[:, h * C_NOPE:(h + 1) * C_NOPE].astype(BF16), wuk_ref[h],
                               preferred_element_type=F32).astype(BF16)
        qpe_o[0, h] = qpe[:, h * C_ROPE:(h + 1) * C_ROPE].astype(BF16)


def _mla_prep(x, nw, w_c, cqn, ckvn, wuq, wuk, tabs, *, tm):
    b, t, d = x.shape
    ck, sk, cq, sq = tabs
    per_pos = ck.shape[0] != 1
    rope_w = C_HEADS * C_ROPE

    def tab_spec(w):
        if per_pos:
            return pl.BlockSpec((tm, w), lambda i, j: (j, 0))
        return _const_spec((1, w))

    def tok_spec(w):
        return pl.BlockSpec((1, tm, w), lambda i, j: (i, j, 0))

    def head_spec(w):
        return pl.BlockSpec((1, C_HEADS, tm, w), lambda i, j: (i, 0, j, 0))

    return pl.pallas_call(
        _mla_prep_kernel,
        grid=(b, t // tm),
        in_specs=[tok_spec(d), _const_spec((1, d)), _const_spec((d, UC_W)),
                  _const_spec((1, C_Q_LORA)), _const_spec((1, C_KV_LORA)),
                  _const_spec(wuq.shape), _const_spec(wuk.shape),
                  tab_spec(LANE), tab_spec(LANE), tab_spec(rope_w), tab_spec(rope_w)],
        out_specs=[tok_spec(C_KV_LORA), tok_spec(C_ROPE), tok_spec(C_KV_LORA), tok_spec(C_ROPE),
                   head_spec(C_KV_LORA), head_spec(C_ROPE)],
        out_shape=[jax.ShapeDtypeStruct((b, t, C_KV_LORA), F32),
                   jax.ShapeDtypeStruct((b, t, C_ROPE), F32),
                   jax.ShapeDtypeStruct((b, t, C_KV_LORA), BF16),
                   jax.ShapeDtypeStruct((b, t, C_ROPE), BF16),
                   jax.ShapeDtypeStruct((b, C_HEADS, t, C_KV_LORA), BF16),
                   jax.ShapeDtypeStruct((b, C_HEADS, t, C_ROPE), BF16)],
        compiler_params=_cparams(("parallel", "parallel")),
        name="mla_prep",
    )(x, nw, w_c, cqn, ckvn, wuq, wuk, ck, sk, cq, sq)


def _mla_flash_kernel(qi_ref, kj_ref, ql_ref, qp_ref, k_ref, p_ref, wuv_ref, o_ref,
                      m_scr, l_scr, acc_scr, *, tq, tk):
    g = pl.program_id(1)
    i = qi_ref[g]
    j = kj_ref[g]
    last_j = (i * tq + tq - 1) // tk
    c2 = MLA_SCALE * LOG2E

    @pl.when(j == 0)
    def _():
        m_scr[...] = jnp.full_like(m_scr, -jnp.inf)
        l_scr[...] = jnp.zeros_like(l_scr)
        acc_scr[...] = jnp.zeros_like(acc_scr)

    def step(masked):
        kv = k_ref[0]
        pe = p_ref[0]
        if masked:
            visible = j * tk + _iota((tq, tk), 1) <= i * tq + _iota((tq, tk), 0)
        for h in range(C_HEADS):
            s = (lax.dot_general(ql_ref[0, h], kv, (((1,), (1,)), ((), ())), preferred_element_type=F32)
                 + lax.dot_general(qp_ref[0, h], pe, (((1,), (1,)), ((), ())), preferred_element_type=F32))
            if masked:
                s = jnp.where(visible, s, -jnp.inf)
            m_old = m_scr[h]
            m_new = jnp.maximum(m_old, jnp.max(s, axis=-1, keepdims=True))
            alpha = jnp.exp2((m_old - m_new) * c2)
            p = jnp.exp2(s * c2 - m_new * c2)
            l_scr[h] = alpha * l_scr[h] + jnp.sum(p, axis=-1, keepdims=True)
            m_scr[h] = m_new
            acc_scr[h] = alpha * acc_scr[h] + jnp.dot(p.astype(BF16), kv, preferred_element_type=F32)

    has_masked = j * tk + tk - 1 > i * tq

    @pl.when(has_masked)
    def _():
        step(True)

    @pl.when(jnp.logical_not(has_masked))
    def _():
        step(False)

    @pl.when(j == last_j)
    def _():
        o_ref[0] = jnp.concatenate(
            [jnp.dot((acc_scr[h] / l_scr[h]).astype(BF16), wuv_ref[h], preferred_element_type=F32)
             for h in range(C_HEADS)], axis=-1)


def _mla_flash(qlat, qpe, ckvb, kpeb, wuv, *, tq, tk):
    b, _, t, _ = qlat.shape
    pairs = [(i, j) for i in range(t // tq) for j in range((i * tq + tq - 1) // tk + 1)]
    qi = jnp.asarray([p[0] for p in pairs], jnp.int32)
    kj = jnp.asarray([p[1] for p in pairs], jnp.int32)
    grid_spec = pltpu.PrefetchScalarGridSpec(
        num_scalar_prefetch=2,
        grid=(b, len(pairs)),
        in_specs=[pl.BlockSpec((1, C_HEADS, tq, C_KV_LORA), lambda bi, g, qi, kj: (bi, 0, qi[g], 0)),
                  pl.BlockSpec((1, C_HEADS, tq, C_ROPE), lambda bi, g, qi, kj: (bi, 0, qi[g], 0)),
                  pl.BlockSpec((1, tk, C_KV_LORA), lambda bi, g, qi, kj: (bi, kj[g], 0)),
                  pl.BlockSpec((1, tk, C_ROPE), lambda bi, g, qi, kj: (bi, kj[g], 0)),
                  _const_spec(wuv.shape)],
        out_specs=pl.BlockSpec((1, tq, C_HEADS * C_VDIM), lambda bi, g, qi, kj: (bi, qi[g], 0)),
        scratch_shapes=[pltpu.VMEM((C_HEADS, tq, 1), F32), pltpu.VMEM((C_HEADS, tq, 1), F32),
                        pltpu.VMEM((C_HEADS, tq, C_KV_LORA), F32)],
    )
    return pl.pallas_call(
        functools.partial(_mla_flash_kernel, tq=tq, tk=tk),
        grid_spec=grid_spec,
        out_shape=jax.ShapeDtypeStruct((b, t, C_HEADS * C_VDIM), F32),
        compiler_params=_cparams(("parallel", "arbitrary")),
        name="mla_flash",
    )(qi, kj, qlat, qpe, ckvb, kpeb, wuv)


def _mla_oproj_kernel(ol_ref, wuv_ref, o_ref):
    o_ref[0] = jnp.concatenate(
        [jnp.dot(ol_ref[0, h].astype(BF16), wuv_ref[h], preferred_element_type=F32)
         for h in range(C_HEADS)], axis=-1)


def _mla_oproj(olat, wuv, *, tm):
    b, _, t, _ = olat.shape
    return pl.pallas_call(
        _mla_oproj_kernel,
        grid=(b, t // tm),
        in_specs=[pl.BlockSpec((1, C_HEADS, tm, C_KV_LORA), lambda i, j: (i, 0, j, 0)),
                  _const_spec(wuv.shape)],
        out_specs=pl.BlockSpec((1, tm, C_HEADS * C_VDIM), lambda i, j: (i, j, 0)),
        out_shape=jax.ShapeDtypeStruct((b, t, C_HEADS * C_VDIM), F32),
        compiler_params=_cparams(("parallel", "parallel")),
        name="mla_oproj",
    )(olat, wuv)


def _mla_paged_kernel(pt_ref, ql_ref, qp_ref, cn_ref, pn_ref, ckv_hbm, kpe_hbm, o_ref,
                      kbuf, pbuf, sem, m_scr, l_scr, acc_scr, *, layer, ppc, nch, page):
    b = pl.program_id(0)
    c = pl.program_id(1)
    g = b * nch + c
    total = pl.num_programs(0) * nch
    slot = g % 2

    def copies(bb, cc, sl, p):
        pid = pt_ref[bb, cc * ppc + p]
        dst = pl.ds(pl.multiple_of(p * page, page), page)
        return (pltpu.make_async_copy(ckv_hbm.at[layer, pid], kbuf.at[sl, dst, :], sem.at[0, sl]),
                pltpu.make_async_copy(kpe_hbm.at[layer, pid], pbuf.at[sl, :, dst], sem.at[1, sl]))

    def issue(bb, cc, sl):
        def body(p, carry):
            for cp in copies(bb, cc, sl, p):
                cp.start()
            return carry
        lax.fori_loop(0, ppc, body, 0)

    @pl.when(g == 0)
    def _():
        issue(0, 0, 0)

    @pl.when(g + 1 < total)
    def _():
        issue((g + 1) // nch, (g + 1) % nch, 1 - slot)

    def wait_body(p, carry):
        for cp in copies(b, c, slot, p):
            cp.wait()
        return carry
    lax.fori_loop(0, ppc, wait_body, 0)

    @pl.when(c == 0)
    def _():
        m_scr[...] = jnp.full_like(m_scr, -jnp.inf)
        l_scr[...] = jnp.zeros_like(l_scr)
        acc_scr[...] = jnp.zeros_like(acc_scr)

    ql = ql_ref[0]
    qp = qp_ref[0]
    kv = kbuf[slot].astype(BF16)
    pe = pbuf[slot].astype(BF16)
    s = (lax.dot_general(ql, kv, (((1,), (1,)), ((), ())), preferred_element_type=F32)
         + jnp.dot(qp, pe, preferred_element_type=F32)) * MLA_SCALE
    m_old = m_scr[...]
    m_new = jnp.maximum(m_old, jnp.max(s, axis=-1, keepdims=True))
    alpha = jnp.exp(m_old - m_new)
    p = jnp.exp(s - m_new)
    l_new = alpha * l_scr[...] + jnp.sum(p, axis=-1, keepdims=True)
    acc_new = alpha * acc_scr[...] + jnp.dot(p.astype(BF16), kv, preferred_element_type=F32)
    m_scr[...] = m_new
    l_scr[...] = l_new
    acc_scr[...] = acc_new

    @pl.when(c == nch - 1)
    def _():
        cn = cn_ref[0].astype(BF16).astype(F32)
        pn = pn_ref[0].astype(BF16).astype(F32)
        s_n = (jnp.sum(ql.astype(F32) * cn, axis=-1, keepdims=True)
               + jnp.sum(qp.astype(F32) * pn, axis=-1, keepdims=True)) * MLA_SCALE
        m_f = jnp.maximum(m_new, s_n)
        a_f = jnp.exp(m_new - m_f)
        p_n = jnp.exp(s_n - m_f)
        l_f = a_f * l_new + p_n
        acc_f = a_f * acc_new + p_n.astype(BF16).astype(F32) * cn
        o_ref[0] = acc_f / l_f


def _mla_paged(page_table, qlat, qpe, ckv_new, kpe_new, cache_ckv, cache_kpe, *, layer, ppc):
    b, hp, _ = qlat.shape
    n_pages = page_table.shape[1]
    page = cache_ckv.shape[2]
    nch = n_pages // ppc
    grid_spec = pltpu.PrefetchScalarGridSpec(
        num_scalar_prefetch=1,
        grid=(b, nch),
        in_specs=[pl.BlockSpec((1, hp, C_KV_LORA), lambda i, j, pt: (i, 0, 0)),
                  pl.BlockSpec((1, hp, C_ROPE), lambda i, j, pt: (i, 0, 0)),
                  pl.BlockSpec((1, 1, C_KV_LORA), lambda i, j, pt: (i, 0, 0)),
                  pl.BlockSpec((1, 1, C_ROPE), lambda i, j, pt: (i, 0, 0)),
                  pl.BlockSpec(memory_space=pl.ANY),
                  pl.BlockSpec(memory_space=pl.ANY)],
        out_specs=pl.BlockSpec((1, hp, C_KV_LORA), lambda i, j, pt: (i, 0, 0)),
        scratch_shapes=[pltpu.VMEM((2, ppc * page, C_KV_LORA), F32),
                        pltpu.VMEM((2, C_ROPE, ppc * page), F32),
                        pltpu.SemaphoreType.DMA((2, 2)),
                        pltpu.VMEM((hp, 1), F32), pltpu.VMEM((hp, 1), F32),
                        pltpu.VMEM((hp, C_KV_LORA), F32)],
    )
    return pl.pallas_call(
        functools.partial(_mla_paged_kernel, layer=layer, ppc=ppc, nch=nch, page=page),
        grid_spec=grid_spec,
        out_shape=jax.ShapeDtypeStruct((b, hp, C_KV_LORA), F32),
        compiler_params=_cparams(("arbitrary", "arbitrary")),
        name="mla_paged",
    )(page_table, qlat, qpe, ckv_new, kpe_new, cache_ckv, cache_kpe)


def _xattn_prompt_kernel(x_ref, oa_ref, ob_ref, oc_ref, wa_ref, wb_ref, wc_ref,
                         nw_ref, wq_ref, wo_ref, mk_ref, mv_ref, o_ref):
    x = x_ref[0]
    for a_ref, w_ref in ((oa_ref, wa_ref), (ob_ref, wb_ref), (oc_ref, wc_ref)):
        x = x + jnp.dot(a_ref[0].astype(BF16), w_ref[...], preferred_element_type=F32)
    xn = _rms(x, nw_ref[...]).astype(BF16)
    q = jnp.dot(xn, wq_ref[...], preferred_element_type=F32).astype(BF16)
    heads = []
    for h in range(X_HEADS):
        sl = slice(h * X_HDIM, (h + 1) * X_HDIM)
        s = lax.dot_general(q[:, sl], mk_ref[0, :, sl], (((1,), (1,)), ((), ())),
                            preferred_element_type=F32) * (X_HDIM ** -0.5)
        e = jnp.exp(s - jnp.max(s, axis=-1, keepdims=True))
        p = (e / jnp.sum(e, axis=-1, keepdims=True)).astype(BF16)
        heads.append(jnp.dot(p, mv_ref[0, :, sl], preferred_element_type=F32).astype(BF16))
    o_ref[0] = x + jnp.dot(jnp.concatenate(heads, axis=-1), wo_ref[...], preferred_element_type=F32)


def _xattn_prompt(x, o_list, w_out_list, nw, wq, wo, mk, mv, *, tm):
    b, t, d = x.shape
    n_mem = mk.shape[1]
    return pl.pallas_call(
        _xattn_prompt_kernel,
        grid=(b, t // tm),
        in_specs=[pl.BlockSpec((1, tm, d), lambda i, j: (i, j, 0))]
                 + [pl.BlockSpec((1, tm, o.shape[2]), lambda i, j: (i, j, 0)) for o in o_list]
                 + [_const_spec(w.shape) for w in w_out_list]
                 + [_const_spec((1, d)), _const_spec(wq.shape), _const_spec(wo.shape),
                  pl.BlockSpec((1, n_mem, d), lambda i, j: (i, 0, 0)),
                  pl.BlockSpec((1, n_mem, d), lambda i, j: (i, 0, 0))],
        out_specs=pl.BlockSpec((1, tm, d), lambda i, j: (i, j, 0)),
        out_shape=jax.ShapeDtypeStruct((b, t, d), F32),
        compiler_params=_cparams(("parallel", "parallel")),
        name="xattn_prompt",
    )(x, *o_list, *w_out_list, nw, wq, wo, mk, mv)


def _xattn_decode_kernel(q_ref, mk_ref, mv_ref, o_ref):
    for bi in range(q_ref.shape[0]):
        _xattn_decode_one(q_ref, mk_ref, mv_ref, o_ref, bi)


def _xattn_decode_one(q_ref, mk_ref, mv_ref, o_ref, bi):
    nt = X_HDIM // LANE
    grp = nt * X_HEADS
    n_rows = mk_ref.shape[2]
    qrow = q_ref[bi]
    qm = jnp.concatenate([qrow[:, h * X_HDIM + t * LANE:h * X_HDIM + (t + 1) * LANE]
                          for t in range(nt) for h in range(X_HEADS)], axis=0)
    sel = _iota((grp, n_rows), 0) == _iota((grp, n_rows), 1) % grp
    s_all = _bdot_nt(qm, mk_ref[0, bi])
    part = jnp.sum(jnp.where(sel, s_all, 0.0), axis=0, keepdims=True)
    lane = _iota((1, n_rows), 1)
    s = part
    for t in range(1, nt):
        s = s + jnp.where(lane % grp < X_HEADS, pltpu.roll(part, n_rows - t * X_HEADS, axis=1),
                          pltpu.roll(part, t * X_HEADS, axis=1))
    s = s * (X_HDIM ** -0.5)
    p = jnp.zeros_like(s)
    for h in range(X_HEADS):
        mine = lane % X_HEADS == h
        e = jnp.exp(s - jnp.max(jnp.where(mine, s, -jnp.inf), axis=-1, keepdims=True))
        den = jnp.sum(jnp.where(lane % grp == h, e, 0.0), axis=-1, keepdims=True)
        p = jnp.where(mine, e / den, p)
    o = _bdot(jnp.where(sel, p, 0.0), mv_ref[0, bi])
    o_ref[bi] = jnp.concatenate([o[t * X_HEADS + h:t * X_HEADS + h + 1]
                                for h in range(X_HEADS) for t in range(nt)], axis=-1)


def _mem_rows(mem):
    dep, b, n_mem, nh, hd = mem.shape
    nt = hd // LANE
    return mem.reshape(dep, b, n_mem, nh, nt, LANE).transpose(0, 1, 2, 4, 3, 5).reshape(dep, b, n_mem * nt * nh, LANE)


def _xattn_decode(q, mk_rows, mv_rows, *, layer, nb):
    _, b, n_rows, _ = mk_rows.shape
    d = X_HEADS * X_HDIM
    assert X_HDIM == 2 * LANE
    mem_spec = pl.BlockSpec((1, nb, n_rows, LANE), lambda i: (layer, i, 0, 0))
    return pl.pallas_call(
        _xattn_decode_kernel,
        grid=(b // nb,),
        in_specs=[pl.BlockSpec((nb, 1, d), lambda i: (i, 0, 0)), mem_spec, mem_spec],
        out_specs=pl.BlockSpec((nb, 1, d), lambda i: (i, 0, 0)),
        out_shape=jax.ShapeDtypeStruct((b, 1, d), F32),
        compiler_params=_cparams(("parallel",)),
        name="xattn_decode",
    )(q, mk_rows, mv_rows)


def _columns(rows):
    w = rows[0].shape[1]
    mat = jnp.concatenate(rows + [jnp.zeros((LANE - len(rows), w), F32)], axis=0)
    return mat.T


def _gdn_decode_kernel(u_ref, cs_ref, s_ref, cw_ref, alog_ref, dtb_ref, anorm_ref,
                       o_ref, cs_out_ref, s_out_ref):
    for bi in range(u_ref.shape[0]):
        _gdn_decode_one(u_ref, cs_ref, s_ref, cw_ref, alog_ref, dtb_ref, anorm_ref,
                        o_ref, cs_out_ref, s_out_ref, bi)


def _gdn_decode_one(u_ref, cs_ref, s_ref, cw_ref, alog_ref, dtb_ref, anorm_ref,
                    o_ref, cs_out_ref, s_out_ref, bi):
    x = u_ref[bi, :, 0:A_CONV_CH]
    z = u_ref[bi, :, A_CONV_CH:A_CONV_CH + A_V]
    ba = u_ref[bi, :, A_CONV_CH + A_V:UA_W]
    cs = cs_ref[0, bi]
    cw = cw_ref[...]
    y = cw[0:1] * cs[0:1] + cw[1:2] * cs[1:2] + cw[2:3] * cs[2:3] + cw[3:4] * x
    cs_out_ref[bi] = jnp.concatenate([cs[1:CONV_W - 1], x], axis=0)
    qkv = _silu(y)
    beta = jax.nn.sigmoid(ba)
    g = -jnp.exp(alog_ref[...]) * jax.nn.softplus(ba + dtb_ref[...])
    qs, ks, vs = [], [], []
    for h in range(A_HEADS):
        qh = qkv[:, h * A_DK:(h + 1) * A_DK]
        kh = qkv[:, A_QK + h * A_DK:A_QK + (h + 1) * A_DK]
        qs.append(qh * lax.rsqrt(jnp.sum(qh * qh, axis=-1, keepdims=True) + EPS) * (A_DK ** -0.5))
        ks.append(kh * lax.rsqrt(jnp.sum(kh * kh, axis=-1, keepdims=True) + EPS))
        vs.append(qkv[:, 2 * A_QK + h * A_DV:2 * A_QK + (h + 1) * A_DV])
    cols = _columns(ks + qs)
    outs = []
    for h in range(A_HEADS):
        s = s_ref[0, bi, h]
        kcol = cols[:, h:h + 1]
        qcol = cols[:, A_HEADS + h:A_HEADS + h + 1]
        bh = beta[:, h:h + 1]
        eg = jnp.exp(g[:, A_HEADS + h:A_HEADS + h + 1])
        ks_row = jnp.sum(kcol * s, axis=0, keepdims=True)
        qs_row = jnp.sum(qcol * s, axis=0, keepdims=True)
        r = vs[h] * bh - (bh * eg) * ks_row
        qk = jnp.sum(qs[h] * ks[h], axis=-1, keepdims=True)
        o = eg * qs_row + qk * r
        s_out_ref[bi, h] = eg * s + kcol * r
        o = o * lax.rsqrt(jnp.mean(o * o, axis=-1, keepdims=True) + EPS) * anorm_ref[...]
        outs.append(o)
    o_ref[bi] = jnp.concatenate(outs, axis=-1) * _silu(z)


def _gdn_decode(u_a, cs, s, cw, alog, dtb, anorm, *, layer, nb):
    b = u_a.shape[0]
    return pl.pallas_call(
        _gdn_decode_kernel,
        grid=(b // nb,),
        in_specs=[pl.BlockSpec((nb, 1, UA_W), lambda i: (i, 0, 0)),
                  pl.BlockSpec((1, nb, CONV_W - 1, A_CONV_CH), lambda i: (layer, i, 0, 0)),
                  pl.BlockSpec((1, nb, A_HEADS, A_DK, A_DV), lambda i: (layer, i, 0, 0, 0)),
                  _const_spec((CONV_W, A_CONV_CH)), _const_spec((1, LANE)), _const_spec((1, LANE)),
                  _const_spec((1, A_DV))],
        out_specs=[pl.BlockSpec((nb, 1, A_V), lambda i: (i, 0, 0)),
                   pl.BlockSpec((nb, CONV_W - 1, A_CONV_CH), lambda i: (i, 0, 0)),
                   pl.BlockSpec((nb, A_HEADS, A_DK, A_DV), lambda i: (i, 0, 0, 0))],
        out_shape=[jax.ShapeDtypeStruct((b, 1, A_V), F32),
                   jax.ShapeDtypeStruct((b, CONV_W - 1, A_CONV_CH), F32),
                   jax.ShapeDtypeStruct((b, A_HEADS, A_DK, A_DV), F32)],
        compiler_params=_cparams(("parallel",)),
        name="gdn_decode",
    )(u_a, cs, s, cw, alog, dtb, anorm)


def _gla_decode_kernel(u_ref, s_ref, w2_ref, gb_ref, bnorm_ref, o_ref, s_out_ref):
    for bi in range(u_ref.shape[0]):
        _gla_decode_one(u_ref, s_ref, w2_ref, gb_ref, bnorm_ref, o_ref, s_out_ref, bi)


def _gla_decode_one(u_ref, s_ref, w2_ref, gb_ref, bnorm_ref, o_ref, s_out_ref, bi):
    u = u_ref[bi]
    q = u[:, 0:B_QK] * (B_DK ** -0.5)
    k = u[:, B_QK:2 * B_QK]
    rg = u[:, 2 * B_QK + B_V:2 * B_QK + 2 * B_V]
    glr = jnp.broadcast_to(u[:, 2 * B_QK + 2 * B_V:], (8, LANE))
    log_a = jax.nn.log_sigmoid(_bdot(glr, w2_ref[...])[0:1] + gb_ref[...]) / B_GATE_TAU
    a = jnp.exp(log_a)
    cols = _columns([k, q * a, a])
    outs = []
    for h in range(B_HEADS):
        rs = slice(h * B_DK, (h + 1) * B_DK)
        s = s_ref[0, bi, h]
        vh = u[:, 2 * B_QK + h * B_DV:2 * B_QK + (h + 1) * B_DV]
        qk = jnp.sum(q[:, rs] * k[:, rs], axis=-1, keepdims=True)
        o = jnp.sum(cols[rs, 1:2] * s, axis=0, keepdims=True) + qk * vh
        s_out_ref[bi, h] = cols[rs, 2:3] * s + cols[rs, 0:1] * vh
        o = o * lax.rsqrt(jnp.mean(o * o, axis=-1, keepdims=True) + EPS) * bnorm_ref[...]
        outs.append(o)
    o_ref[bi] = jnp.concatenate(outs, axis=-1) * _silu(rg)


def _gla_decode(u_b, s, w2, gb, bnorm, *, layer, nb):
    b = u_b.shape[0]
    return pl.pallas_call(
        _gla_decode_kernel,
        grid=(b // nb,),
        in_specs=[pl.BlockSpec((nb, 1, UB_W), lambda i: (i, 0, 0)),
                  pl.BlockSpec((1, nb, B_HEADS, B_DK, B_DV), lambda i: (layer, i, 0, 0, 0)),
                  _const_spec((LANE, B_QK)), _const_spec((1, B_QK)), _const_spec((1, B_DV))],
        out_specs=[pl.BlockSpec((nb, 1, B_V), lambda i: (i, 0, 0)),
                   pl.BlockSpec((nb, B_HEADS, B_DK, B_DV), lambda i: (i, 0, 0, 0))],
        out_shape=[jax.ShapeDtypeStruct((b, 1, B_V), F32),
                   jax.ShapeDtypeStruct((b, B_HEADS, B_DK, B_DV), F32)],
        compiler_params=_cparams(("parallel",)),
        name="gla_decode",
    )(u_b, s, w2, gb, bnorm)


def _pad_cols(w, n):
    return jnp.pad(w, ((0, 0), (0, n - w.shape[1])))


def _rope_tables(pos):
    half = C_ROPE // 2
    inv = ROPE_THETA ** (-jnp.arange(half, dtype=F32) / half)
    ang = pos.astype(F32)[:, None] * inv[None, :]
    cos, sin = jnp.cos(ang), jnp.sin(ang)
    zero = jnp.zeros_like(cos)
    ck = jnp.concatenate([cos, cos, zero, zero], axis=-1)
    sk = jnp.concatenate([-sin, sin, zero, zero], axis=-1)
    cq = jnp.tile(jnp.concatenate([cos, cos], axis=-1), (1, C_HEADS))
    sq = jnp.tile(jnp.concatenate([-sin, sin], axis=-1), (1, C_HEADS))
    return ck, sk, cq, sq


def _layer_weights(l, P):
    w_in = P['w_in'][l]
    o_b, o_c = A_IN, A_IN + B_IN
    w_a = jnp.concatenate([w_in[:, :A_CONV_CH + A_V], _pad_cols(w_in[:, A_CONV_CH + A_V:A_IN], LANE)], axis=1)
    w_b = jnp.concatenate([w_in[:, o_b:o_b + 2 * B_QK + 2 * B_V],
                           _pad_cols(w_in[:, o_b + 2 * B_QK + 2 * B_V:o_c], LANE)], axis=1)
    w_c = jnp.concatenate([w_in[:, o_c:o_c + C_Q_LORA + C_KV_LORA],
                           _pad_cols(w_in[:, o_c + C_Q_LORA + C_KV_LORA:], LANE)], axis=1)
    wuq = P['c_w_uq'][l].reshape(C_Q_LORA, C_HEADS, C_NOPE + C_ROPE)
    wuq = jnp.concatenate([wuq[:, :, :C_NOPE].reshape(C_Q_LORA, -1),
                           wuq[:, :, C_NOPE:].reshape(C_Q_LORA, -1)], axis=1)
    w_out = P['w_out'][l]
    row = lambda a: a.reshape(1, -1).astype(F32)
    head_lanes = lambda a: jnp.pad(a.reshape(1, -1).astype(F32), ((0, 0), (A_HEADS, LANE - 2 * A_HEADS)))
    return dict(
        norm_ffn1=row(P['norm_ffn1'][l]), norm_ffn2=row(P['norm_ffn2'][l]),
        ffn1=tuple(P[n][l].astype(BF16) for n in ('w_ffn1_gate', 'w_ffn1_up', 'w_ffn1_down')),
        ffn2=tuple(P[n][l].astype(BF16) for n in ('w_ffn2_gate', 'w_ffn2_up', 'w_ffn2_down')),
        norm_mix=row(P['norm_mix'][l]),
        w_in=(w_a.astype(BF16), w_b.astype(BF16), w_c.astype(BF16)),
        a_conv_w=P['a_conv_w'][l].astype(F32),
        a_log=head_lanes(P['a_log'][l]), a_dt_bias=head_lanes(P['a_dt_bias'][l]),
        a_norm=row(P['a_norm'][l]), a_norm_t=row(jnp.tile(P['a_norm'][l], A_HEADS)),
        b_w2=jnp.pad(P['b_gate_w2'][l], ((0, LANE - B_GATE_RANK), (0, 0))).astype(BF16),
        b_gate_bias=row(P['b_gate_bias'][l]),
        b_norm=row(P['b_norm'][l]), b_norm_t=row(jnp.tile(P['b_norm'][l], B_HEADS)),
        c_q_norm=row(P['c_q_norm'][l]), c_kv_norm=row(P['c_kv_norm'][l]),
        c_w_uq=wuq.astype(BF16),
        c_w_uk=jnp.transpose(P['c_w_uk'][l], (1, 2, 0)).astype(BF16),
        c_w_uv=jnp.transpose(P['c_w_uv'][l], (1, 0, 2)).astype(BF16),
        w_out=(w_out[:A_V].astype(BF16), w_out[A_V:A_V + B_V].astype(BF16), w_out[A_V + B_V:].astype(BF16)),
        norm_x=row(P['norm_x'][l]), norm_mem=row(P['norm_mem'][l]),
        w_xq=P['w_xq'][l].astype(BF16), w_xo=P['w_xo'][l].astype(BF16),
        w_xk=P['w_xk'][l].astype(BF16), w_xv=P['w_xv'][l].astype(BF16),
    )


def _pick(n, pref):
    return pref if n % pref == 0 else n


def kernel(x_prompt, x_sample, mem_prompt, cache_ckv, cache_kpe, page_table, state_conv_a, state_delta, state_gla, cache_mem_k, cache_mem_v, norm_ffn1, w_ffn1_gate, w_ffn1_up, w_ffn1_down, norm_mix, w_in, a_conv_w, a_log, a_dt_bias, a_norm, b_gate_w2, b_gate_bias, b_norm, c_q_norm, c_w_uq, c_kv_norm, c_w_uk, c_w_uv, w_out, norm_x, norm_mem, w_xq, w_xk, w_xv, w_xo, norm_ffn2, w_ffn2_gate, w_ffn2_up, w_ffn2_down, final_norm):
    P = dict(norm_ffn1=norm_ffn1, w_ffn1_gate=w_ffn1_gate, w_ffn1_up=w_ffn1_up, w_ffn1_down=w_ffn1_down,
             norm_mix=norm_mix, w_in=w_in, a_conv_w=a_conv_w, a_log=a_log, a_dt_bias=a_dt_bias,
             a_norm=a_norm, b_gate_w2=b_gate_w2, b_gate_bias=b_gate_bias, b_norm=b_norm,
             c_q_norm=c_q_norm, c_w_uq=c_w_uq, c_kv_norm=c_kv_norm, c_w_uk=c_w_uk, c_w_uv=c_w_uv,
             w_out=w_out, norm_x=norm_x, norm_mem=norm_mem, w_xq=w_xq, w_xk=w_xk, w_xv=w_xv, w_xo=w_xo,
             norm_ffn2=norm_ffn2, w_ffn2_gate=w_ffn2_gate, w_ffn2_up=w_ffn2_up, w_ffn2_down=w_ffn2_down)
    depth = w_in.shape[0]
    W = [_layer_weights(l, P) for l in range(depth)]
    fnorm = final_norm.reshape(1, -1).astype(F32)

    bp, tp, d = x_prompt.shape
    mp = bp * tp
    n_mem = mem_prompt.shape[1]
    tm = _pick(tp, 512)
    tabs_p = _rope_tables(jnp.arange(tp, dtype=jnp.int32))
    x = x_prompt.reshape(mp, d)
    mem = mem_prompt.reshape(bp * n_mem, d)
    p_ckv, p_kpe, p_conv, p_delta, p_gla, p_mk, p_mv = [], [], [], [], [], [], []
    for l in range(depth):
        w = W[l]
        mk, mv = _norm_mm(mem, w['norm_mem'], [w['w_xk'], w['w_xv']], tm=_pick(bp * n_mem, 512))
        x = _ffn(x, w['norm_ffn1'], *w['ffn1'], fnorm, final=False, tm=tm)
        x3 = x.reshape(bp, tp, d)
        w_a, w_b, w_c = w['w_in']
        o_a, conv_new, sd = _gdn_prompt(x3, w['norm_mix'], w_a, w['a_conv_w'], w['a_log'], w['a_dt_bias'],
                                        w['a_norm_t'], tblk=_pick(tp, 256), nb=_pick(bp, 2))
        o_b, sg = _gla_prompt(x3, w['norm_mix'], w_b, w['b_w2'], w['b_gate_bias'], w['b_norm_t'],
                              tblk=_pick(tp, 512))
        ckv, kpe, ckvb, kpeb, qlat, qpe = _mla_prep(x3, w['norm_mix'], w_c, w['c_q_norm'], w['c_kv_norm'],
                                                    w['c_w_uq'], w['c_w_uk'], tabs_p, tm=tm)
        o_c = _mla_flash(qlat, qpe, ckvb, kpeb, w['c_w_uv'], tq=_pick(tp, 512), tk=_pick(tp, 512))
        x = _xattn_prompt(x.reshape(bp, tp, d), [o_a, o_b, o_c], list(w['w_out']),
                          w['norm_x'], w['w_xq'], w['w_xo'],
                          mk.reshape(bp, n_mem, d).astype(BF16), mv.reshape(bp, n_mem, d).astype(BF16),
                          tm=tm).reshape(mp, d)
        x = _ffn(x, w['norm_ffn2'], *w['ffn2'], fnorm, final=(l == depth - 1), tm=tm)
        p_ckv.append(ckv)
        p_kpe.append(kpe)
        p_conv.append(conv_new)
        p_delta.append(sd.reshape(bp, A_DK, A_HEADS, A_DV).transpose(0, 2, 1, 3))
        sg = sg.reshape(bp, B_HEADS, B_DV, B_HEADS, B_DK)
        p_gla.append(jnp.stack([sg[:, h, :, h, :] for h in range(B_HEADS)], axis=1).transpose(0, 1, 3, 2))
        p_mk.append(mk.reshape(bp, n_mem, X_HEADS, X_HDIM))
        p_mv.append(mv.reshape(bp, n_mem, X_HEADS, X_HDIM))
    y_prompt = x.reshape(bp, tp, d)

    bs, ts, _ = x_sample.shape
    n_pages, page = page_table.shape[1], cache_ckv.shape[2]
    past_len = n_pages * page
    tabs_s = _rope_tables(past_len + jnp.arange(ts, dtype=jnp.int32))
    cache_kpe_t = jnp.swapaxes(cache_kpe, 2, 3)
    mem_k_rows, mem_v_rows = _mem_rows(cache_mem_k), _mem_rows(cache_mem_v)
    nb_dec = _pick(bs, 4)
    x = x_sample.reshape(bs, d)
    s_ckv, s_kpe, s_conv, s_delta, s_gla = [], [], [], [], []
    for l in range(depth):
        w = W[l]
        x = _ffn(x, w['norm_ffn1'], *w['ffn1'], fnorm, final=False, tm=bs)
        w_a, w_b, w_c = w['w_in']
        u_a, u_b = _norm_mm(x, w['norm_mix'], [w_a, w_b], tm=bs)
        o_a, conv_new, sd = _gdn_decode(u_a.reshape(bs, 1, UA_W), state_conv_a, state_delta,
                                        w['a_conv_w'], w['a_log'], w['a_dt_bias'], w['a_norm'], layer=l,
                                        nb=nb_dec)
        o_b, sg = _gla_decode(u_b.reshape(bs, 1, UB_W), state_gla, w['b_w2'], w['b_gate_bias'], w['b_norm'],
                              layer=l, nb=nb_dec)
        ckv, kpe, _, _, qlat, qpe = _mla_prep(x.reshape(1, bs, d), w['norm_mix'], w_c, w['c_q_norm'],
                                              w['c_kv_norm'], w['c_w_uq'], w['c_w_uk'], tabs_s, tm=bs)
        pad_heads = lambda a: jnp.pad(a[0].transpose(1, 0, 2), ((0, 0), (0, 8 - C_HEADS), (0, 0)))
        olat = _mla_paged(page_table, pad_heads(qlat), pad_heads(qpe), ckv.reshape(bs, 1, C_KV_LORA),
                          kpe.reshape(bs, 1, C_ROPE), cache_ckv, cache_kpe_t, layer=l, ppc=_pick(n_pages, 64))
        o_c = _mla_oproj(olat[:, :C_HEADS].transpose(1, 0, 2)[None], w['c_w_uv'], tm=bs)
        x = _mm_res(x, [o_a.reshape(bs, A_V), o_b.reshape(bs, B_V), o_c.reshape(bs, C_HEADS * C_VDIM)],
                    list(w['w_out']), tm=bs)
        (q,) = _norm_mm(x, w['norm_x'], [w['w_xq']], tm=bs)
        att = _xattn_decode(q.reshape(bs, 1, d), mem_k_rows, mem_v_rows, layer=l, nb=nb_dec)
        x = _mm_res(x, [att.reshape(bs, d)], [w['w_xo']], tm=bs)
        x = _ffn(x, w['norm_ffn2'], *w['ffn2'], fnorm, final=(l == depth - 1), tm=bs)
        s_ckv.append(ckv.reshape(bs, ts, C_KV_LORA))
        s_kpe.append(kpe.reshape(bs, ts, C_ROPE))
        s_conv.append(conv_new)
        s_delta.append(sd)
        s_gla.append(sg)
    y_sample = x.reshape(bs, ts, d)

    return (y_prompt, y_sample,
            jnp.stack(p_ckv), jnp.stack(p_kpe), jnp.stack(p_conv), jnp.stack(p_delta), jnp.stack(p_gla),
            jnp.stack(p_mk), jnp.stack(p_mv),
            jnp.stack(s_ckv), jnp.stack(s_kpe), jnp.stack(s_conv), jnp.stack(s_delta), jnp.stack(s_gla))
```

```python
import functools

import jax
import jax.numpy as jnp
import numpy as np
from jax import lax
from jax.experimental import pallas as pl
from jax.experimental.pallas import tpu as pltpu

F32 = jnp.float32
BF16 = jnp.bfloat16

D_MODEL = 1024
A_HEADS, A_DK, A_DV, CONV_W = 4, 64, 64, 4
B_HEADS, B_DK, B_DV, B_GATE_RANK, B_GATE_TAU = 4, 32, 64, 16, 16.0
C_HEADS, C_NOPE, C_ROPE, C_VDIM, C_Q_LORA, C_KV_LORA = 4, 128, 64, 128, 384, 256
ROPE_THETA = 10000.0
X_HEADS, X_HDIM = 4, 256
D_FF = 2816
CHUNK = 64
EPS = 1e-6

A_QK = A_HEADS * A_DK
A_V = A_HEADS * A_DV
A_CONV_CH = 2 * A_QK + A_V
A_IN = A_CONV_CH + A_V + 2 * A_HEADS
B_QK = B_HEADS * B_DK
B_V = B_HEADS * B_DV
B_IN = 2 * B_QK + 2 * B_V + B_GATE_RANK
C_IN = C_Q_LORA + C_KV_LORA + C_ROPE
MLA_SCALE = (C_NOPE + C_ROPE) ** -0.5

LANE = 128
UA_W = A_CONV_CH + A_V + LANE
UB_W = 2 * B_QK + 2 * B_V + LANE
UC_W = C_Q_LORA + C_KV_LORA + LANE
SUB = 16
LOG2E = 1.4426950408889634
VMEM_LIMIT = 56 * 1024 * 1024


def _cparams(sem):
    return pltpu.CompilerParams(dimension_semantics=sem, vmem_limit_bytes=VMEM_LIMIT)


def _const_spec(shape):
    nd = len(shape)
    return pl.BlockSpec(shape, lambda *_: (0,) * nd, pipeline_mode=pl.Buffered(1))


def _w_arr(w):
    return w[0] if isinstance(w, tuple) else w


def _w_shape(w):
    return w[0].shape[1:] if isinstance(w, tuple) else w.shape


def _w_spec(w):
    if not isinstance(w, tuple):
        return _const_spec(w.shape)
    arr, layer = w
    return pl.BlockSpec((None,) + arr.shape[1:], lambda *_: (layer,) + (0,) * (arr.ndim - 1),
                        pipeline_mode=pl.Buffered(1))


def _rms(x, w):
    return x * lax.rsqrt(jnp.mean(x * x, axis=-1, keepdims=True) + EPS) * w


def _silu(x):
    return x * jax.nn.sigmoid(x)


def _bdot(a, b):
    return jnp.dot(a.astype(BF16), b.astype(BF16), preferred_element_type=F32)


def _bdot_nt(a, b):
    return lax.dot_general(a.astype(BF16), b.astype(BF16), (((1,), (1,)), ((), ())),
                           preferred_element_type=F32)


def _bdot_tn(a, b):
    return lax.dot_general(a.astype(BF16), b.astype(BF16), (((0,), (0,)), ((), ())),
                           preferred_element_type=F32)


def _iota(shape, dim):
    return lax.broadcasted_iota(jnp.int32, shape, dim)


def _block_mask(rows, cols, rb, cb):
    return (_iota((rows, cols), 0) // rb) == (_iota((rows, cols), 1) // cb)


def _block_diag(x, mask):
    n = mask.shape[0] // x.shape[0]
    xb = x.astype(BF16)
    return jnp.where(mask, jnp.concatenate([xb] * n, axis=0), jnp.zeros_like(xb[:1, :1]))


def _diag_blocks(m, rb, cb):
    n = m.shape[0] // rb
    lane_blk = _iota((rb, m.shape[1]), 1) // cb
    out = jnp.zeros((rb, m.shape[1]), m.dtype)
    for h in range(n):
        out = jnp.where(lane_blk == h, m[h * rb:(h + 1) * rb, :], out)
    return out


def _seg_sum(x, ones_bd):
    hi = x.astype(BF16)
    lo = (x - hi.astype(F32)).astype(BF16)
    return (jnp.dot(hi, ones_bd, preferred_element_type=F32)
            + jnp.dot(lo, ones_bd, preferred_element_type=F32))


def _expand_heads(x, off, nh, w):
    lane_blk = _iota((x.shape[0], nh * w), 1) // w
    out = jnp.zeros((x.shape[0], nh * w), x.dtype)
    for h in range(nh):
        out = jnp.where(lane_blk == h, x[:, off + h:off + h + 1], out)
    return out


def _cumsum_rows(x):
    n = x.shape[0]
    row = _iota(x.shape, 0)
    s = 1
    while s < n:
        x = x + jnp.where(row >= s, pltpu.roll(x, s, axis=0), 0.0)
        s *= 2
    return x


def _ffn_kernel(x_ref, nw_ref, wg_ref, wu_ref, wd_ref, fn_ref, o_ref, *, final):
    x = x_ref[...]
    xn = _rms(x, nw_ref[...]).astype(BF16)
    g = jnp.dot(xn, wg_ref[...], preferred_element_type=F32)
    u = jnp.dot(xn, wu_ref[...], preferred_element_type=F32)
    h = (_silu(g) * u).astype(BF16)
    y = x + 0.5 * jnp.dot(h, wd_ref[...], preferred_element_type=F32)
    if final:
        y = _rms(y, fn_ref[...])
    o_ref[...] = y


def _ffn(x, nw, wg, wu, wd, fn, *, final, tm):
    m = x.shape[0]
    return pl.pallas_call(
        functools.partial(_ffn_kernel, final=final),
        grid=(m // tm,),
        in_specs=[pl.BlockSpec((tm, D_MODEL), lambda i: (i, 0)),
                  _const_spec((1, D_MODEL)),
                  _w_spec(wg), _w_spec(wu), _w_spec(wd), _const_spec((1, D_MODEL))],
        out_specs=pl.BlockSpec((tm, D_MODEL), lambda i: (i, 0)),
        out_shape=jax.ShapeDtypeStruct((m, D_MODEL), F32),
        compiler_params=_cparams(("parallel",)),
        name="ffn",
    )(x, nw, _w_arr(wg), _w_arr(wu), _w_arr(wd), fn)


def _norm_mm_kernel(x_ref, nw_ref, *refs, n_w):
    xn = _rms(x_ref[...], nw_ref[...]).astype(BF16)
    for w_ref, o_ref in zip(refs[:n_w], refs[n_w:]):
        o_ref[...] = jnp.dot(xn, w_ref[...], preferred_element_type=F32)


def _norm_mm(x, nw, ws, *, tm):
    m, d = x.shape
    return pl.pallas_call(
        functools.partial(_norm_mm_kernel, n_w=len(ws)),
        grid=(m // tm,),
        in_specs=[pl.BlockSpec((tm, d), lambda i: (i, 0)), _const_spec((1, d))]
                 + [_w_spec(w) for w in ws],
        out_specs=[pl.BlockSpec((tm, _w_shape(w)[1]), lambda i: (i, 0)) for w in ws],
        out_shape=[jax.ShapeDtypeStruct((m, _w_shape(w)[1]), F32) for w in ws],
        compiler_params=_cparams(("parallel",)),
        name="norm_mm",
    )(x, nw, *[_w_arr(w) for w in ws])


def _mm_res_kernel(x_ref, *refs, n_a):
    acc = x_ref[...]
    for a_ref, w_ref in zip(refs[:n_a], refs[n_a:2 * n_a]):
        acc = acc + jnp.dot(a_ref[...].astype(BF16), w_ref[...], preferred_element_type=F32)
    refs[2 * n_a][...] = acc


def _mm_res(x, a_list, w_list, *, tm):
    m, d = x.shape
    n_a = len(a_list)
    return pl.pallas_call(
        functools.partial(_mm_res_kernel, n_a=n_a),
        grid=(m // tm,),
        in_specs=[pl.BlockSpec((tm, d), lambda i: (i, 0))]
                 + [pl.BlockSpec((tm, a.shape[1]), lambda i: (i, 0)) for a in a_list]
                 + [_w_spec(w) for w in w_list],
        out_specs=pl.BlockSpec((tm, d), lambda i: (i, 0)),
        out_shape=jax.ShapeDtypeStruct((m, d), F32),
        compiler_params=_cparams(("parallel",)),
        name="mm_res",
    )(x, *a_list, *[_w_arr(w) for w in w_list])


def _project(x, nw_ref, win_ref):
    return jnp.dot(_rms(x, nw_ref[...]).astype(BF16), win_ref[...], preferred_element_type=F32)


def _gdn_prompt_kernel(x_ref, nw_ref, win_ref, cw_ref, alog_ref, dtb_ref, anorm_ref, o_ref, conv_ref, s_out_ref,
                       xbuf, s_scr, *, tblk, nb):
    t = pl.program_id(1)
    hw = A_HEADS * A_DV
    c = CHUNK
    nc = tblk // c

    @pl.when(t == 0)
    def _():
        for bi in range(nb):
            xbuf[bi, 0:8, :] = jnp.zeros((8, A_CONV_CH), F32)
        s_scr[...] = jnp.zeros_like(s_scr)

    cw = cw_ref[...]
    ones_bd = _block_mask(hw, hw, A_DK, A_DK).astype(BF16)
    xs, zs, qs, ks, vs, betas, gs = [], [], [], [], [], [], []
    for bi in range(nb):
        u = _project(x_ref[bi], nw_ref, win_ref)
        x = u[:, 0:A_CONV_CH]
        xbuf[bi, 8:8 + tblk, :] = x
        y = (cw[0:1] * xbuf[bi, 5:5 + tblk, :] + cw[1:2] * xbuf[bi, 6:6 + tblk, :]
             + cw[2:3] * xbuf[bi, 7:7 + tblk, :] + cw[3:4] * x)
        xbuf[bi, 0:8, :] = x[tblk - 8:tblk, :]
        qkv = _silu(y)
        ba = u[:, A_CONV_CH + A_V:UA_W]
        q = qkv[:, 0:A_QK]
        k = qkv[:, A_QK:2 * A_QK]
        xs.append(x)
        zs.append(u[:, A_CONV_CH:A_CONV_CH + A_V])
        qs.append(q * lax.rsqrt(_seg_sum(q * q, ones_bd) + EPS) * (A_DK ** -0.5))
        ks.append(k * lax.rsqrt(_seg_sum(k * k, ones_bd) + EPS))
        vs.append(qkv[:, 2 * A_QK:])
        betas.append(_expand_heads(jax.nn.sigmoid(ba), 0, A_HEADS, A_DV))
        gs.append(_expand_heads(-jnp.exp(alog_ref[...]) * jax.nn.softplus(ba + dtb_ref[...]),
                                A_HEADS, A_HEADS, A_DV))

    bd = _block_mask(hw, hw, c, c)
    row = _iota((c, hw), 0)
    col = _iota((c, hw), 1) % c
    incl = row >= col
    strict = row > col
    items = [(ci, bi) for ci in range(nc) for bi in range(nb)]
    n_it = range(len(items))

    def chunks(arrs):
        return [arrs[bi][ci * c:(ci + 1) * c] for ci, bi in items]

    qc, kc, vc, bc = chunks(qs), chunks(ks), chunks(vs), chunks(betas)
    gcum = [_cumsum_rows(gi) for gi in chunks(gs)]
    grow = [jnp.sum(jnp.where(row == col, gi, 0.0), axis=0, keepdims=True) for gi in gcum]
    decay = [jnp.exp(jnp.where(incl, gcum[n] - grow[n], -jnp.inf)) for n in n_it]
    eg = [jnp.exp(gi) for gi in gcum]
    kb = [kc[n] * bc[n] for n in n_it]
    aq = [_bdot_nt(jnp.concatenate([kb[n], qc[n]], axis=0), _block_diag(kc[n], bd)) for n in n_it]
    a = [jnp.where(strict, aq[n][:c] * decay[n], 0.0) for n in n_it]
    qk = [aq[n][c:] * decay[n] for n in n_it]
    p = [-ai for ai in a]
    pw = [_bdot(ai, _block_diag(ai, bd)) for ai in a]
    n_sq = int(np.log2(c)) - 1
    for r in range(n_sq):
        if r < n_sq - 1:
            both = [_bdot(jnp.concatenate([p[n], pw[n]], axis=0), _block_diag(pw[n], bd)) for n in n_it]
            p = [p[n] + pw[n] + both[n][:c] for n in n_it]
            pw = [both[n][c:] for n in n_it]
        else:
            p = [p[n] + pw[n] + _bdot(p[n], _block_diag(pw[n], bd)) for n in n_it]
    vb = [vc[n] * bc[n] for n in n_it]
    kbg = [kb[n] * eg[n] for n in n_it]
    uw = [_bdot(p[n], jnp.concatenate([_block_diag(vb[n], bd), _block_diag(kbg[n], bd)], axis=1)) for n in n_it]
    u = [vb[n] + uw[n][:, :hw] for n in n_it]
    wq = [jnp.concatenate([kbg[n] + uw[n][:, hw:], qc[n] * eg[n]], axis=0) for n in n_it]
    g_last = [gi[c - 1:c, :] for gi in gcum]
    kd = [kc[n] * jnp.exp(g_last[n] - gcum[n]) for n in n_it]
    eg_last = [jnp.exp(gl) for gl in g_last]

    s = [s_scr[bi] for bi in range(nb)]
    outs = [[] for _ in range(nb)]
    for n, (ci, bi) in enumerate(items):
        ws = _bdot(wq[n], _block_diag(s[bi], bd))
        r = u[n] - ws[:c]
        outs[bi].append(ws[c:] + _bdot(qk[n], _block_diag(r, bd)))
        s[bi] = eg_last[n] * s[bi] + _diag_blocks(_bdot_tn(kd[n], r), A_DK, A_DV)

    for bi in range(nb):
        s_scr[bi] = s[bi]
        o = jnp.concatenate(outs[bi], axis=0)
        o = o * lax.rsqrt(_seg_sum(o * o, ones_bd) * (1.0 / A_DV) + EPS) * anorm_ref[...]
        o_ref[bi] = o * _silu(zs[bi])

    @pl.when(t == pl.num_programs(1) - 1)
    def _():
        for bi in range(nb):
            conv_ref[bi] = xs[bi][tblk - (CONV_W - 1):tblk, :]
            s_out_ref[bi] = s[bi]


def _gdn_prompt(x, nw, w_a, cw, alog, dtb, anorm, *, tblk, nb):
    b, t, d = x.shape
    hw = A_HEADS * A_DV
    return pl.pallas_call(
        functools.partial(_gdn_prompt_kernel, tblk=tblk, nb=nb),
        grid=(b // nb, t // tblk),
        in_specs=[pl.BlockSpec((nb, tblk, d), lambda i, j: (i, j, 0)),
                  _const_spec((1, d)), _const_spec((d, UA_W)),
                  _const_spec((CONV_W, A_CONV_CH)), _const_spec((1, LANE)), _const_spec((1, LANE)),
                  _const_spec((1, hw))],
        out_specs=[pl.BlockSpec((nb, tblk, hw), lambda i, j: (i, j, 0)),
                   pl.BlockSpec((nb, CONV_W - 1, A_CONV_CH), lambda i, j: (i, 0, 0)),
                   pl.BlockSpec((nb, A_DK, hw), lambda i, j: (i, 0, 0))],
        out_shape=[jax.ShapeDtypeStruct((b, t, hw), F32),
                   jax.ShapeDtypeStruct((b, CONV_W - 1, A_CONV_CH), F32),
                   jax.ShapeDtypeStruct((b, A_DK, hw), F32)],
        scratch_shapes=[pltpu.VMEM((nb, tblk + 8, A_CONV_CH), F32), pltpu.VMEM((nb, A_DK, hw), F32)],
        compiler_params=_cparams(("parallel", "arbitrary")),
        name="gdn_prompt",
    )(x, nw, w_a, cw, alog, dtb, anorm)


def _gla_prompt_kernel(x_ref, nw_ref, win_ref, w2_ref, gb_ref, bnorm_ref, o_ref, s_out_ref, s_scr, *, tblk):
    t = pl.program_id(1)
    vw = B_HEADS * B_DV

    @pl.when(t == 0)
    def _():
        s_scr[...] = jnp.zeros_like(s_scr)

    u = _project(x_ref[0], nw_ref, win_ref)
    q = u[:, 0:B_QK] * (B_DK ** -0.5)
    k = u[:, B_QK:2 * B_QK]
    v = u[:, 2 * B_QK:2 * B_QK + B_V]
    rg = u[:, 2 * B_QK + B_V:2 * B_QK + 2 * B_V]
    glr = u[:, 2 * B_QK + 2 * B_V:]
    log_a = jax.nn.log_sigmoid(_bdot(glr, w2_ref[...]) + gb_ref[...]) / B_GATE_TAU

    c = CHUNK
    kmask = _block_mask(vw, B_QK, c, B_DK)
    vmask = _block_mask(vw, vw, c, B_DV)
    smask = _block_mask(vw, B_QK, B_DV, B_DK)
    row_k = _iota((c, B_QK), 0)
    row_s = _iota((SUB, vw), 0)
    col_s = _iota((SUB, vw), 1) % c
    st = s_scr[...]
    outs = []
    for ci in range(tblk // c):
        sl = slice(ci * c, (ci + 1) * c)
        qc, kc, vc = q[sl], k[sl], v[sl]
        b = _cumsum_rows(log_a[sl])
        o = _bdot_nt(qc * jnp.exp(b), st)
        att = []
        for i in range(c // SUB):
            r0 = i * SUB
            bref = b[r0:r0 + 1, :]
            qs = qc[r0:r0 + SUB] * jnp.exp(b[r0:r0 + SUB] - bref)
            ks = kc * jnp.exp(jnp.where(row_k < r0 + SUB, bref - b, -jnp.inf))
            a_i = _bdot_nt(qs, _block_diag(ks, kmask))
            att.append(jnp.where(row_s + r0 >= col_s, a_i, 0.0))
        o = o + _bdot(jnp.concatenate(att, axis=0), _block_diag(vc, vmask))
        outs.append(o)
        b_last = b[c - 1:c, :]
        kd = kc * jnp.exp(b_last - b)
        st = jnp.exp(b_last) * st + jnp.where(smask, _bdot_tn(vc, kd), 0.0)
    s_scr[...] = st

    o = jnp.concatenate(outs, axis=0)
    ones_bd = _block_mask(vw, vw, B_DV, B_DV).astype(BF16)
    o = o * lax.rsqrt(_seg_sum(o * o, ones_bd) * (1.0 / B_DV) + EPS) * bnorm_ref[...]
    o_ref[0] = o * _silu(rg)

    @pl.when(t == pl.num_programs(1) - 1)
    def _():
        s_out_ref[0] = st


def _gla_prompt(x, nw, w_b, w2, gb, bnorm, *, tblk):
    b, t, d = x.shape
    vw = B_HEADS * B_DV
    return pl.pallas_call(
        functools.partial(_gla_prompt_kernel, tblk=tblk),
        grid=(b, t // tblk),
        in_specs=[pl.BlockSpec((1, tblk, d), lambda i, j: (i, j, 0)),
                  _const_spec((1, d)), _const_spec((d, UB_W)),
                  _const_spec((LANE, B_QK)), _const_spec((1, B_QK)), _const_spec((1, vw))],
        out_specs=[pl.BlockSpec((1, tblk, vw), lambda i, j: (i, j, 0)),
                   pl.BlockSpec((1, vw, B_QK), lambda i, j: (i, 0, 0))],
        out_shape=[jax.ShapeDtypeStruct((b, t, vw), F32),
                   jax.ShapeDtypeStruct((b, vw, B_QK), F32)],
        scratch_shapes=[pltpu.VMEM((vw, B_QK), F32)],
        compiler_params=_cparams(("parallel", "arbitrary")),
        name="gla_prompt",
    )(x, nw, w_b, w2, gb, bnorm)


def _mla_prep_kernel(x_ref, nw_ref, win_ref, cqn_ref, ckvn_ref, wuq_ref, wuk_ref, ck_ref, sk_ref, cq_ref, sq_ref,
                     ckv_o, kpe_o, ckvb_o, kpeb_o, qlat_o, qpe_o):
    u = _project(x_ref[0], nw_ref, win_ref)
    cq = _rms(u[:, 0:C_Q_LORA], cqn_ref[...]).astype(BF16)
    ckv = _rms(u[:, C_Q_LORA:C_Q_LORA + C_KV_LORA], ckvn_ref[...])
    half = C_ROPE // 2
    kx = u[:, C_Q_LORA + C_KV_LORA:]
    lane = _iota(kx.shape, 1)
    kswap = jnp.where(lane < half, pltpu.roll(kx, LANE - half, axis=1), pltpu.roll(kx, half, axis=1))
    kpe = (kx * ck_ref[...] + kswap * sk_ref[...])[:, 0:C_ROPE]
    ckv_o[0] = ckv
    kpe_o[0] = kpe
    ckvb_o[0] = ckv.astype(BF16)
    kpeb_o[0] = kpe.astype(BF16)

    qf = jnp.dot(cq, wuq_ref[...], preferred_element_type=F32)
    nope_w = C_HEADS * C_NOPE
    rope_w = C_HEADS * C_ROPE
    qr = qf[:, nope_w:]
    lane_r = _iota(qr.shape, 1) % C_ROPE
    qswap = jnp.where(lane_r < half, pltpu.roll(qr, rope_w - half, axis=1), pltpu.roll(qr, half, axis=1))
    qpe = qr * cq_ref[...] + qswap * sq_ref[...]
    for h in range(C_HEADS):
        qlat_o[0, h] = jnp.dot(qf[:, h * C_NOPE:(h + 1) * C_NOPE].astype(BF16), wuk_ref[h],
                               preferred_element_type=F32).astype(BF16)
        qpe_o[0, h] = qpe[:, h * C_ROPE:(h + 1) * C_ROPE].astype(BF16)


def _mla_prep(x, nw, w_c, cqn, ckvn, wuq, wuk, tabs, *, tm):
    b, t, d = x.shape
    ck, sk, cq, sq = tabs
    per_pos = ck.shape[0] != 1
    rope_w = C_HEADS * C_ROPE

    def tab_spec(w):
        if per_pos:
            return pl.BlockSpec((tm, w), lambda i, j: (j, 0))
        return _const_spec((1, w))

    def tok_spec(w):
        return pl.BlockSpec((1, tm, w), lambda i, j: (i, j, 0))

    def head_spec(w):
        return pl.BlockSpec((1, C_HEADS, tm, w), lambda i, j: (i, 0, j, 0))

    return pl.pallas_call(
        _mla_prep_kernel,
        grid=(b, t // tm),
        in_specs=[tok_spec(d), _const_spec((1, d)), _const_spec((d, UC_W)),
                  _const_spec((1, C_Q_LORA)), _const_spec((1, C_KV_LORA)),
                  _const_spec(wuq.shape), _const_spec(wuk.shape),
                  tab_spec(LANE), tab_spec(LANE), tab_spec(rope_w), tab_spec(rope_w)],
        out_specs=[tok_spec(C_KV_LORA), tok_spec(C_ROPE), tok_spec(C_KV_LORA), tok_spec(C_ROPE),
                   head_spec(C_KV_LORA), head_spec(C_ROPE)],
        out_shape=[jax.ShapeDtypeStruct((b, t, C_KV_LORA), F32),
                   jax.ShapeDtypeStruct((b, t, C_ROPE), F32),
                   jax.ShapeDtypeStruct((b, t, C_KV_LORA), BF16),
                   jax.ShapeDtypeStruct((b, t, C_ROPE), BF16),
                   jax.ShapeDtypeStruct((b, C_HEADS, t, C_KV_LORA), BF16),
                   jax.ShapeDtypeStruct((b, C_HEADS, t, C_ROPE), BF16)],
        compiler_params=_cparams(("parallel", "parallel")),
        name="mla_prep",
    )(x, nw, w_c, cqn, ckvn, wuq, wuk, ck, sk, cq, sq)


def _mla_flash_kernel(qi_ref, kj_ref, ql_ref, qp_ref, k_ref, p_ref, wuv_ref, o_ref,
                      m_scr, l_scr, acc_scr, *, tq, tk):
    g = pl.program_id(1)
    i = qi_ref[g]
    j = kj_ref[g]
    last_j = (i * tq + tq - 1) // tk
    c2 = MLA_SCALE * LOG2E

    @pl.when(j == 0)
    def _():
        m_scr[...] = jnp.full_like(m_scr, -jnp.inf)
        l_scr[...] = jnp.zeros_like(l_scr)
        acc_scr[...] = jnp.zeros_like(acc_scr)

    def step(masked):
        kv = k_ref[0]
        pe = p_ref[0]
        if masked:
            visible = j * tk + _iota((tq, tk), 1) <= i * tq + _iota((tq, tk), 0)
        for h in range(C_HEADS):
            s = (lax.dot_general(ql_ref[0, h], kv, (((1,), (1,)), ((), ())), preferred_element_type=F32)
                 + lax.dot_general(qp_ref[0, h], pe, (((1,), (1,)), ((), ())), preferred_element_type=F32))
            if masked:
                s = jnp.where(visible, s, -jnp.inf)
            m_old = m_scr[h]
            m_new = jnp.maximum(m_old, jnp.max(s, axis=-1, keepdims=True))
            alpha = jnp.exp2((m_old - m_new) * c2)
            p = jnp.exp2(s * c2 - m_new * c2)
            l_scr[h] = alpha * l_scr[h] + jnp.sum(p, axis=-1, keepdims=True)
            m_scr[h] = m_new
            acc_scr[h] = alpha * acc_scr[h] + jnp.dot(p.astype(BF16), kv, preferred_element_type=F32)

    has_masked = j * tk + tk - 1 > i * tq

    @pl.when(has_masked)
    def _():
        step(True)

    @pl.when(jnp.logical_not(has_masked))
    def _():
        step(False)

    @pl.when(j == last_j)
    def _():
        o_ref[0] = jnp.concatenate(
            [jnp.dot((acc_scr[h] / l_scr[h]).astype(BF16), wuv_ref[h], preferred_element_type=F32)
             for h in range(C_HEADS)], axis=-1)


def _mla_flash(qlat, qpe, ckvb, kpeb, wuv, *, tq, tk):
    b, _, t, _ = qlat.shape
    pairs = [(i, j) for i in range(t // tq) for j in range((i * tq + tq - 1) // tk + 1)]
    qi = jnp.asarray([p[0] for p in pairs], jnp.int32)
    kj = jnp.asarray([p[1] for p in pairs], jnp.int32)
    grid_spec = pltpu.PrefetchScalarGridSpec(
        num_scalar_prefetch=2,
        grid=(b, len(pairs)),
        in_specs=[pl.BlockSpec((1, C_HEADS, tq, C_KV_LORA), lambda bi, g, qi, kj: (bi, 0, qi[g], 0)),
                  pl.BlockSpec((1, C_HEADS, tq, C_ROPE), lambda bi, g, qi, kj: (bi, 0, qi[g], 0)),
                  pl.BlockSpec((1, tk, C_KV_LORA), lambda bi, g, qi, kj: (bi, kj[g], 0)),
                  pl.BlockSpec((1, tk, C_ROPE), lambda bi, g, qi, kj: (bi, kj[g], 0)),
                  _const_spec(wuv.shape)],
        out_specs=pl.BlockSpec((1, tq, C_HEADS * C_VDIM), lambda bi, g, qi, kj: (bi, qi[g], 0)),
        scratch_shapes=[pltpu.VMEM((C_HEADS, tq, 1), F32), pltpu.VMEM((C_HEADS, tq, 1), F32),
                        pltpu.VMEM((C_HEADS, tq, C_KV_LORA), F32)],
    )
    return pl.pallas_call(
        functools.partial(_mla_flash_kernel, tq=tq, tk=tk),
        grid_spec=grid_spec,
        out_shape=jax.ShapeDtypeStruct((b, t, C_HEADS * C_VDIM), F32),
        compiler_params=_cparams(("parallel", "arbitrary")),
        name="mla_flash",
    )(qi, kj, qlat, qpe, ckvb, kpeb, wuv)


def _mla_oproj_kernel(ol_ref, wuv_ref, o_ref):
    o_ref[0] = jnp.concatenate(
        [jnp.dot(ol_ref[0, h].astype(BF16), wuv_ref[h], preferred_element_type=F32)
         for h in range(C_HEADS)], axis=-1)


def _mla_oproj(olat, wuv, *, tm):
    b, _, t, _ = olat.shape
    return pl.pallas_call(
        _mla_oproj_kernel,
        grid=(b, t // tm),
        in_specs=[pl.BlockSpec((1, C_HEADS, tm, C_KV_LORA), lambda i, j: (i, 0, j, 0)),
                  _const_spec(wuv.shape)],
        out_specs=pl.BlockSpec((1, tm, C_HEADS * C_VDIM), lambda i, j: (i, j, 0)),
        out_shape=jax.ShapeDtypeStruct((b, t, C_HEADS * C_VDIM), F32),
        compiler_params=_cparams(("parallel", "parallel")),
        name="mla_oproj",
    )(olat, wuv)


def _mla_paged_kernel(pt_ref, ql_ref, qp_ref, cn_ref, pn_ref, ckv_hbm, kpe_hbm, o_ref,
                      kbuf, pbuf, sem, m_scr, l_scr, acc_scr, *, layer, ppc, nch, page):
    b = pl.program_id(0)
    c = pl.program_id(1)
    g = b * nch + c
    total = pl.num_programs(0) * nch
    slot = g % 2

    def copies(bb, cc, sl, p):
        pid = pt_ref[bb, cc * ppc + p]
        dst = pl.ds(pl.multiple_of(p * page, page), page)
        return (pltpu.make_async_copy(ckv_hbm.at[layer, pid], kbuf.at[sl, dst, :], sem.at[0, sl]),
                pltpu.make_async_copy(kpe_hbm.at[layer, pid], pbuf.at[sl, :, dst], sem.at[1, sl]))

    def issue(bb, cc, sl):
        def body(p, carry):
            for prio, cp in enumerate(copies(bb, cc, sl, p)):
                cp.start(priority=prio)
            return carry
        lax.fori_loop(0, ppc, body, 0)

    @pl.when(g == 0)
    def _():
        issue(0, 0, 0)

    @pl.when(g + 1 < total)
    def _():
        issue((g + 1) // nch, (g + 1) % nch, 1 - slot)

    def wait_body(p, carry):
        for cp in copies(b, c, slot, p):
            cp.wait()
        return carry
    lax.fori_loop(0, ppc, wait_body, 0)

    @pl.when(c == 0)
    def _():
        m_scr[...] = jnp.full_like(m_scr, -jnp.inf)
        l_scr[...] = jnp.zeros_like(l_scr)
        acc_scr[...] = jnp.zeros_like(acc_scr)

    ql = ql_ref[0]
    qp = qp_ref[0]
    kv = kbuf[slot].astype(BF16)
    pe = pbuf[slot].astype(BF16)
    s = (lax.dot_general(ql, kv, (((1,), (1,)), ((), ())), preferred_element_type=F32)
         + jnp.dot(qp, pe, preferred_element_type=F32)) * MLA_SCALE
    m_old = m_scr[...]
    m_new = jnp.maximum(m_old, jnp.max(s, axis=-1, keepdims=True))
    alpha = jnp.exp(m_old - m_new)
    p = jnp.exp(s - m_new)
    l_new = alpha * l_scr[...] + jnp.sum(p, axis=-1, keepdims=True)
    acc_new = alpha * acc_scr[...] + jnp.dot(p.astype(BF16), kv, preferred_element_type=F32)
    m_scr[...] = m_new
    l_scr[...] = l_new
    acc_scr[...] = acc_new

    @pl.when(c == nch - 1)
    def _():
        cn = cn_ref[0].astype(BF16).astype(F32)
        pn = pn_ref[0].astype(BF16).astype(F32)
        s_n = (jnp.sum(ql.astype(F32) * cn, axis=-1, keepdims=True)
               + jnp.sum(qp.astype(F32) * pn, axis=-1, keepdims=True)) * MLA_SCALE
        m_f = jnp.maximum(m_new, s_n)
        a_f = jnp.exp(m_new - m_f)
        p_n = jnp.exp(s_n - m_f)
        l_f = a_f * l_new + p_n
        acc_f = a_f * acc_new + p_n.astype(BF16).astype(F32) * cn
        o_ref[0] = acc_f / l_f


def _mla_paged(page_table, qlat, qpe, ckv_new, kpe_new, cache_ckv, cache_kpe, *, layer, ppc):
    b, hp, _ = qlat.shape
    n_pages = page_table.shape[1]
    page = cache_ckv.shape[2]
    nch = n_pages // ppc
    grid_spec = pltpu.PrefetchScalarGridSpec(
        num_scalar_prefetch=1,
        grid=(b, nch),
        in_specs=[pl.BlockSpec((1, hp, C_KV_LORA), lambda i, j, pt: (i, 0, 0)),
                  pl.BlockSpec((1, hp, C_ROPE), lambda i, j, pt: (i, 0, 0)),
                  pl.BlockSpec((1, 1, C_KV_LORA), lambda i, j, pt: (i, 0, 0)),
                  pl.BlockSpec((1, 1, C_ROPE), lambda i, j, pt: (i, 0, 0)),
                  pl.BlockSpec(memory_space=pl.ANY),
                  pl.BlockSpec(memory_space=pl.ANY)],
        out_specs=pl.BlockSpec((1, hp, C_KV_LORA), lambda i, j, pt: (i, 0, 0)),
        scratch_shapes=[pltpu.VMEM((2, ppc * page, C_KV_LORA), F32),
                        pltpu.VMEM((2, C_ROPE, ppc * page), F32),
                        pltpu.SemaphoreType.DMA((2, 2)),
                        pltpu.VMEM((hp, 1), F32), pltpu.VMEM((hp, 1), F32),
                        pltpu.VMEM((hp, C_KV_LORA), F32)],
    )
    return pl.pallas_call(
        functools.partial(_mla_paged_kernel, layer=layer, ppc=ppc, nch=nch, page=page),
        grid_spec=grid_spec,
        out_shape=jax.ShapeDtypeStruct((b, hp, C_KV_LORA), F32),
        compiler_params=_cparams(("arbitrary", "arbitrary")),
        name="mla_paged",
    )(page_table, qlat, qpe, ckv_new, kpe_new, cache_ckv, cache_kpe)


def _xattn_prompt_kernel(x_ref, oa_ref, ob_ref, oc_ref, wa_ref, wb_ref, wc_ref,
                         nw_ref, wq_ref, wo_ref, mk_ref, mv_ref, o_ref):
    x = x_ref[0]
    for a_ref, w_ref in ((oa_ref, wa_ref), (ob_ref, wb_ref), (oc_ref, wc_ref)):
        x = x + jnp.dot(a_ref[0].astype(BF16), w_ref[...], preferred_element_type=F32)
    xn = _rms(x, nw_ref[...]).astype(BF16)
    q = jnp.dot(xn, wq_ref[...], preferred_element_type=F32).astype(BF16)
    heads = []
    for h in range(X_HEADS):
        sl = slice(h * X_HDIM, (h + 1) * X_HDIM)
        s = lax.dot_general(q[:, sl], mk_ref[0, :, sl], (((1,), (1,)), ((), ())),
                            preferred_element_type=F32) * (X_HDIM ** -0.5)
        e = jnp.exp(s - jnp.max(s, axis=-1, keepdims=True))
        p = (e / jnp.sum(e, axis=-1, keepdims=True)).astype(BF16)
        heads.append(jnp.dot(p, mv_ref[0, :, sl], preferred_element_type=F32).astype(BF16))
    o_ref[0] = x + jnp.dot(jnp.concatenate(heads, axis=-1), wo_ref[...], preferred_element_type=F32)


def _xattn_prompt(x, o_list, w_out_list, nw, wq, wo, mk, mv, *, tm):
    b, t, d = x.shape
    n_mem = mk.shape[1]
    return pl.pallas_call(
        _xattn_prompt_kernel,
        grid=(b, t // tm),
        in_specs=[pl.BlockSpec((1, tm, d), lambda i, j: (i, j, 0))]
                 + [pl.BlockSpec((1, tm, o.shape[2]), lambda i, j: (i, j, 0)) for o in o_list]
                 + [_w_spec(w) for w in w_out_list]
                 + [_const_spec((1, d)), _w_spec(wq), _w_spec(wo),
                  pl.BlockSpec((1, n_mem, d), lambda i, j: (i, 0, 0)),
                  pl.BlockSpec((1, n_mem, d), lambda i, j: (i, 0, 0))],
        out_specs=pl.BlockSpec((1, tm, d), lambda i, j: (i, j, 0)),
        out_shape=jax.ShapeDtypeStruct((b, t, d), F32),
        compiler_params=_cparams(("parallel", "parallel")),
        name="xattn_prompt",
    )(x, *o_list, *[_w_arr(w) for w in w_out_list], nw, _w_arr(wq), _w_arr(wo), mk, mv)


def _xattn_decode_kernel(q_ref, mk_ref, mv_ref, o_ref):
    for bi in range(q_ref.shape[0]):
        _xattn_decode_one(q_ref, mk_ref, mv_ref, o_ref, bi)


def _xattn_decode_one(q_ref, mk_ref, mv_ref, o_ref, bi):
    nt = X_HDIM // LANE
    grp = nt * X_HEADS
    n_rows = mk_ref.shape[2]
    qrow = q_ref[bi]
    qm = jnp.concatenate([qrow[:, h * X_HDIM + t * LANE:h * X_HDIM + (t + 1) * LANE]
                          for t in range(nt) for h in range(X_HEADS)], axis=0)
    sel = _iota((grp, n_rows), 0) == _iota((grp, n_rows), 1) % grp
    s_all = _bdot_nt(qm, mk_ref[0, bi])
    part = jnp.sum(jnp.where(sel, s_all, 0.0), axis=0, keepdims=True)
    lane = _iota((1, n_rows), 1)
    s = part
    for t in range(1, nt):
        s = s + jnp.where(lane % grp < X_HEADS, pltpu.roll(part, n_rows - t * X_HEADS, axis=1),
                          pltpu.roll(part, t * X_HEADS, axis=1))
    s = s * (X_HDIM ** -0.5)
    p = jnp.zeros_like(s)
    for h in range(X_HEADS):
        mine = lane % X_HEADS == h
        e = jnp.exp(s - jnp.max(jnp.where(mine, s, -jnp.inf), axis=-1, keepdims=True))
        den = jnp.sum(jnp.where(lane % grp == h, e, 0.0), axis=-1, keepdims=True)
        p = jnp.where(mine, e / den, p)
    o = _bdot(jnp.where(sel, p, 0.0), mv_ref[0, bi])
    o_ref[bi] = jnp.concatenate([o[t * X_HEADS + h:t * X_HEADS + h + 1]
                                for h in range(X_HEADS) for t in range(nt)], axis=-1)


def _mem_rows(mem):
    dep, b, n_mem, nh, hd = mem.shape
    nt = hd // LANE
    return mem.reshape(dep, b, n_mem, nh, nt, LANE).transpose(0, 1, 2, 4, 3, 5).reshape(dep, b, n_mem * nt * nh, LANE)


def _xattn_decode(q, mk_rows, mv_rows, *, layer, nb):
    _, b, n_rows, _ = mk_rows.shape
    d = X_HEADS * X_HDIM
    assert X_HDIM == 2 * LANE
    mem_spec = pl.BlockSpec((1, nb, n_rows, LANE), lambda i: (layer, i, 0, 0))
    return pl.pallas_call(
        _xattn_decode_kernel,
        grid=(b // nb,),
        in_specs=[pl.BlockSpec((nb, 1, d), lambda i: (i, 0, 0)), mem_spec, mem_spec],
        out_specs=pl.BlockSpec((nb, 1, d), lambda i: (i, 0, 0)),
        out_shape=jax.ShapeDtypeStruct((b, 1, d), F32),
        compiler_params=_cparams(("parallel",)),
        name="xattn_decode",
    )(q, mk_rows, mv_rows)


def _columns(rows):
    w = rows[0].shape[1]
    assert len(rows) <= LANE
    pad = [jnp.zeros((LANE - len(rows), w), F32)] if len(rows) < LANE else []
    return jnp.concatenate(rows + pad, axis=0).T


def _gdn_decode_kernel(u_ref, cs_ref, s_ref, cw_ref, alog_ref, dtb_ref, anorm_ref,
                       o_ref, cs_out_ref, s_out_ref):
    seqs = range(u_ref.shape[0])
    heads = range(A_HEADS)
    items = [(bi, h) for bi in seqs for h in heads]
    cw = cw_ref[...]
    x = [u_ref[bi, :, 0:A_CONV_CH] for bi in seqs]
    z = [u_ref[bi, :, A_CONV_CH:A_CONV_CH + A_V] for bi in seqs]
    ba = [u_ref[bi, :, A_CONV_CH + A_V:UA_W] for bi in seqs]
    cs = [cs_ref[0, bi] for bi in seqs]
    qkv = [_silu(cw[0:1] * cs[bi][0:1] + cw[1:2] * cs[bi][1:2] + cw[2:3] * cs[bi][2:3] + cw[3:4] * x[bi])
           for bi in seqs]
    for bi in seqs:
        cs_out_ref[bi] = jnp.concatenate([cs[bi][1:CONV_W - 1], x[bi]], axis=0)
    beta = [jax.nn.sigmoid(b) for b in ba]
    eg_all = [jnp.exp(-jnp.exp(alog_ref[...]) * jax.nn.softplus(b + dtb_ref[...])) for b in ba]
    q_raw = [qkv[bi][:, h * A_DK:(h + 1) * A_DK] for bi, h in items]
    k_raw = [qkv[bi][:, A_QK + h * A_DK:A_QK + (h + 1) * A_DK] for bi, h in items]
    v = [qkv[bi][:, 2 * A_QK + h * A_DV:2 * A_QK + (h + 1) * A_DV] for bi, h in items]
    q = [a * lax.rsqrt(jnp.sum(a * a, axis=-1, keepdims=True) + EPS) * (A_DK ** -0.5) for a in q_raw]
    k = [a * lax.rsqrt(jnp.sum(a * a, axis=-1, keepdims=True) + EPS) for a in k_raw]
    cols = _columns(k + q)
    kcol = [cols[:, n:n + 1] for n in range(len(items))]
    qcol = [cols[:, len(items) + n:len(items) + n + 1] for n in range(len(items))]
    s = [s_ref[0, bi, h] for bi, h in items]
    bh = [beta[bi][:, h:h + 1] for bi, h in items]
    eg = [eg_all[bi][:, A_HEADS + h:A_HEADS + h + 1] for bi, h in items]
    ks_row = [jnp.sum(kcol[n] * s[n], axis=0, keepdims=True) for n in range(len(items))]
    qs_row = [jnp.sum(qcol[n] * s[n], axis=0, keepdims=True) for n in range(len(items))]
    r = [v[n] * bh[n] - (bh[n] * eg[n]) * ks_row[n] for n in range(len(items))]
    qk = [jnp.sum(q[n] * k[n], axis=-1, keepdims=True) for n in range(len(items))]
    o = [eg[n] * qs_row[n] + qk[n] * r[n] for n in range(len(items))]
    for n, (bi, h) in enumerate(items):
        s_out_ref[bi, h] = eg[n] * s[n] + kcol[n] * r[n]
    o = [a * lax.rsqrt(jnp.mean(a * a, axis=-1, keepdims=True) + EPS) * anorm_ref[...] for a in o]
    for bi in seqs:
        o_ref[bi] = jnp.concatenate(o[bi * A_HEADS:(bi + 1) * A_HEADS], axis=-1) * _silu(z[bi])


def _gdn_decode(u_a, cs, s, cw, alog, dtb, anorm, *, layer, nb):
    b = u_a.shape[0]
    return pl.pallas_call(
        _gdn_decode_kernel,
        grid=(b // nb,),
        in_specs=[pl.BlockSpec((nb, 1, UA_W), lambda i: (i, 0, 0)),
                  pl.BlockSpec((1, nb, CONV_W - 1, A_CONV_CH), lambda i: (layer, i, 0, 0)),
                  pl.BlockSpec((1, nb, A_HEADS, A_DK, A_DV), lambda i: (layer, i, 0, 0, 0)),
                  _const_spec((CONV_W, A_CONV_CH)), _const_spec((1, LANE)), _const_spec((1, LANE)),
                  _const_spec((1, A_DV))],
        out_specs=[pl.BlockSpec((nb, 1, A_V), lambda i: (i, 0, 0)),
                   pl.BlockSpec((nb, CONV_W - 1, A_CONV_CH), lambda i: (i, 0, 0)),
                   pl.BlockSpec((nb, A_HEADS, A_DK, A_DV), lambda i: (i, 0, 0, 0))],
        out_shape=[jax.ShapeDtypeStruct((b, 1, A_V), F32),
                   jax.ShapeDtypeStruct((b, CONV_W - 1, A_CONV_CH), F32),
                   jax.ShapeDtypeStruct((b, A_HEADS, A_DK, A_DV), F32)],
        compiler_params=_cparams(("parallel",)),
        name="gdn_decode",
    )(u_a, cs, s, cw, alog, dtb, anorm)


def _gla_decode_kernel(u_ref, s_ref, w2_ref, gb_ref, bnorm_ref, o_ref, s_out_ref):
    nb = u_ref.shape[0]
    seqs = range(nb)
    items = [(bi, h) for bi in seqs for h in range(B_HEADS)]
    u = [u_ref[bi] for bi in seqs]
    q = [a[:, 0:B_QK] * (B_DK ** -0.5) for a in u]
    k = [a[:, B_QK:2 * B_QK] for a in u]
    rg = [a[:, 2 * B_QK + B_V:2 * B_QK + 2 * B_V] for a in u]
    pad = [jnp.zeros(((-nb) % 8, LANE), F32)] if nb % 8 else []
    gate = _bdot(jnp.concatenate([a[:, 2 * B_QK + 2 * B_V:] for a in u] + pad, axis=0), w2_ref[...])
    dec = [jnp.exp(jax.nn.log_sigmoid(gate[bi:bi + 1] + gb_ref[...]) / B_GATE_TAU) for bi in seqs]
    cols = _columns(k + [q[bi] * dec[bi] for bi in seqs] + dec)
    v = [u[bi][:, 2 * B_QK + h * B_DV:2 * B_QK + (h + 1) * B_DV] for bi, h in items]
    s = [s_ref[0, bi, h] for bi, h in items]
    kcol = [cols[h * B_DK:(h + 1) * B_DK, bi:bi + 1] for bi, h in items]
    qacol = [cols[h * B_DK:(h + 1) * B_DK, nb + bi:nb + bi + 1] for bi, h in items]
    acol = [cols[h * B_DK:(h + 1) * B_DK, 2 * nb + bi:2 * nb + bi + 1] for bi, h in items]
    qk = [jnp.sum(q[bi][:, h * B_DK:(h + 1) * B_DK] * k[bi][:, h * B_DK:(h + 1) * B_DK], axis=-1, keepdims=True)
          for bi, h in items]
    o = [jnp.sum(qacol[n] * s[n], axis=0, keepdims=True) + qk[n] * v[n] for n in range(len(items))]
    for n, (bi, h) in enumerate(items):
        s_out_ref[bi, h] = acol[n] * s[n] + kcol[n] * v[n]
    o = [a * lax.rsqrt(jnp.mean(a * a, axis=-1, keepdims=True) + EPS) * bnorm_ref[...] for a in o]
    for bi in seqs:
        o_ref[bi] = jnp.concatenate(o[bi * B_HEADS:(bi + 1) * B_HEADS], axis=-1) * _silu(rg[bi])


def _gla_decode(u_b, s, w2, gb, bnorm, *, layer, nb):
    b = u_b.shape[0]
    return pl.pallas_call(
        _gla_decode_kernel,
        grid=(b // nb,),
        in_specs=[pl.BlockSpec((nb, 1, UB_W), lambda i: (i, 0, 0)),
                  pl.BlockSpec((1, nb, B_HEADS, B_DK, B_DV), lambda i: (layer, i, 0, 0, 0)),
                  _const_spec((LANE, B_QK)), _const_spec((1, B_QK)), _const_spec((1, B_DV))],
        out_specs=[pl.BlockSpec((nb, 1, B_V), lambda i: (i, 0, 0)),
                   pl.BlockSpec((nb, B_HEADS, B_DK, B_DV), lambda i: (i, 0, 0, 0))],
        out_shape=[jax.ShapeDtypeStruct((b, 1, B_V), F32),
                   jax.ShapeDtypeStruct((b, B_HEADS, B_DK, B_DV), F32)],
        compiler_params=_cparams(("parallel",)),
        name="gla_decode",
    )(u_b, s, w2, gb, bnorm)


def _pad_cols(w, n):
    return jnp.pad(w, ((0, 0), (0, n - w.shape[1])))


def _rope_tables(pos):
    half = C_ROPE // 2
    inv = ROPE_THETA ** (-jnp.arange(half, dtype=F32) / half)
    ang = pos.astype(F32)[:, None] * inv[None, :]
    cos, sin = jnp.cos(ang), jnp.sin(ang)
    zero = jnp.zeros_like(cos)
    ck = jnp.concatenate([cos, cos, zero, zero], axis=-1)
    sk = jnp.concatenate([-sin, sin, zero, zero], axis=-1)
    cq = jnp.tile(jnp.concatenate([cos, cos], axis=-1), (1, C_HEADS))
    sq = jnp.tile(jnp.concatenate([-sin, sin], axis=-1), (1, C_HEADS))
    return ck, sk, cq, sq


STACKED_BF16 = ('w_ffn1_gate', 'w_ffn1_up', 'w_ffn1_down', 'w_ffn2_gate', 'w_ffn2_up', 'w_ffn2_down',
                'w_xq', 'w_xo', 'w_xk', 'w_xv')


def _layer_weights(l, P, Pb):
    w_in = P['w_in'][l]
    o_b, o_c = A_IN, A_IN + B_IN
    w_a = jnp.concatenate([w_in[:, :A_CONV_CH + A_V], _pad_cols(w_in[:, A_CONV_CH + A_V:A_IN], LANE)], axis=1)
    w_b = jnp.concatenate([w_in[:, o_b:o_b + 2 * B_QK + 2 * B_V],
                           _pad_cols(w_in[:, o_b + 2 * B_QK + 2 * B_V:o_c], LANE)], axis=1)
    w_c = jnp.concatenate([w_in[:, o_c:o_c + C_Q_LORA + C_KV_LORA],
                           _pad_cols(w_in[:, o_c + C_Q_LORA + C_KV_LORA:], LANE)], axis=1)
    wuq = P['c_w_uq'][l].reshape(C_Q_LORA, C_HEADS, C_NOPE + C_ROPE)
    wuq = jnp.concatenate([wuq[:, :, :C_NOPE].reshape(C_Q_LORA, -1),
                           wuq[:, :, C_NOPE:].reshape(C_Q_LORA, -1)], axis=1)
    w_out = P['w_out'][l]
    row = lambda a: a.reshape(1, -1).astype(F32)
    head_lanes = lambda a: jnp.pad(a.reshape(1, -1).astype(F32), ((0, 0), (A_HEADS, LANE - 2 * A_HEADS)))
    return dict(
        norm_ffn1=row(P['norm_ffn1'][l]), norm_ffn2=row(P['norm_ffn2'][l]),
        ffn1=tuple((Pb[n], l) for n in ('w_ffn1_gate', 'w_ffn1_up', 'w_ffn1_down')),
        ffn2=tuple((Pb[n], l) for n in ('w_ffn2_gate', 'w_ffn2_up', 'w_ffn2_down')),
        norm_mix=row(P['norm_mix'][l]),
        w_in=(w_a.astype(BF16), w_b.astype(BF16), w_c.astype(BF16)),
        a_conv_w=P['a_conv_w'][l].astype(F32),
        a_log=head_lanes(P['a_log'][l]), a_dt_bias=head_lanes(P['a_dt_bias'][l]),
        a_norm=row(P['a_norm'][l]), a_norm_t=row(jnp.tile(P['a_norm'][l], A_HEADS)),
        b_w2=jnp.pad(P['b_gate_w2'][l], ((0, LANE - B_GATE_RANK), (0, 0))).astype(BF16),
        b_gate_bias=row(P['b_gate_bias'][l]),
        b_norm=row(P['b_norm'][l]), b_norm_t=row(jnp.tile(P['b_norm'][l], B_HEADS)),
        c_q_norm=row(P['c_q_norm'][l]), c_kv_norm=row(P['c_kv_norm'][l]),
        c_w_uq=wuq.astype(BF16),
        c_w_uk=jnp.transpose(P['c_w_uk'][l], (1, 2, 0)).astype(BF16),
        c_w_uv=jnp.transpose(P['c_w_uv'][l], (1, 0, 2)).astype(BF16),
        w_out=(w_out[:A_V].astype(BF16), w_out[A_V:A_V + B_V].astype(BF16), w_out[A_V + B_V:].astype(BF16)),
        norm_x=row(P['norm_x'][l]), norm_mem=row(P['norm_mem'][l]),
        w_xq=(Pb['w_xq'], l), w_xo=(Pb['w_xo'], l), w_xk=(Pb['w_xk'], l), w_xv=(Pb['w_xv'], l),
    )


def _pick(n, pref):
    return pref if n % pref == 0 else n


def kernel(x_prompt, x_sample, mem_prompt, cache_ckv, cache_kpe, page_table, state_conv_a, state_delta, state_gla, cache_mem_k, cache_mem_v, norm_ffn1, w_ffn1_gate, w_ffn1_up, w_ffn1_down, norm_mix, w_in, a_conv_w, a_log, a_dt_bias, a_norm, b_gate_w2, b_gate_bias, b_norm, c_q_norm, c_w_uq, c_kv_norm, c_w_uk, c_w_uv, w_out, norm_x, norm_mem, w_xq, w_xk, w_xv, w_xo, norm_ffn2, w_ffn2_gate, w_ffn2_up, w_ffn2_down, final_norm):
    P = dict(norm_ffn1=norm_ffn1, w_ffn1_gate=w_ffn1_gate, w_ffn1_up=w_ffn1_up, w_ffn1_down=w_ffn1_down,
             norm_mix=norm_mix, w_in=w_in, a_conv_w=a_conv_w, a_log=a_log, a_dt_bias=a_dt_bias,
             a_norm=a_norm, b_gate_w2=b_gate_w2, b_gate_bias=b_gate_bias, b_norm=b_norm,
             c_q_norm=c_q_norm, c_w_uq=c_w_uq, c_kv_norm=c_kv_norm, c_w_uk=c_w_uk, c_w_uv=c_w_uv,
             w_out=w_out, norm_x=norm_x, norm_mem=norm_mem, w_xq=w_xq, w_xk=w_xk, w_xv=w_xv, w_xo=w_xo,
             norm_ffn2=norm_ffn2, w_ffn2_gate=w_ffn2_gate, w_ffn2_up=w_ffn2_up, w_ffn2_down=w_ffn2_down)
    depth = w_in.shape[0]
    Pb = {n: P[n].astype(BF16) for n in STACKED_BF16}
    W = [_layer_weights(l, P, Pb) for l in range(depth)]
    fnorm = final_norm.reshape(1, -1).astype(F32)

    bp, tp, d = x_prompt.shape
    mp = bp * tp
    n_mem = mem_prompt.shape[1]
    tm = _pick(tp, 512)
    tabs_p = _rope_tables(jnp.arange(tp, dtype=jnp.int32))
    x = x_prompt.reshape(mp, d)
    mem = mem_prompt.reshape(bp * n_mem, d)
    p_ckv, p_kpe, p_conv, p_delta, p_gla, p_mk, p_mv = [], [], [], [], [], [], []
    for l in range(depth):
        w = W[l]
        mk, mv = _norm_mm(mem, w['norm_mem'], [w['w_xk'], w['w_xv']], tm=_pick(bp * n_mem, 512))
        x = _ffn(x, w['norm_ffn1'], *w['ffn1'], fnorm, final=False, tm=tm)
        x3 = x.reshape(bp, tp, d)
        w_a, w_b, w_c = w['w_in']
        o_a, conv_new, sd = _gdn_prompt(x3, w['norm_mix'], w_a, w['a_conv_w'], w['a_log'], w['a_dt_bias'],
                                        w['a_norm_t'], tblk=_pick(tp, 256), nb=_pick(bp, 2))
        o_b, sg = _gla_prompt(x3, w['norm_mix'], w_b, w['b_w2'], w['b_gate_bias'], w['b_norm_t'],
                              tblk=_pick(tp, 512))
        ckv, kpe, ckvb, kpeb, qlat, qpe = _mla_prep(x3, w['norm_mix'], w_c, w['c_q_norm'], w['c_kv_norm'],
                                                    w['c_w_uq'], w['c_w_uk'], tabs_p, tm=tm)
        o_c = _mla_flash(qlat, qpe, ckvb, kpeb, w['c_w_uv'], tq=_pick(tp, 512), tk=_pick(tp, 512))
        x = _xattn_prompt(x.reshape(bp, tp, d), [o_a, o_b, o_c], list(w['w_out']),
                          w['norm_x'], w['w_xq'], w['w_xo'],
                          mk.reshape(bp, n_mem, d).astype(BF16), mv.reshape(bp, n_mem, d).astype(BF16),
                          tm=tm).reshape(mp, d)
        x = _ffn(x, w['norm_ffn2'], *w['ffn2'], fnorm, final=(l == depth - 1), tm=tm)
        p_ckv.append(ckv)
        p_kpe.append(kpe)
        p_conv.append(conv_new)
        p_delta.append(sd.reshape(bp, A_DK, A_HEADS, A_DV).transpose(0, 2, 1, 3))
        sg = sg.reshape(bp, B_HEADS, B_DV, B_HEADS, B_DK)
        p_gla.append(jnp.stack([sg[:, h, :, h, :] for h in range(B_HEADS)], axis=1).transpose(0, 1, 3, 2))
        p_mk.append(mk.reshape(bp, n_mem, X_HEADS, X_HDIM))
        p_mv.append(mv.reshape(bp, n_mem, X_HEADS, X_HDIM))
    y_prompt = x.reshape(bp, tp, d)

    bs, ts, _ = x_sample.shape
    n_pages, page = page_table.shape[1], cache_ckv.shape[2]
    past_len = n_pages * page
    tabs_s = _rope_tables(past_len + jnp.arange(ts, dtype=jnp.int32))
    cache_kpe_t = jnp.swapaxes(cache_kpe, 2, 3)
    mem_k_rows, mem_v_rows = _mem_rows(cache_mem_k), _mem_rows(cache_mem_v)
    nb_dec, nb_mem = _pick(bs, 8), _pick(bs, 4)
    x = x_sample.reshape(bs, d)
    s_ckv, s_kpe, s_conv, s_delta, s_gla = [], [], [], [], []
    for l in range(depth):
        w = W[l]
        x = _ffn(x, w['norm_ffn1'], *w['ffn1'], fnorm, final=False, tm=bs)
        w_a, w_b, w_c = w['w_in']
        u_a, u_b = _norm_mm(x, w['norm_mix'], [w_a, w_b], tm=bs)
        o_a, conv_new, sd = _gdn_decode(u_a.reshape(bs, 1, UA_W), state_conv_a, state_delta,
                                        w['a_conv_w'], w['a_log'], w['a_dt_bias'], w['a_norm'], layer=l,
                                        nb=nb_dec)
        o_b, sg = _gla_decode(u_b.reshape(bs, 1, UB_W), state_gla, w['b_w2'], w['b_gate_bias'], w['b_norm'],
                              layer=l, nb=nb_dec)
        ckv, kpe, _, _, qlat, qpe = _mla_prep(x.reshape(1, bs, d), w['norm_mix'], w_c, w['c_q_norm'],
                                              w['c_kv_norm'], w['c_w_uq'], w['c_w_uk'], tabs_s, tm=bs)
        pad_heads = lambda a: jnp.pad(a[0].transpose(1, 0, 2), ((0, 0), (0, 8 - C_HEADS), (0, 0)))
        olat = _mla_paged(page_table, pad_heads(qlat), pad_heads(qpe), ckv.reshape(bs, 1, C_KV_LORA),
                          kpe.reshape(bs, 1, C_ROPE), cache_ckv, cache_kpe_t, layer=l, ppc=_pick(n_pages, 64))
        o_c = _mla_oproj(olat[:, :C_HEADS].transpose(1, 0, 2)[None], w['c_w_uv'], tm=bs)
        x = _mm_res(x, [o_a.reshape(bs, A_V), o_b.reshape(bs, B_V), o_c.reshape(bs, C_HEADS * C_VDIM)],
                    list(w['w_out']), tm=bs)
        (q,) = _norm_mm(x, w['norm_x'], [w['w_xq']], tm=bs)
        att = _xattn_decode(q.reshape(bs, 1, d), mem_k_rows, mem_v_rows, layer=l, nb=nb_mem)
        x = _mm_res(x, [att.reshape(bs, d)], [w['w_xo']], tm=bs)
        x = _ffn(x, w['norm_ffn2'], *w['ffn2'], fnorm, final=(l == depth - 1), tm=bs)
        s_ckv.append(ckv.reshape(bs, ts, C_KV_LORA))
        s_kpe.append(kpe.reshape(bs, ts, C_ROPE))
        s_conv.append(conv_new)
        s_delta.append(sd)
        s_gla.append(sg)
    y_sample = x.reshape(bs, ts, d)

    return (y_prompt, y_sample,
            jnp.stack(p_ckv), jnp.stack(p_kpe), jnp.stack(p_conv), jnp.stack(p_delta), jnp.stack(p_gla),
            jnp.stack(p_mk), jnp.stack(p_mv),
            jnp.stack(s_ckv), jnp.stack(s_kpe), jnp.stack(s_conv), jnp.stack(s_delta), jnp.stack(s_gla))
```

```python
import functools

import jax
import jax.numpy as jnp
import numpy as np
from jax import lax
from jax.experimental import pallas as pl
from jax.experimental.pallas import tpu as pltpu

F32 = jnp.float32
BF16 = jnp.bfloat16

D_MODEL = 1024
A_HEADS, A_DK, A_DV, CONV_W = 4, 64, 64, 4
B_HEADS, B_DK, B_DV, B_GATE_RANK, B_GATE_TAU = 4, 32, 64, 16, 16.0
C_HEADS, C_NOPE, C_ROPE, C_VDIM, C_Q_LORA, C_KV_LORA = 4, 128, 64, 128, 384, 256
ROPE_THETA = 10000.0
X_HEADS, X_HDIM = 4, 256
D_FF = 2816
CHUNK = 64
EPS = 1e-6

A_QK = A_HEADS * A_DK
A_V = A_HEADS * A_DV
A_CONV_CH = 2 * A_QK + A_V
A_IN = A_CONV_CH + A_V + 2 * A_HEADS
B_QK = B_HEADS * B_DK
B_V = B_HEADS * B_DV
B_IN = 2 * B_QK + 2 * B_V + B_GATE_RANK
C_IN = C_Q_LORA + C_KV_LORA + C_ROPE
MLA_SCALE = (C_NOPE + C_ROPE) ** -0.5

LANE = 128
UA_W = A_CONV_CH + A_V + LANE
UB_W = 2 * B_QK + 2 * B_V + LANE
UC_W = C_Q_LORA + C_KV_LORA + LANE
SUB = 16
LOG2E = 1.4426950408889634
VMEM_LIMIT = 56 * 1024 * 1024


def _cparams(sem):
    return pltpu.CompilerParams(dimension_semantics=sem, vmem_limit_bytes=VMEM_LIMIT)


def _const_spec(shape):
    nd = len(shape)
    return pl.BlockSpec(shape, lambda *_: (0,) * nd, pipeline_mode=pl.Buffered(1))


def _w_arr(w):
    return w[0] if isinstance(w, tuple) else w


def _w_shape(w):
    return w[0].shape[1:] if isinstance(w, tuple) else w.shape


def _w_spec(w):
    if not isinstance(w, tuple):
        return _const_spec(w.shape)
    arr, layer = w
    return pl.BlockSpec((None,) + arr.shape[1:], lambda *_: (layer,) + (0,) * (arr.ndim - 1),
                        pipeline_mode=pl.Buffered(1))


def _rms(x, w):
    return x * lax.rsqrt(jnp.mean(x * x, axis=-1, keepdims=True) + EPS) * w


def _silu(x):
    return x * jax.nn.sigmoid(x)


def _bdot(a, b):
    return jnp.dot(a.astype(BF16), b.astype(BF16), preferred_element_type=F32)


def _bdot_nt(a, b):
    return lax.dot_general(a.astype(BF16), b.astype(BF16), (((1,), (1,)), ((), ())),
                           preferred_element_type=F32)


def _bdot_tn(a, b):
    return lax.dot_general(a.astype(BF16), b.astype(BF16), (((0,), (0,)), ((), ())),
                           preferred_element_type=F32)


def _iota(shape, dim):
    return lax.broadcasted_iota(jnp.int32, shape, dim)


def _block_mask(rows, cols, rb, cb):
    return (_iota((rows, cols), 0) // rb) == (_iota((rows, cols), 1) // cb)


def _block_diag(x, mask):
    n = mask.shape[0] // x.shape[0]
    xb = x.astype(BF16)
    return jnp.where(mask, jnp.concatenate([xb] * n, axis=0), jnp.zeros_like(xb[:1, :1]))


def _diag_blocks(m, rb, cb):
    n = m.shape[0] // rb
    lane_blk = _iota((rb, m.shape[1]), 1) // cb
    out = jnp.zeros((rb, m.shape[1]), m.dtype)
    for h in range(n):
        out = jnp.where(lane_blk == h, m[h * rb:(h + 1) * rb, :], out)
    return out


def _seg_sum(x, ones_bd):
    hi = x.astype(BF16)
    lo = (x - hi.astype(F32)).astype(BF16)
    return (jnp.dot(hi, ones_bd, preferred_element_type=F32)
            + jnp.dot(lo, ones_bd, preferred_element_type=F32))


def _expand_heads(x, off, nh, w):
    lane_blk = _iota((x.shape[0], nh * w), 1) // w
    out = jnp.zeros((x.shape[0], nh * w), x.dtype)
    for h in range(nh):
        out = jnp.where(lane_blk == h, x[:, off + h:off + h + 1], out)
    return out


def _cumsum_rows(x):
    n = x.shape[0]
    row = _iota(x.shape, 0)
    s = 1
    while s < n:
        x = x + jnp.where(row >= s, pltpu.roll(x, s, axis=0), 0.0)
        s *= 2
    return x


def _ffn_kernel(x_ref, nw_ref, wg_ref, wu_ref, wd_ref, fn_ref, o_ref, *, final):
    x = x_ref[...]
    xn = _rms(x, nw_ref[...]).astype(BF16)
    g = jnp.dot(xn, wg_ref[...], preferred_element_type=F32)
    u = jnp.dot(xn, wu_ref[...], preferred_element_type=F32)
    h = (_silu(g) * u).astype(BF16)
    y = x + 0.5 * jnp.dot(h, wd_ref[...], preferred_element_type=F32)
    if final:
        y = _rms(y, fn_ref[...])
    o_ref[...] = y


def _ffn(x, nw, wg, wu, wd, fn, *, final, tm):
    m = x.shape[0]
    return pl.pallas_call(
        functools.partial(_ffn_kernel, final=final),
        grid=(m // tm,),
        in_specs=[pl.BlockSpec((tm, D_MODEL), lambda i: (i, 0)),
                  _const_spec((1, D_MODEL)),
                  _w_spec(wg), _w_spec(wu), _w_spec(wd), _const_spec((1, D_MODEL))],
        out_specs=pl.BlockSpec((tm, D_MODEL), lambda i: (i, 0)),
        out_shape=jax.ShapeDtypeStruct((m, D_MODEL), F32),
        compiler_params=_cparams(("parallel",)),
        name="ffn",
    )(x, nw, _w_arr(wg), _w_arr(wu), _w_arr(wd), fn)


def _norm_mm_kernel(x_ref, nw_ref, *refs, n_w):
    xn = _rms(x_ref[...], nw_ref[...]).astype(BF16)
    for w_ref, o_ref in zip(refs[:n_w], refs[n_w:]):
        o_ref[...] = jnp.dot(xn, w_ref[...], preferred_element_type=F32)


def _norm_mm(x, nw, ws, *, tm):
    m, d = x.shape
    return pl.pallas_call(
        functools.partial(_norm_mm_kernel, n_w=len(ws)),
        grid=(m // tm,),
        in_specs=[pl.BlockSpec((tm, d), lambda i: (i, 0)), _const_spec((1, d))]
                 + [_w_spec(w) for w in ws],
        out_specs=[pl.BlockSpec((tm, _w_shape(w)[1]), lambda i: (i, 0)) for w in ws],
        out_shape=[jax.ShapeDtypeStruct((m, _w_shape(w)[1]), F32) for w in ws],
        compiler_params=_cparams(("parallel",)),
        name="norm_mm",
    )(x, nw, *[_w_arr(w) for w in ws])


def _mm_res_kernel(x_ref, *refs, n_a):
    acc = x_ref[...]
    for a_ref, w_ref in zip(refs[:n_a], refs[n_a:2 * n_a]):
        acc = acc + jnp.dot(a_ref[...].astype(BF16), w_ref[...], preferred_element_type=F32)
    refs[2 * n_a][...] = acc


def _mm_res(x, a_list, w_list, *, tm):
    m, d = x.shape
    n_a = len(a_list)
    return pl.pallas_call(
        functools.partial(_mm_res_kernel, n_a=n_a),
        grid=(m // tm,),
        in_specs=[pl.BlockSpec((tm, d), lambda i: (i, 0))]
                 + [pl.BlockSpec((tm, a.shape[1]), lambda i: (i, 0)) for a in a_list]
                 + [_w_spec(w) for w in w_list],
        out_specs=pl.BlockSpec((tm, d), lambda i: (i, 0)),
        out_shape=jax.ShapeDtypeStruct((m, d), F32),
        compiler_params=_cparams(("parallel",)),
        name="mm_res",
    )(x, *a_list, *[_w_arr(w) for w in w_list])


def _project(x, nw_ref, win_ref):
    return jnp.dot(_rms(x, nw_ref[...]).astype(BF16), win_ref[...], preferred_element_type=F32)


def _gdn_prompt_kernel(x_ref, nw_ref, win_ref, cw_ref, alog_ref, dtb_ref, anorm_ref, o_ref, conv_ref, s_out_ref,
                       xbuf, s_scr, *, tblk, nb):
    t = pl.program_id(1)
    hw = A_HEADS * A_DV
    c = CHUNK
    nc = tblk // c

    @pl.when(t == 0)
    def _():
        for bi in range(nb):
            xbuf[bi, 0:8, :] = jnp.zeros((8, A_CONV_CH), F32)
        s_scr[...] = jnp.zeros_like(s_scr)

    cw = cw_ref[...]
    ones_bd = _block_mask(hw, hw, A_DK, A_DK).astype(BF16)
    xs, zs, qs, ks, vs, betas, gs = [], [], [], [], [], [], []
    for bi in range(nb):
        u = _project(x_ref[bi], nw_ref, win_ref)
        x = u[:, 0:A_CONV_CH]
        xbuf[bi, 8:8 + tblk, :] = x
        y = (cw[0:1] * xbuf[bi, 5:5 + tblk, :] + cw[1:2] * xbuf[bi, 6:6 + tblk, :]
             + cw[2:3] * xbuf[bi, 7:7 + tblk, :] + cw[3:4] * x)
        xbuf[bi, 0:8, :] = x[tblk - 8:tblk, :]
        qkv = _silu(y)
        ba = u[:, A_CONV_CH + A_V:UA_W]
        q = qkv[:, 0:A_QK]
        k = qkv[:, A_QK:2 * A_QK]
        xs.append(x)
        zs.append(u[:, A_CONV_CH:A_CONV_CH + A_V])
        qs.append(q * lax.rsqrt(_seg_sum(q * q, ones_bd) + EPS) * (A_DK ** -0.5))
        ks.append(k * lax.rsqrt(_seg_sum(k * k, ones_bd) + EPS))
        vs.append(qkv[:, 2 * A_QK:])
        betas.append(_expand_heads(jax.nn.sigmoid(ba), 0, A_HEADS, A_DV))
        gs.append(_expand_heads(-jnp.exp(alog_ref[...]) * jax.nn.softplus(ba + dtb_ref[...]),
                                A_HEADS, A_HEADS, A_DV))

    bd = _block_mask(hw, hw, c, c)
    row = _iota((c, hw), 0)
    col = _iota((c, hw), 1) % c
    incl = row >= col
    strict = row > col
    items = [(ci, bi) for ci in range(nc) for bi in range(nb)]
    n_it = range(len(items))

    def chunks(arrs):
        return [arrs[bi][ci * c:(ci + 1) * c] for ci, bi in items]

    qc, kc, vc, bc = chunks(qs), chunks(ks), chunks(vs), chunks(betas)
    gcum = [_cumsum_rows(gi) for gi in chunks(gs)]
    grow = [jnp.sum(jnp.where(row == col, gi, 0.0), axis=0, keepdims=True) for gi in gcum]
    decay = [jnp.exp(jnp.where(incl, gcum[n] - grow[n], -jnp.inf)) for n in n_it]
    eg = [jnp.exp(gi) for gi in gcum]
    kb = [kc[n] * bc[n] for n in n_it]
    aq = [_bdot_nt(jnp.concatenate([kb[n], qc[n]], axis=0), _block_diag(kc[n], bd)) for n in n_it]
    a = [jnp.where(strict, aq[n][:c] * decay[n], 0.0) for n in n_it]
    qk = [aq[n][c:] * decay[n] for n in n_it]
    p = [-ai for ai in a]
    pw = [_bdot(ai, _block_diag(ai, bd)) for ai in a]
    n_sq = int(np.log2(c)) - 1
    for r in range(n_sq):
        if r < n_sq - 1:
            both = [_bdot(jnp.concatenate([p[n], pw[n]], axis=0), _block_diag(pw[n], bd)) for n in n_it]
            p = [p[n] + pw[n] + both[n][:c] for n in n_it]
            pw = [both[n][c:] for n in n_it]
        else:
            p = [p[n] + pw[n] + _bdot(p[n], _block_diag(pw[n], bd)) for n in n_it]
    vb = [vc[n] * bc[n] for n in n_it]
    kbg = [kb[n] * eg[n] for n in n_it]
    uw = [_bdot(p[n], jnp.concatenate([_block_diag(vb[n], bd), _block_diag(kbg[n], bd)], axis=1)) for n in n_it]
    u = [vb[n] + uw[n][:, :hw] for n in n_it]
    wq = [jnp.concatenate([kbg[n] + uw[n][:, hw:], qc[n] * eg[n]], axis=0) for n in n_it]
    g_last = [gi[c - 1:c, :] for gi in gcum]
    kd = [kc[n] * jnp.exp(g_last[n] - gcum[n]) for n in n_it]
    eg_last = [jnp.exp(gl) for gl in g_last]

    s = [s_scr[bi] for bi in range(nb)]
    outs = [[] for _ in range(nb)]
    for n, (ci, bi) in enumerate(items):
        ws = _bdot(wq[n], _block_diag(s[bi], bd))
        r = u[n] - ws[:c]
        outs[bi].append(ws[c:] + _bdot(qk[n], _block_diag(r, bd)))
        s[bi] = eg_last[n] * s[bi] + _diag_blocks(_bdot_tn(kd[n], r), A_DK, A_DV)

    for bi in range(nb):
        s_scr[bi] = s[bi]
        o = jnp.concatenate(outs[bi], axis=0)
        o = o * lax.rsqrt(_seg_sum(o * o, ones_bd) * (1.0 / A_DV) + EPS) * anorm_ref[...]
        o_ref[bi] = o * _silu(zs[bi])

    @pl.when(t == pl.num_programs(1) - 1)
    def _():
        for bi in range(nb):
            conv_ref[bi] = xs[bi][tblk - (CONV_W - 1):tblk, :]
            s_out_ref[bi] = s[bi]


def _gdn_prompt(x, nw, w_a, cw, alog, dtb, anorm, *, tblk, nb):
    b, t, d = x.shape
    hw = A_HEADS * A_DV
    return pl.pallas_call(
        functools.partial(_gdn_prompt_kernel, tblk=tblk, nb=nb),
        grid=(b // nb, t // tblk),
        in_specs=[pl.BlockSpec((nb, tblk, d), lambda i, j: (i, j, 0)),
                  _const_spec((1, d)), _const_spec((d, UA_W)),
                  _const_spec((CONV_W, A_CONV_CH)), _const_spec((1, LANE)), _const_spec((1, LANE)),
                  _const_spec((1, hw))],
        out_specs=[pl.BlockSpec((nb, tblk, hw), lambda i, j: (i, j, 0)),
                   pl.BlockSpec((nb, CONV_W - 1, A_CONV_CH), lambda i, j: (i, 0, 0)),
                   pl.BlockSpec((nb, A_DK, hw), lambda i, j: (i, 0, 0))],
        out_shape=[jax.ShapeDtypeStruct((b, t, hw), F32),
                   jax.ShapeDtypeStruct((b, CONV_W - 1, A_CONV_CH), F32),
                   jax.ShapeDtypeStruct((b, A_DK, hw), F32)],
        scratch_shapes=[pltpu.VMEM((nb, tblk + 8, A_CONV_CH), F32), pltpu.VMEM((nb, A_DK, hw), F32)],
        compiler_params=_cparams(("parallel", "arbitrary")),
        name="gdn_prompt",
    )(x, nw, w_a, cw, alog, dtb, anorm)


def _gla_prompt_kernel(x_ref, nw_ref, win_ref, w2_ref, gb_ref, bnorm_ref, o_ref, s_out_ref, s_scr, *, tblk):
    t = pl.program_id(1)
    vw = B_HEADS * B_DV

    @pl.when(t == 0)
    def _():
        s_scr[...] = jnp.zeros_like(s_scr)

    u = _project(x_ref[0], nw_ref, win_ref)
    q = u[:, 0:B_QK] * (B_DK ** -0.5)
    k = u[:, B_QK:2 * B_QK]
    v = u[:, 2 * B_QK:2 * B_QK + B_V]
    rg = u[:, 2 * B_QK + B_V:2 * B_QK + 2 * B_V]
    glr = u[:, 2 * B_QK + 2 * B_V:]
    log_a = jax.nn.log_sigmoid(_bdot(glr, w2_ref[...]) + gb_ref[...]) / B_GATE_TAU

    c = CHUNK
    kmask = _block_mask(vw, B_QK, c, B_DK)
    vmask = _block_mask(vw, vw, c, B_DV)
    smask = _block_mask(vw, B_QK, B_DV, B_DK)
    row_k = _iota((c, B_QK), 0)
    row_s = _iota((SUB, vw), 0)
    col_s = _iota((SUB, vw), 1) % c
    st = s_scr[...]
    outs = []
    for ci in range(tblk // c):
        sl = slice(ci * c, (ci + 1) * c)
        qc, kc, vc = q[sl], k[sl], v[sl]
        b = _cumsum_rows(log_a[sl])
        o = _bdot_nt(qc * jnp.exp(b), st)
        att = []
        for i in range(c // SUB):
            r0 = i * SUB
            bref = b[r0:r0 + 1, :]
            qs = qc[r0:r0 + SUB] * jnp.exp(b[r0:r0 + SUB] - bref)
            ks = kc * jnp.exp(jnp.where(row_k < r0 + SUB, bref - b, -jnp.inf))
            a_i = _bdot_nt(qs, _block_diag(ks, kmask))
            att.append(jnp.where(row_s + r0 >= col_s, a_i, 0.0))
        o = o + _bdot(jnp.concatenate(att, axis=0), _block_diag(vc, vmask))
        outs.append(o)
        b_last = b[c - 1:c, :]
        kd = kc * jnp.exp(b_last - b)
        st = jnp.exp(b_last) * st + jnp.where(smask, _bdot_tn(vc, kd), 0.0)
    s_scr[...] = st

    o = jnp.concatenate(outs, axis=0)
    ones_bd = _block_mask(vw, vw, B_DV, B_DV).astype(BF16)
    o = o * lax.rsqrt(_seg_sum(o * o, ones_bd) * (1.0 / B_DV) + EPS) * bnorm_ref[...]
    o_ref[0] = o * _silu(rg)

    @pl.when(t == pl.num_programs(1) - 1)
    def _():
        s_out_ref[0] = st


def _gla_prompt(x, nw, w_b, w2, gb, bnorm, *, tblk):
    b, t, d = x.shape
    vw = B_HEADS * B_DV
    return pl.pallas_call(
        functools.partial(_gla_prompt_kernel, tblk=tblk),
        grid=(b, t // tblk),
        in_specs=[pl.BlockSpec((1, tblk, d), lambda i, j: (i, j, 0)),
                  _const_spec((1, d)), _const_spec((d, UB_W)),
                  _const_spec((LANE, B_QK)), _const_spec((1, B_QK)), _const_spec((1, vw))],
        out_specs=[pl.BlockSpec((1, tblk, vw), lambda i, j: (i, j, 0)),
                   pl.BlockSpec((1, vw, B_QK), lambda i, j: (i, 0, 0))],
        out_shape=[jax.ShapeDtypeStruct((b, t, vw), F32),
                   jax.ShapeDtypeStruct((b, vw, B_QK), F32)],
        scratch_shapes=[pltpu.VMEM((vw, B_QK), F32)],
        compiler_params=_cparams(("parallel", "arbitrary")),
        name="gla_prompt",
    )(x, nw, w_b, w2, gb, bnorm)


def _mla_prep_kernel(x_ref, nw_ref, win_ref, cqn_ref, ckvn_ref, wuq_ref, wuk_ref, ck_ref, sk_ref, cq_ref, sq_ref,
                     ckv_o, kpe_o, ckvb_o, kpeb_o, qlat_o, qpe_o):
    u = _project(x_ref[0], nw_ref, win_ref)
    cq = _rms(u[:, 0:C_Q_LORA], cqn_ref[...]).astype(BF16)
    ckv = _rms(u[:, C_Q_LORA:C_Q_LORA + C_KV_LORA], ckvn_ref[...])
    half = C_ROPE // 2
    kx = u[:, C_Q_LORA + C_KV_LORA:]
    lane = _iota(kx.shape, 1)
    kswap = jnp.where(lane < half, pltpu.roll(kx, LANE - half, axis=1), pltpu.roll(kx, half, axis=1))
    kpe = (kx * ck_ref[...] + kswap * sk_ref[...])[:, 0:C_ROPE]
    ckv_o[0] = ckv
    kpe_o[0] = kpe
    ckvb_o[0] = ckv.astype(BF16)
    kpeb_o[0] = kpe.astype(BF16)

    qf = jnp.dot(cq, wuq_ref[...], preferred_element_type=F32)
    nope_w = C_HEADS * C_NOPE
    rope_w = C_HEADS * C_ROPE
    qr = qf[:, nope_w:]
    lane_r = _iota(qr.shape, 1) % C_ROPE
    qswap = jnp.where(lane_r < half, pltpu.roll(qr, rope_w - half, axis=1), pltpu.roll(qr, half, axis=1))
    qpe = qr * cq_ref[...] + qswap * sq_ref[...]
    for h in range(C_HEADS):
        qlat_o[0, h] = jnp.dot(qf[:, h * C_NOPE:(h + 1) * C_NOPE].astype(BF16), wuk_ref[h],
                               preferred_element_type=F32).astype(BF16)
        qpe_o[0, h] = qpe[:, h * C_ROPE:(h + 1) * C_ROPE].astype(BF16)


def _mla_prep(x, nw, w_c, cqn, ckvn, wuq, wuk, tabs, *, tm):
    b, t, d = x.shape
    ck, sk, cq, sq = tabs
    per_pos = ck.shape[0] != 1
    rope_w = C_HEADS * C_ROPE

    def tab_spec(w):
        if per_pos:
            return pl.BlockSpec((tm, w), lambda i, j: (j, 0))
        return _const_spec((1, w))

    def tok_spec(w):
        return pl.BlockSpec((1, tm, w), lambda i, j: (i, j, 0))

    def head_spec(w):
        return pl.BlockSpec((1, C_HEADS, tm, w), lambda i, j: (i, 0, j, 0))

    return pl.pallas_call(
        _mla_prep_kernel,
        grid=(b, t // tm),
        in_specs=[tok_spec(d), _const_spec((1, d)), _const_spec((d, UC_W)),
                  _const_spec((1, C_Q_LORA)), _const_spec((1, C_KV_LORA)),
                  _const_spec(wuq.shape), _const_spec(wuk.shape),
                  tab_spec(LANE), tab_spec(LANE), tab_spec(rope_w), tab_spec(rope_w)],
        out_specs=[tok_spec(C_KV_LORA), tok_spec(C_ROPE), tok_spec(C_KV_LORA), tok_spec(C_ROPE),
                   head_spec(C_KV_LORA), head_spec(C_ROPE)],
        out_shape=[jax.ShapeDtypeStruct((b, t, C_KV_LORA), F32),
                   jax.ShapeDtypeStruct((b, t, C_ROPE), F32),
                   jax.ShapeDtypeStruct((b, t, C_KV_LORA), BF16),
                   jax.ShapeDtypeStruct((b, t, C_ROPE), BF16),
                   jax.ShapeDtypeStruct((b, C_HEADS, t, C_KV_LORA), BF16),
                   jax.ShapeDtypeStruct((b, C_HEADS, t, C_ROPE), BF16)],
        compiler_params=_cparams(("parallel", "parallel")),
        name="mla_prep",
    )(x, nw, w_c, cqn, ckvn, wuq, wuk, ck, sk, cq, sq)


def _mla_flash_kernel(qi_ref, kj_ref, ql_ref, qp_ref, k_ref, p_ref, wuv_ref, o_ref,
                      m_scr, l_scr, acc_scr, *, tq, tk):
    g = pl.program_id(1)
    i = qi_ref[g]
    j = kj_ref[g]
    last_j = (i * tq + tq - 1) // tk
    c2 = MLA_SCALE * LOG2E

    @pl.when(j == 0)
    def _():
        m_scr[...] = jnp.full_like(m_scr, -jnp.inf)
        l_scr[...] = jnp.zeros_like(l_scr)
        acc_scr[...] = jnp.zeros_like(acc_scr)

    def step(masked):
        kv = k_ref[0]
        pe = p_ref[0]
        if masked and tq == tk:
            parts = [(0, tq // 2, tk // 2), (tq // 2, tq, tk)]
        else:
            parts = [(0, tq, tk)]
        for h in range(C_HEADS):
            for r0, r1, nk in parts:
                s = (lax.dot_general(ql_ref[0, h, r0:r1, :], kv[:nk], (((1,), (1,)), ((), ())),
                                     preferred_element_type=F32)
                     + lax.dot_general(qp_ref[0, h, r0:r1, :], pe[:nk], (((1,), (1,)), ((), ())),
                                       preferred_element_type=F32))
                if masked:
                    visible = j * tk + _iota((r1 - r0, nk), 1) <= i * tq + r0 + _iota((r1 - r0, nk), 0)
                    s = jnp.where(visible, s, -jnp.inf)
                m_old = m_scr[h, r0:r1, :]
                m_new = jnp.maximum(m_old, jnp.max(s, axis=-1, keepdims=True))
                alpha = jnp.exp2((m_old - m_new) * c2)
                p = jnp.exp2(s * c2 - m_new * c2)
                l_scr[h, r0:r1, :] = alpha * l_scr[h, r0:r1, :] + jnp.sum(p, axis=-1, keepdims=True)
                m_scr[h, r0:r1, :] = m_new
                acc_scr[h, r0:r1, :] = alpha * acc_scr[h, r0:r1, :] + jnp.dot(
                    p.astype(BF16), kv[:nk], preferred_element_type=F32)

    has_masked = j * tk + tk - 1 > i * tq

    @pl.when(has_masked)
    def _():
        step(True)

    @pl.when(jnp.logical_not(has_masked))
    def _():
        step(False)

    @pl.when(j == last_j)
    def _():
        o_ref[0] = jnp.concatenate(
            [jnp.dot((acc_scr[h] / l_scr[h]).astype(BF16), wuv_ref[h], preferred_element_type=F32)
             for h in range(C_HEADS)], axis=-1)


def _mla_flash(qlat, qpe, ckvb, kpeb, wuv, *, tq, tk):
    b, _, t, _ = qlat.shape
    pairs = [(i, j) for i in range(t // tq) for j in range((i * tq + tq - 1) // tk + 1)]
    qi = jnp.asarray([p[0] for p in pairs], jnp.int32)
    kj = jnp.asarray([p[1] for p in pairs], jnp.int32)
    grid_spec = pltpu.PrefetchScalarGridSpec(
        num_scalar_prefetch=2,
        grid=(b, len(pairs)),
        in_specs=[pl.BlockSpec((1, C_HEADS, tq, C_KV_LORA), lambda bi, g, qi, kj: (bi, 0, qi[g], 0)),
                  pl.BlockSpec((1, C_HEADS, tq, C_ROPE), lambda bi, g, qi, kj: (bi, 0, qi[g], 0)),
                  pl.BlockSpec((1, tk, C_KV_LORA), lambda bi, g, qi, kj: (bi, kj[g], 0)),
                  pl.BlockSpec((1, tk, C_ROPE), lambda bi, g, qi, kj: (bi, kj[g], 0)),
                  _const_spec(wuv.shape)],
        out_specs=pl.BlockSpec((1, tq, C_HEADS * C_VDIM), lambda bi, g, qi, kj: (bi, qi[g], 0)),
        scratch_shapes=[pltpu.VMEM((C_HEADS, tq, 1), F32), pltpu.VMEM((C_HEADS, tq, 1), F32),
                        pltpu.VMEM((C_HEADS, tq, C_KV_LORA), F32)],
    )
    return pl.pallas_call(
        functools.partial(_mla_flash_kernel, tq=tq, tk=tk),
        grid_spec=grid_spec,
        out_shape=jax.ShapeDtypeStruct((b, t, C_HEADS * C_VDIM), F32),
        compiler_params=_cparams(("parallel", "arbitrary")),
        name="mla_flash",
    )(qi, kj, qlat, qpe, ckvb, kpeb, wuv)


def _mla_oproj_kernel(ol_ref, wuv_ref, o_ref):
    o_ref[0] = jnp.concatenate(
        [jnp.dot(ol_ref[0, h].astype(BF16), wuv_ref[h], preferred_element_type=F32)
         for h in range(C_HEADS)], axis=-1)


def _mla_oproj(olat, wuv, *, tm):
    b, _, t, _ = olat.shape
    return pl.pallas_call(
        _mla_oproj_kernel,
        grid=(b, t // tm),
        in_specs=[pl.BlockSpec((1, C_HEADS, tm, C_KV_LORA), lambda i, j: (i, 0, j, 0)),
                  _const_spec(wuv.shape)],
        out_specs=pl.BlockSpec((1, tm, C_HEADS * C_VDIM), lambda i, j: (i, j, 0)),
        out_shape=jax.ShapeDtypeStruct((b, t, C_HEADS * C_VDIM), F32),
        compiler_params=_cparams(("parallel", "parallel")),
        name="mla_oproj",
    )(olat, wuv)


def _mla_paged_kernel(pt_ref, ql_ref, qp_ref, cn_ref, pn_ref, ckv_hbm, kpe_hbm, o_ref,
                      kbuf, pbuf, sem, m_scr, l_scr, acc_scr, *, layer, ppc, nch, page):
    b = pl.program_id(0)
    c = pl.program_id(1)
    g = b * nch + c
    total = pl.num_programs(0) * nch
    slot = g % 2

    def copies(pid, sl, p):
        dst = pl.ds(p * page, page)
        return (pltpu.make_async_copy(ckv_hbm.at[layer, pid], kbuf.at[sl, dst, :], sem.at[0, sl]),
                pltpu.make_async_copy(kpe_hbm.at[layer, pid], pbuf.at[sl, :, dst], sem.at[1, sl]))

    def issue(bb, cc, sl):
        for p in range(ppc):
            for prio, cp in enumerate(copies(pt_ref[bb, cc * ppc + p], sl, p)):
                cp.start(priority=prio)

    def wait_all(sl):
        for p in range(ppc):
            for cp in copies(0, sl, p):
                cp.wait()

    @pl.when(g == 0)
    def _():
        issue(0, 0, 0)

    @pl.when(c == 0)
    def _():
        m_scr[...] = jnp.full_like(m_scr, -jnp.inf)
        l_scr[...] = jnp.zeros_like(l_scr)
        acc_scr[...] = jnp.zeros_like(acc_scr)

    wait_all(slot)
    nxt = jnp.minimum(g + 1, total - 1)
    issue(nxt // nch, nxt % nch, 1 - slot)

    ql = ql_ref[0]
    qp = qp_ref[0]
    kv = kbuf[slot].astype(BF16)
    pe = pbuf[slot].astype(BF16)
    s = (lax.dot_general(ql, kv, (((1,), (1,)), ((), ())), preferred_element_type=F32)
         + jnp.dot(qp, pe, preferred_element_type=F32)) * MLA_SCALE
    m_old = m_scr[...]
    m_new = jnp.maximum(m_old, jnp.max(s, axis=-1, keepdims=True))
    alpha = jnp.exp(m_old - m_new)
    p = jnp.exp(s - m_new)
    l_new = alpha * l_scr[...] + jnp.sum(p, axis=-1, keepdims=True)
    acc_new = alpha * acc_scr[...] + jnp.dot(p.astype(BF16), kv, preferred_element_type=F32)
    m_scr[...] = m_new
    l_scr[...] = l_new
    acc_scr[...] = acc_new

    @pl.when(c == nch - 1)
    def _():
        cn = cn_ref[0].astype(BF16).astype(F32)
        pn = pn_ref[0].astype(BF16).astype(F32)
        s_n = (jnp.sum(ql.astype(F32) * cn, axis=-1, keepdims=True)
               + jnp.sum(qp.astype(F32) * pn, axis=-1, keepdims=True)) * MLA_SCALE
        m_f = jnp.maximum(m_new, s_n)
        a_f = jnp.exp(m_new - m_f)
        p_n = jnp.exp(s_n - m_f)
        l_f = a_f * l_new + p_n
        acc_f = a_f * acc_new + p_n.astype(BF16).astype(F32) * cn
        o_ref[0] = acc_f / l_f

    @pl.when(g == total - 1)
    def _():
        wait_all(1 - slot)


def _mla_paged(page_table, qlat, qpe, ckv_new, kpe_new, cache_ckv, cache_kpe, *, layer, ppc):
    b, hp, _ = qlat.shape
    n_pages = page_table.shape[1]
    page = cache_ckv.shape[2]
    nch = n_pages // ppc
    grid_spec = pltpu.PrefetchScalarGridSpec(
        num_scalar_prefetch=1,
        grid=(b, nch),
        in_specs=[pl.BlockSpec((1, hp, C_KV_LORA), lambda i, j, pt: (i, 0, 0)),
                  pl.BlockSpec((1, hp, C_ROPE), lambda i, j, pt: (i, 0, 0)),
                  pl.BlockSpec((1, 1, C_KV_LORA), lambda i, j, pt: (i, 0, 0)),
                  pl.BlockSpec((1, 1, C_ROPE), lambda i, j, pt: (i, 0, 0)),
                  pl.BlockSpec(memory_space=pl.ANY),
                  pl.BlockSpec(memory_space=pl.ANY)],
        out_specs=pl.BlockSpec((1, hp, C_KV_LORA), lambda i, j, pt: (i, 0, 0)),
        scratch_shapes=[pltpu.VMEM((2, ppc * page, C_KV_LORA), F32),
                        pltpu.VMEM((2, C_ROPE, ppc * page), F32),
                        pltpu.SemaphoreType.DMA((2, 2)),
                        pltpu.VMEM((hp, 1), F32), pltpu.VMEM((hp, 1), F32),
                        pltpu.VMEM((hp, C_KV_LORA), F32)],
    )
    return pl.pallas_call(
        functools.partial(_mla_paged_kernel, layer=layer, ppc=ppc, nch=nch, page=page),
        grid_spec=grid_spec,
        out_shape=jax.ShapeDtypeStruct((b, hp, C_KV_LORA), F32),
        compiler_params=_cparams(("arbitrary", "arbitrary")),
        name="mla_paged",
    )(page_table, qlat, qpe, ckv_new, kpe_new, cache_ckv, cache_kpe)


def _xattn_prompt_kernel(x_ref, oa_ref, ob_ref, oc_ref, wa_ref, wb_ref, wc_ref,
                         nw_ref, wq_ref, wo_ref, mk_ref, mv_ref, o_ref):
    x = x_ref[0]
    for a_ref, w_ref in ((oa_ref, wa_ref), (ob_ref, wb_ref), (oc_ref, wc_ref)):
        x = x + jnp.dot(a_ref[0].astype(BF16), w_ref[...], preferred_element_type=F32)
    xn = _rms(x, nw_ref[...]).astype(BF16)
    q = jnp.dot(xn, wq_ref[...], preferred_element_type=F32).astype(BF16)
    heads = []
    for h in range(X_HEADS):
        sl = slice(h * X_HDIM, (h + 1) * X_HDIM)
        s = lax.dot_general(q[:, sl], mk_ref[0, :, sl], (((1,), (1,)), ((), ())),
                            preferred_element_type=F32) * (X_HDIM ** -0.5)
        e = jnp.exp(s - jnp.max(s, axis=-1, keepdims=True))
        p = (e / jnp.sum(e, axis=-1, keepdims=True)).astype(BF16)
        heads.append(jnp.dot(p, mv_ref[0, :, sl], preferred_element_type=F32).astype(BF16))
    o_ref[0] = x + jnp.dot(jnp.concatenate(heads, axis=-1), wo_ref[...], preferred_element_type=F32)


def _xattn_prompt(x, o_list, w_out_list, nw, wq, wo, mk, mv, *, tm):
    b, t, d = x.shape
    n_mem = mk.shape[1]
    return pl.pallas_call(
        _xattn_prompt_kernel,
        grid=(b, t // tm),
        in_specs=[pl.BlockSpec((1, tm, d), lambda i, j: (i, j, 0))]
                 + [pl.BlockSpec((1, tm, o.shape[2]), lambda i, j: (i, j, 0)) for o in o_list]
                 + [_w_spec(w) for w in w_out_list]
                 + [_const_spec((1, d)), _w_spec(wq), _w_spec(wo),
                  pl.BlockSpec((1, n_mem, d), lambda i, j: (i, 0, 0)),
                  pl.BlockSpec((1, n_mem, d), lambda i, j: (i, 0, 0))],
        out_specs=pl.BlockSpec((1, tm, d), lambda i, j: (i, j, 0)),
        out_shape=jax.ShapeDtypeStruct((b, t, d), F32),
        compiler_params=_cparams(("parallel", "parallel")),
        name="xattn_prompt",
    )(x, *o_list, *[_w_arr(w) for w in w_out_list], nw, _w_arr(wq), _w_arr(wo), mk, mv)


def _xattn_decode_kernel(q_ref, mk_ref, mv_ref, o_ref):
    for bi in range(q_ref.shape[0]):
        _xattn_decode_one(q_ref, mk_ref, mv_ref, o_ref, bi)


def _xattn_decode_one(q_ref, mk_ref, mv_ref, o_ref, bi):
    nt = X_HDIM // LANE
    grp = nt * X_HEADS
    n_rows = mk_ref.shape[2]
    qrow = q_ref[bi]
    qm = jnp.concatenate([qrow[:, h * X_HDIM + t * LANE:h * X_HDIM + (t + 1) * LANE]
                          for t in range(nt) for h in range(X_HEADS)], axis=0)
    sel = _iota((grp, n_rows), 0) == _iota((grp, n_rows), 1) % grp
    s_all = _bdot_nt(qm, mk_ref[0, bi])
    part = jnp.sum(jnp.where(sel, s_all, 0.0), axis=0, keepdims=True)
    lane = _iota((1, n_rows), 1)
    s = part
    for t in range(1, nt):
        s = s + jnp.where(lane % grp < X_HEADS, pltpu.roll(part, n_rows - t * X_HEADS, axis=1),
                          pltpu.roll(part, t * X_HEADS, axis=1))
    s = s * (X_HDIM ** -0.5)
    p = jnp.zeros_like(s)
    for h in range(X_HEADS):
        mine = lane % X_HEADS == h
        e = jnp.exp(s - jnp.max(jnp.where(mine, s, -jnp.inf), axis=-1, keepdims=True))
        den = jnp.sum(jnp.where(lane % grp == h, e, 0.0), axis=-1, keepdims=True)
        p = jnp.where(mine, e / den, p)
    o = _bdot(jnp.where(sel, p, 0.0), mv_ref[0, bi])
    o_ref[bi] = jnp.concatenate([o[t * X_HEADS + h:t * X_HEADS + h + 1]
                                for h in range(X_HEADS) for t in range(nt)], axis=-1)


def _mem_rows(mem):
    dep, b, n_mem, nh, hd = mem.shape
    nt = hd // LANE
    return mem.reshape(dep, b, n_mem, nh, nt, LANE).transpose(0, 1, 2, 4, 3, 5).reshape(dep, b, n_mem * nt * nh, LANE)


def _xattn_decode(q, mk_rows, mv_rows, *, layer, nb):
    _, b, n_rows, _ = mk_rows.shape
    d = X_HEADS * X_HDIM
    assert X_HDIM == 2 * LANE
    mem_spec = pl.BlockSpec((1, nb, n_rows, LANE), lambda i: (layer, i, 0, 0))
    return pl.pallas_call(
        _xattn_decode_kernel,
        grid=(b // nb,),
        in_specs=[pl.BlockSpec((nb, 1, d), lambda i: (i, 0, 0)), mem_spec, mem_spec],
        out_specs=pl.BlockSpec((nb, 1, d), lambda i: (i, 0, 0)),
        out_shape=jax.ShapeDtypeStruct((b, 1, d), F32),
        compiler_params=_cparams(("parallel",)),
        name="xattn_decode",
    )(q, mk_rows, mv_rows)


def _columns(rows):
    w = rows[0].shape[1]
    assert len(rows) <= LANE
    pad = [jnp.zeros((LANE - len(rows), w), F32)] if len(rows) < LANE else []
    return jnp.concatenate(rows + pad, axis=0).T


def _gdn_decode_kernel(u_ref, cs_ref, s_ref, cw_ref, alog_ref, dtb_ref, anorm_ref,
                       o_ref, cs_out_ref, s_out_ref):
    seqs = range(u_ref.shape[0])
    heads = range(A_HEADS)
    items = [(bi, h) for bi in seqs for h in heads]
    cw = cw_ref[...]
    x = [u_ref[bi, :, 0:A_CONV_CH] for bi in seqs]
    z = [u_ref[bi, :, A_CONV_CH:A_CONV_CH + A_V] for bi in seqs]
    ba = [u_ref[bi, :, A_CONV_CH + A_V:UA_W] for bi in seqs]
    cs = [cs_ref[0, bi] for bi in seqs]
    qkv = [_silu(cw[0:1] * cs[bi][0:1] + cw[1:2] * cs[bi][1:2] + cw[2:3] * cs[bi][2:3] + cw[3:4] * x[bi])
           for bi in seqs]
    for bi in seqs:
        cs_out_ref[bi] = jnp.concatenate([cs[bi][1:CONV_W - 1], x[bi]], axis=0)
    beta = [jax.nn.sigmoid(b) for b in ba]
    eg_all = [jnp.exp(-jnp.exp(alog_ref[...]) * jax.nn.softplus(b + dtb_ref[...])) for b in ba]
    q_raw = [qkv[bi][:, h * A_DK:(h + 1) * A_DK] for bi, h in items]
    k_raw = [qkv[bi][:, A_QK + h * A_DK:A_QK + (h + 1) * A_DK] for bi, h in items]
    v = [qkv[bi][:, 2 * A_QK + h * A_DV:2 * A_QK + (h + 1) * A_DV] for bi, h in items]
    q = [a * lax.rsqrt(jnp.sum(a * a, axis=-1, keepdims=True) + EPS) * (A_DK ** -0.5) for a in q_raw]
    k = [a * lax.rsqrt(jnp.sum(a * a, axis=-1, keepdims=True) + EPS) for a in k_raw]
    cols = _columns(k + q)
    kcol = [cols[:, n:n + 1] for n in range(len(items))]
    qcol = [cols[:, len(items) + n:len(items) + n + 1] for n in range(len(items))]
    s = [s_ref[0, bi, h] for bi, h in items]
    bh = [beta[bi][:, h:h + 1] for bi, h in items]
    eg = [eg_all[bi][:, A_HEADS + h:A_HEADS + h + 1] for bi, h in items]
    ks_row = [jnp.sum(kcol[n] * s[n], axis=0, keepdims=True) for n in range(len(items))]
    qs_row = [jnp.sum(qcol[n] * s[n], axis=0, keepdims=True) for n in range(len(items))]
    r = [v[n] * bh[n] - (bh[n] * eg[n]) * ks_row[n] for n in range(len(items))]
    qk = [jnp.sum(q[n] * k[n], axis=-1, keepdims=True) for n in range(len(items))]
    o = [eg[n] * qs_row[n] + qk[n] * r[n] for n in range(len(items))]
    for n, (bi, h) in enumerate(items):
        s_out_ref[bi, h] = eg[n] * s[n] + kcol[n] * r[n]
    o = [a * lax.rsqrt(jnp.mean(a * a, axis=-1, keepdims=True) + EPS) * anorm_ref[...] for a in o]
    for bi in seqs:
        o_ref[bi] = jnp.concatenate(o[bi * A_HEADS:(bi + 1) * A_HEADS], axis=-1) * _silu(z[bi])


def _gdn_decode(u_a, cs, s, cw, alog, dtb, anorm, *, layer, nb):
    b = u_a.shape[0]
    return pl.pallas_call(
        _gdn_decode_kernel,
        grid=(b // nb,),
        in_specs=[pl.BlockSpec((nb, 1, UA_W), lambda i: (i, 0, 0)),
                  pl.BlockSpec((1, nb, CONV_W - 1, A_CONV_CH), lambda i: (layer, i, 0, 0)),
                  pl.BlockSpec((1, nb, A_HEADS, A_DK, A_DV), lambda i: (layer, i, 0, 0, 0)),
                  _const_spec((CONV_W, A_CONV_CH)), _const_spec((1, LANE)), _const_spec((1, LANE)),
                  _const_spec((1, A_DV))],
        out_specs=[pl.BlockSpec((nb, 1, A_V), lambda i: (i, 0, 0)),
                   pl.BlockSpec((nb, CONV_W - 1, A_CONV_CH), lambda i: (i, 0, 0)),
                   pl.BlockSpec((nb, A_HEADS, A_DK, A_DV), lambda i: (i, 0, 0, 0))],
        out_shape=[jax.ShapeDtypeStruct((b, 1, A_V), F32),
                   jax.ShapeDtypeStruct((b, CONV_W - 1, A_CONV_CH), F32),
                   jax.ShapeDtypeStruct((b, A_HEADS, A_DK, A_DV), F32)],
        compiler_params=_cparams(("parallel",)),
        name="gdn_decode",
    )(u_a, cs, s, cw, alog, dtb, anorm)


def _gla_decode_kernel(u_ref, s_ref, w2_ref, gb_ref, bnorm_ref, o_ref, s_out_ref):
    nb = u_ref.shape[0]
    seqs = range(nb)
    items = [(bi, h) for bi in seqs for h in range(B_HEADS)]
    u = [u_ref[bi] for bi in seqs]
    q = [a[:, 0:B_QK] * (B_DK ** -0.5) for a in u]
    k = [a[:, B_QK:2 * B_QK] for a in u]
    rg = [a[:, 2 * B_QK + B_V:2 * B_QK + 2 * B_V] for a in u]
    pad = [jnp.zeros(((-nb) % 8, LANE), F32)] if nb % 8 else []
    gate = _bdot(jnp.concatenate([a[:, 2 * B_QK + 2 * B_V:] for a in u] + pad, axis=0), w2_ref[...])
    dec = [jnp.exp(jax.nn.log_sigmoid(gate[bi:bi + 1] + gb_ref[...]) / B_GATE_TAU) for bi in seqs]
    cols = _columns(k + [q[bi] * dec[bi] for bi in seqs] + dec)
    v = [u[bi][:, 2 * B_QK + h * B_DV:2 * B_QK + (h + 1) * B_DV] for bi, h in items]
    s = [s_ref[0, bi, h] for bi, h in items]
    kcol = [cols[h * B_DK:(h + 1) * B_DK, bi:bi + 1] for bi, h in items]
    qacol = [cols[h * B_DK:(h + 1) * B_DK, nb + bi:nb + bi + 1] for bi, h in items]
    acol = [cols[h * B_DK:(h + 1) * B_DK, 2 * nb + bi:2 * nb + bi + 1] for bi, h in items]
    qk = [jnp.sum(q[bi][:, h * B_DK:(h + 1) * B_DK] * k[bi][:, h * B_DK:(h + 1) * B_DK], axis=-1, keepdims=True)
          for bi, h in items]
    o = [jnp.sum(qacol[n] * s[n], axis=0, keepdims=True) + qk[n] * v[n] for n in range(len(items))]
    for n, (bi, h) in enumerate(items):
        s_out_ref[bi, h] = acol[n] * s[n] + kcol[n] * v[n]
    o = [a * lax.rsqrt(jnp.mean(a * a, axis=-1, keepdims=True) + EPS) * bnorm_ref[...] for a in o]
    for bi in seqs:
        o_ref[bi] = jnp.concatenate(o[bi * B_HEADS:(bi + 1) * B_HEADS], axis=-1) * _silu(rg[bi])


def _gla_decode(u_b, s, w2, gb, bnorm, *, layer, nb):
    b = u_b.shape[0]
    return pl.pallas_call(
        _gla_decode_kernel,
        grid=(b // nb,),
        in_specs=[pl.BlockSpec((nb, 1, UB_W), lambda i: (i, 0, 0)),
                  pl.BlockSpec((1, nb, B_HEADS, B_DK, B_DV), lambda i: (layer, i, 0, 0, 0)),
                  _const_spec((LANE, B_QK)), _const_spec((1, B_QK)), _const_spec((1, B_DV))],
        out_specs=[pl.BlockSpec((nb, 1, B_V), lambda i: (i, 0, 0)),
                   pl.BlockSpec((nb, B_HEADS, B_DK, B_DV), lambda i: (i, 0, 0, 0))],
        out_shape=[jax.ShapeDtypeStruct((b, 1, B_V), F32),
                   jax.ShapeDtypeStruct((b, B_HEADS, B_DK, B_DV), F32)],
        compiler_params=_cparams(("parallel",)),
        name="gla_decode",
    )(u_b, s, w2, gb, bnorm)


def _pad_cols(w, n):
    return jnp.pad(w, ((0, 0), (0, n - w.shape[1])))


def _rope_tables(pos):
    half = C_ROPE // 2
    inv = ROPE_THETA ** (-jnp.arange(half, dtype=F32) / half)
    ang = pos.astype(F32)[:, None] * inv[None, :]
    cos, sin = jnp.cos(ang), jnp.sin(ang)
    zero = jnp.zeros_like(cos)
    ck = jnp.concatenate([cos, cos, zero, zero], axis=-1)
    sk = jnp.concatenate([-sin, sin, zero, zero], axis=-1)
    cq = jnp.tile(jnp.concatenate([cos, cos], axis=-1), (1, C_HEADS))
    sq = jnp.tile(jnp.concatenate([-sin, sin], axis=-1), (1, C_HEADS))
    return ck, sk, cq, sq


STACKED_BF16 = ('w_ffn1_gate', 'w_ffn1_up', 'w_ffn1_down', 'w_ffn2_gate', 'w_ffn2_up', 'w_ffn2_down',
                'w_xq', 'w_xo', 'w_xk', 'w_xv')


def _layer_weights(l, P, Pb):
    w_in = P['w_in'][l]
    o_b, o_c = A_IN, A_IN + B_IN
    w_a = jnp.concatenate([w_in[:, :A_CONV_CH + A_V], _pad_cols(w_in[:, A_CONV_CH + A_V:A_IN], LANE)], axis=1)
    w_b = jnp.concatenate([w_in[:, o_b:o_b + 2 * B_QK + 2 * B_V],
                           _pad_cols(w_in[:, o_b + 2 * B_QK + 2 * B_V:o_c], LANE)], axis=1)
    w_c = jnp.concatenate([w_in[:, o_c:o_c + C_Q_LORA + C_KV_LORA],
                           _pad_cols(w_in[:, o_c + C_Q_LORA + C_KV_LORA:], LANE)], axis=1)
    wuq = P['c_w_uq'][l].reshape(C_Q_LORA, C_HEADS, C_NOPE + C_ROPE)
    wuq = jnp.concatenate([wuq[:, :, :C_NOPE].reshape(C_Q_LORA, -1),
                           wuq[:, :, C_NOPE:].reshape(C_Q_LORA, -1)], axis=1)
    w_out = P['w_out'][l]
    row = lambda a: a.reshape(1, -1).astype(F32)
    head_lanes = lambda a: jnp.pad(a.reshape(1, -1).astype(F32), ((0, 0), (A_HEADS, LANE - 2 * A_HEADS)))
    return dict(
        norm_ffn1=row(P['norm_ffn1'][l]), norm_ffn2=row(P['norm_ffn2'][l]),
        ffn1=tuple((Pb[n], l) for n in ('w_ffn1_gate', 'w_ffn1_up', 'w_ffn1_down')),
        ffn2=tuple((Pb[n], l) for n in ('w_ffn2_gate', 'w_ffn2_up', 'w_ffn2_down')),
        norm_mix=row(P['norm_mix'][l]),
        w_in=(w_a.astype(BF16), w_b.astype(BF16), w_c.astype(BF16)),
        a_conv_w=P['a_conv_w'][l].astype(F32),
        a_log=head_lanes(P['a_log'][l]), a_dt_bias=head_lanes(P['a_dt_bias'][l]),
        a_norm=row(P['a_norm'][l]), a_norm_t=row(jnp.tile(P['a_norm'][l], A_HEADS)),
        b_w2=jnp.pad(P['b_gate_w2'][l], ((0, LANE - B_GATE_RANK), (0, 0))).astype(BF16),
        b_gate_bias=row(P['b_gate_bias'][l]),
        b_norm=row(P['b_norm'][l]), b_norm_t=row(jnp.tile(P['b_norm'][l], B_HEADS)),
        c_q_norm=row(P['c_q_norm'][l]), c_kv_norm=row(P['c_kv_norm'][l]),
        c_w_uq=wuq.astype(BF16),
        c_w_uk=jnp.transpose(P['c_w_uk'][l], (1, 2, 0)).astype(BF16),
        c_w_uv=jnp.transpose(P['c_w_uv'][l], (1, 0, 2)).astype(BF16),
        w_out=(w_out[:A_V].astype(BF16), w_out[A_V:A_V + B_V].astype(BF16), w_out[A_V + B_V:].astype(BF16)),
        norm_x=row(P['norm_x'][l]), norm_mem=row(P['norm_mem'][l]),
        w_xq=(Pb['w_xq'], l), w_xo=(Pb['w_xo'], l), w_xk=(Pb['w_xk'], l), w_xv=(Pb['w_xv'], l),
    )


def _pick(n, pref):
    return pref if n % pref == 0 else n


def kernel(x_prompt, x_sample, mem_prompt, cache_ckv, cache_kpe, page_table, state_conv_a, state_delta, state_gla, cache_mem_k, cache_mem_v, norm_ffn1, w_ffn1_gate, w_ffn1_up, w_ffn1_down, norm_mix, w_in, a_conv_w, a_log, a_dt_bias, a_norm, b_gate_w2, b_gate_bias, b_norm, c_q_norm, c_w_uq, c_kv_norm, c_w_uk, c_w_uv, w_out, norm_x, norm_mem, w_xq, w_xk, w_xv, w_xo, norm_ffn2, w_ffn2_gate, w_ffn2_up, w_ffn2_down, final_norm):
    P = dict(norm_ffn1=norm_ffn1, w_ffn1_gate=w_ffn1_gate, w_ffn1_up=w_ffn1_up, w_ffn1_down=w_ffn1_down,
             norm_mix=norm_mix, w_in=w_in, a_conv_w=a_conv_w, a_log=a_log, a_dt_bias=a_dt_bias,
             a_norm=a_norm, b_gate_w2=b_gate_w2, b_gate_bias=b_gate_bias, b_norm=b_norm,
             c_q_norm=c_q_norm, c_w_uq=c_w_uq, c_kv_norm=c_kv_norm, c_w_uk=c_w_uk, c_w_uv=c_w_uv,
             w_out=w_out, norm_x=norm_x, norm_mem=norm_mem, w_xq=w_xq, w_xk=w_xk, w_xv=w_xv, w_xo=w_xo,
             norm_ffn2=norm_ffn2, w_ffn2_gate=w_ffn2_gate, w_ffn2_up=w_ffn2_up, w_ffn2_down=w_ffn2_down)
    depth = w_in.shape[0]
    Pb = {n: P[n].astype(BF16) for n in STACKED_BF16}
    W = [_layer_weights(l, P, Pb) for l in range(depth)]
    fnorm = final_norm.reshape(1, -1).astype(F32)

    bp, tp, d = x_prompt.shape
    mp = bp * tp
    n_mem = mem_prompt.shape[1]
    tm = _pick(tp, 512)
    tabs_p = _rope_tables(jnp.arange(tp, dtype=jnp.int32))
    x = x_prompt.reshape(mp, d)
    mem = mem_prompt.reshape(bp * n_mem, d)
    p_ckv, p_kpe, p_conv, p_delta, p_gla, p_mk, p_mv = [], [], [], [], [], [], []
    for l in range(depth):
        w = W[l]
        mk, mv = _norm_mm(mem, w['norm_mem'], [w['w_xk'], w['w_xv']], tm=_pick(bp * n_mem, 512))
        x = _ffn(x, w['norm_ffn1'], *w['ffn1'], fnorm, final=False, tm=tm)
        x3 = x.reshape(bp, tp, d)
        w_a, w_b, w_c = w['w_in']
        o_a, conv_new, sd = _gdn_prompt(x3, w['norm_mix'], w_a, w['a_conv_w'], w['a_log'], w['a_dt_bias'],
                                        w['a_norm_t'], tblk=_pick(tp, 256), nb=_pick(bp, 2))
        o_b, sg = _gla_prompt(x3, w['norm_mix'], w_b, w['b_w2'], w['b_gate_bias'], w['b_norm_t'],
                              tblk=_pick(tp, 512))
        ckv, kpe, ckvb, kpeb, qlat, qpe = _mla_prep(x3, w['norm_mix'], w_c, w['c_q_norm'], w['c_kv_norm'],
                                                    w['c_w_uq'], w['c_w_uk'], tabs_p, tm=tm)
        o_c = _mla_flash(qlat, qpe, ckvb, kpeb, w['c_w_uv'], tq=_pick(tp, 512), tk=_pick(tp, 512))
        x = _xattn_prompt(x.reshape(bp, tp, d), [o_a, o_b, o_c], list(w['w_out']),
                          w['norm_x'], w['w_xq'], w['w_xo'],
                          mk.reshape(bp, n_mem, d).astype(BF16), mv.reshape(bp, n_mem, d).astype(BF16),
                          tm=tm).reshape(mp, d)
        x = _ffn(x, w['norm_ffn2'], *w['ffn2'], fnorm, final=(l == depth - 1), tm=tm)
        p_ckv.append(ckv)
        p_kpe.append(kpe)
        p_conv.append(conv_new)
        p_delta.append(sd.reshape(bp, A_DK, A_HEADS, A_DV).transpose(0, 2, 1, 3))
        sg = sg.reshape(bp, B_HEADS, B_DV, B_HEADS, B_DK)
        p_gla.append(jnp.stack([sg[:, h, :, h, :] for h in range(B_HEADS)], axis=1).transpose(0, 1, 3, 2))
        p_mk.append(mk.reshape(bp, n_mem, X_HEADS, X_HDIM))
        p_mv.append(mv.reshape(bp, n_mem, X_HEADS, X_HDIM))
    y_prompt = x.reshape(bp, tp, d)

    bs, ts, _ = x_sample.shape
    n_pages, page = page_table.shape[1], cache_ckv.shape[2]
    past_len = n_pages * page
    tabs_s = _rope_tables(past_len + jnp.arange(ts, dtype=jnp.int32))
    cache_kpe_t = jnp.swapaxes(cache_kpe, 2, 3)
    mem_k_rows, mem_v_rows = _mem_rows(cache_mem_k), _mem_rows(cache_mem_v)
    nb_dec, nb_mem = _pick(bs, 8), _pick(bs, 4)
    x = x_sample.reshape(bs, d)
    s_ckv, s_kpe, s_conv, s_delta, s_gla = [], [], [], [], []
    for l in range(depth):
        w = W[l]
        x = _ffn(x, w['norm_ffn1'], *w['ffn1'], fnorm, final=False, tm=bs)
        w_a, w_b, w_c = w['w_in']
        u_a, u_b = _norm_mm(x, w['norm_mix'], [w_a, w_b], tm=bs)
        o_a, conv_new, sd = _gdn_decode(u_a.reshape(bs, 1, UA_W), state_conv_a, state_delta,
                                        w['a_conv_w'], w['a_log'], w['a_dt_bias'], w['a_norm'], layer=l,
                                        nb=nb_dec)
        o_b, sg = _gla_decode(u_b.reshape(bs, 1, UB_W), state_gla, w['b_w2'], w['b_gate_bias'], w['b_norm'],
                              layer=l, nb=nb_dec)
        ckv, kpe, _, _, qlat, qpe = _mla_prep(x.reshape(1, bs, d), w['norm_mix'], w_c, w['c_q_norm'],
                                              w['c_kv_norm'], w['c_w_uq'], w['c_w_uk'], tabs_s, tm=bs)
        pad_heads = lambda a: jnp.pad(a[0].transpose(1, 0, 2), ((0, 0), (0, 8 - C_HEADS), (0, 0)))
        olat = _mla_paged(page_table, pad_heads(qlat), pad_heads(qpe), ckv.reshape(bs, 1, C_KV_LORA),
                          kpe.reshape(bs, 1, C_ROPE), cache_ckv, cache_kpe_t, layer=l, ppc=_pick(n_pages, 64))
        o_c = _mla_oproj(olat[:, :C_HEADS].transpose(1, 0, 2)[None], w['c_w_uv'], tm=bs)
        x = _mm_res(x, [o_a.reshape(bs, A_V), o_b.reshape(bs, B_V), o_c.reshape(bs, C_HEADS * C_VDIM)],
                    list(w['w_out']), tm=bs)
        (q,) = _norm_mm(x, w['norm_x'], [w['w_xq']], tm=bs)
        att = _xattn_decode(q.reshape(bs, 1, d), mem_k_rows, mem_v_rows, layer=l, nb=nb_mem)
        x = _mm_res(x, [att.reshape(bs, d)], [w['w_xo']], tm=bs)
        x = _ffn(x, w['norm_ffn2'], *w['ffn2'], fnorm, final=(l == depth - 1), tm=bs)
        s_ckv.append(ckv.reshape(bs, ts, C_KV_LORA))
        s_kpe.append(kpe.reshape(bs, ts, C_ROPE))
        s_conv.append(conv_new)
        s_delta.append(sd)
        s_gla.append(sg)
    y_sample = x.reshape(bs, ts, d)

    return (y_prompt, y_sample,
            jnp.stack(p_ckv), jnp.stack(p_kpe), jnp.stack(p_conv), jnp.stack(p_delta), jnp.stack(p_gla),
            jnp.stack(p_mk), jnp.stack(p_mv),
            jnp.stack(s_ckv), jnp.stack(s_kpe), jnp.stack(s_conv), jnp.stack(s_delta), jnp.stack(s_gla))
```

```python
import functools

import jax
import jax.numpy as jnp
import numpy as np
from jax import lax
from jax.experimental import pallas as pl
from jax.experimental.pallas import tpu as pltpu

F32 = jnp.float32
BF16 = jnp.bfloat16

D_MODEL = 1024
A_HEADS, A_DK, A_DV, CONV_W = 4, 64, 64, 4
B_HEADS, B_DK, B_DV, B_GATE_RANK, B_GATE_TAU = 4, 32, 64, 16, 16.0
C_HEADS, C_NOPE, C_ROPE, C_VDIM, C_Q_LORA, C_KV_LORA = 4, 128, 64, 128, 384, 256
ROPE_THETA = 10000.0
X_HEADS, X_HDIM = 4, 256
D_FF = 2816
CHUNK = 64
EPS = 1e-6

A_QK = A_HEADS * A_DK
A_V = A_HEADS * A_DV
A_CONV_CH = 2 * A_QK + A_V
A_IN = A_CONV_CH + A_V + 2 * A_HEADS
B_QK = B_HEADS * B_DK
B_V = B_HEADS * B_DV
B_IN = 2 * B_QK + 2 * B_V + B_GATE_RANK
C_IN = C_Q_LORA + C_KV_LORA + C_ROPE
MLA_SCALE = (C_NOPE + C_ROPE) ** -0.5

LANE = 128
UA_W = A_CONV_CH + A_V + LANE
UB_W = 2 * B_QK + 2 * B_V + LANE
UC_W = C_Q_LORA + C_KV_LORA + LANE
SUB = 16
LOG2E = 1.4426950408889634
PAGED_SLOTS = 3
VMEM_LIMIT = 56 * 1024 * 1024


def _cparams(sem):
    return pltpu.CompilerParams(dimension_semantics=sem, vmem_limit_bytes=VMEM_LIMIT)


def _const_spec(shape):
    nd = len(shape)
    return pl.BlockSpec(shape, lambda *_: (0,) * nd, pipeline_mode=pl.Buffered(1))


def _w_arr(w):
    return w[0] if isinstance(w, tuple) else w


def _w_shape(w):
    return w[0].shape[1:] if isinstance(w, tuple) else w.shape


def _w_spec(w):
    if not isinstance(w, tuple):
        return _const_spec(w.shape)
    arr, layer = w
    return pl.BlockSpec((None,) + arr.shape[1:], lambda *_: (layer,) + (0,) * (arr.ndim - 1),
                        pipeline_mode=pl.Buffered(1))


def _rms(x, w):
    return x * lax.rsqrt(jnp.mean(x * x, axis=-1, keepdims=True) + EPS) * w


def _silu(x):
    return x * jax.nn.sigmoid(x)


def _bdot(a, b):
    return jnp.dot(a.astype(BF16), b.astype(BF16), preferred_element_type=F32)


def _bdot_nt(a, b):
    return lax.dot_general(a.astype(BF16), b.astype(BF16), (((1,), (1,)), ((), ())),
                           preferred_element_type=F32)


def _bdot_tn(a, b):
    return lax.dot_general(a.astype(BF16), b.astype(BF16), (((0,), (0,)), ((), ())),
                           preferred_element_type=F32)


def _iota(shape, dim):
    return lax.broadcasted_iota(jnp.int32, shape, dim)


def _block_mask(rows, cols, rb, cb):
    return (_iota((rows, cols), 0) // rb) == (_iota((rows, cols), 1) // cb)


def _block_diag(x, mask):
    n = mask.shape[0] // x.shape[0]
    xb = x.astype(BF16)
    return jnp.where(mask, jnp.concatenate([xb] * n, axis=0), jnp.zeros_like(xb[:1, :1]))


def _diag_blocks(m, rb, cb):
    n = m.shape[0] // rb
    lane_blk = _iota((rb, m.shape[1]), 1) // cb
    out = jnp.zeros((rb, m.shape[1]), m.dtype)
    for h in range(n):
        out = jnp.where(lane_blk == h, m[h * rb:(h + 1) * rb, :], out)
    return out


def _seg_sum(x, ones_bd):
    hi = x.astype(BF16)
    lo = (x - hi.astype(F32)).astype(BF16)
    return (jnp.dot(hi, ones_bd, preferred_element_type=F32)
            + jnp.dot(lo, ones_bd, preferred_element_type=F32))


def _expand_heads(x, off, nh, w):
    lane_blk = _iota((x.shape[0], nh * w), 1) // w
    out = jnp.zeros((x.shape[0], nh * w), x.dtype)
    for h in range(nh):
        out = jnp.where(lane_blk == h, x[:, off + h:off + h + 1], out)
    return out


def _cumsum_rows(x):
    n = x.shape[0]
    row = _iota(x.shape, 0)
    s = 1
    while s < n:
        x = x + jnp.where(row >= s, pltpu.roll(x, s, axis=0), 0.0)
        s *= 2
    return x


def _ffn_kernel(x_ref, nw_ref, wg_ref, wu_ref, wd_ref, fn_ref, o_ref, *, final):
    x = x_ref[...]
    xn = _rms(x, nw_ref[...]).astype(BF16)
    g = jnp.dot(xn, wg_ref[...], preferred_element_type=F32)
    u = jnp.dot(xn, wu_ref[...], preferred_element_type=F32)
    h = (_silu(g) * u).astype(BF16)
    y = x + 0.5 * jnp.dot(h, wd_ref[...], preferred_element_type=F32)
    if final:
        y = _rms(y, fn_ref[...])
    o_ref[...] = y


def _ffn(x, nw, wg, wu, wd, fn, *, final, tm):
    m = x.shape[0]
    return pl.pallas_call(
        functools.partial(_ffn_kernel, final=final),
        grid=(m // tm,),
        in_specs=[pl.BlockSpec((tm, D_MODEL), lambda i: (i, 0)),
                  _const_spec((1, D_MODEL)),
                  _w_spec(wg), _w_spec(wu), _w_spec(wd), _const_spec((1, D_MODEL))],
        out_specs=pl.BlockSpec((tm, D_MODEL), lambda i: (i, 0)),
        out_shape=jax.ShapeDtypeStruct((m, D_MODEL), F32),
        compiler_params=_cparams(("parallel",)),
        name="ffn",
    )(x, nw, _w_arr(wg), _w_arr(wu), _w_arr(wd), fn)


def _norm_mm_kernel(x_ref, nw_ref, *refs, n_w):
    xn = _rms(x_ref[...], nw_ref[...]).astype(BF16)
    for w_ref, o_ref in zip(refs[:n_w], refs[n_w:]):
        o_ref[...] = jnp.dot(xn, w_ref[...], preferred_element_type=F32)


def _norm_mm(x, nw, ws, *, tm):
    m, d = x.shape
    return pl.pallas_call(
        functools.partial(_norm_mm_kernel, n_w=len(ws)),
        grid=(m // tm,),
        in_specs=[pl.BlockSpec((tm, d), lambda i: (i, 0)), _const_spec((1, d))]
                 + [_w_spec(w) for w in ws],
        out_specs=[pl.BlockSpec((tm, _w_shape(w)[1]), lambda i: (i, 0)) for w in ws],
        out_shape=[jax.ShapeDtypeStruct((m, _w_shape(w)[1]), F32) for w in ws],
        compiler_params=_cparams(("parallel",)),
        name="norm_mm",
    )(x, nw, *[_w_arr(w) for w in ws])


def _mm_res_kernel(x_ref, *refs, n_a):
    acc = x_ref[...]
    for a_ref, w_ref in zip(refs[:n_a], refs[n_a:2 * n_a]):
        acc = acc + jnp.dot(a_ref[...].astype(BF16), w_ref[...], preferred_element_type=F32)
    refs[2 * n_a][...] = acc


def _mm_res(x, a_list, w_list, *, tm):
    m, d = x.shape
    n_a = len(a_list)
    return pl.pallas_call(
        functools.partial(_mm_res_kernel, n_a=n_a),
        grid=(m // tm,),
        in_specs=[pl.BlockSpec((tm, d), lambda i: (i, 0))]
                 + [pl.BlockSpec((tm, a.shape[1]), lambda i: (i, 0)) for a in a_list]
                 + [_w_spec(w) for w in w_list],
        out_specs=pl.BlockSpec((tm, d), lambda i: (i, 0)),
        out_shape=jax.ShapeDtypeStruct((m, d), F32),
        compiler_params=_cparams(("parallel",)),
        name="mm_res",
    )(x, *a_list, *[_w_arr(w) for w in w_list])


def _project(x, nw_ref, win_ref):
    return jnp.dot(_rms(x, nw_ref[...]).astype(BF16), win_ref[...], preferred_element_type=F32)


def _gdn_prompt_kernel(x_ref, nw_ref, win_ref, cw_ref, alog_ref, dtb_ref, anorm_ref, o_ref, conv_ref, s_out_ref,
                       xbuf, s_scr, *, tblk, nb):
    t = pl.program_id(1)
    hw = A_HEADS * A_DV
    c = CHUNK
    nc = tblk // c

    @pl.when(t == 0)
    def _():
        for bi in range(nb):
            xbuf[bi, 0:8, :] = jnp.zeros((8, A_CONV_CH), F32)
        s_scr[...] = jnp.zeros_like(s_scr)

    cw = cw_ref[...]
    ones_bd = _block_mask(hw, hw, A_DK, A_DK).astype(BF16)
    xs, zs, qs, ks, vs, betas, gs = [], [], [], [], [], [], []
    for bi in range(nb):
        u = _project(x_ref[bi], nw_ref, win_ref)
        x = u[:, 0:A_CONV_CH]
        xbuf[bi, 8:8 + tblk, :] = x
        y = (cw[0:1] * xbuf[bi, 5:5 + tblk, :] + cw[1:2] * xbuf[bi, 6:6 + tblk, :]
             + cw[2:3] * xbuf[bi, 7:7 + tblk, :] + cw[3:4] * x)
        xbuf[bi, 0:8, :] = x[tblk - 8:tblk, :]
        qkv = _silu(y)
        ba = u[:, A_CONV_CH + A_V:UA_W]
        q = qkv[:, 0:A_QK]
        k = qkv[:, A_QK:2 * A_QK]
        xs.append(x)
        zs.append(u[:, A_CONV_CH:A_CONV_CH + A_V])
        qs.append(q * lax.rsqrt(_seg_sum(q * q, ones_bd) + EPS) * (A_DK ** -0.5))
        ks.append(k * lax.rsqrt(_seg_sum(k * k, ones_bd) + EPS))
        vs.append(qkv[:, 2 * A_QK:])
        betas.append(_expand_heads(jax.nn.sigmoid(ba), 0, A_HEADS, A_DV))
        gs.append(_expand_heads(-jnp.exp(alog_ref[...]) * jax.nn.softplus(ba + dtb_ref[...]),
                                A_HEADS, A_HEADS, A_DV))

    bd = _block_mask(hw, hw, c, c)
    row = _iota((c, hw), 0)
    col = _iota((c, hw), 1) % c
    incl = row >= col
    strict = row > col
    items = [(ci, bi) for ci in range(nc) for bi in range(nb)]
    n_it = range(len(items))

    def chunks(arrs):
        return [arrs[bi][ci * c:(ci + 1) * c] for ci, bi in items]

    qc, kc, vc, bc = chunks(qs), chunks(ks), chunks(vs), chunks(betas)
    gcum = [_cumsum_rows(gi) for gi in chunks(gs)]
    grow = [jnp.sum(jnp.where(row == col, gi, 0.0), axis=0, keepdims=True) for gi in gcum]
    decay = [jnp.exp(jnp.where(incl, gcum[n] - grow[n], -jnp.inf)) for n in n_it]
    eg = [jnp.exp(gi) for gi in gcum]
    kb = [kc[n] * bc[n] for n in n_it]
    aq = [_bdot_nt(jnp.concatenate([kb[n], qc[n]], axis=0), _block_diag(kc[n], bd)) for n in n_it]
    a = [jnp.where(strict, aq[n][:c] * decay[n], 0.0) for n in n_it]
    qk = [aq[n][c:] * decay[n] for n in n_it]
    p = [-ai for ai in a]
    pw = [_bdot(ai, _block_diag(ai, bd)) for ai in a]
    n_sq = int(np.log2(c)) - 1
    for r in range(n_sq):
        if r < n_sq - 1:
            both = [_bdot(jnp.concatenate([p[n], pw[n]], axis=0), _block_diag(pw[n], bd)) for n in n_it]
            p = [p[n] + pw[n] + both[n][:c] for n in n_it]
            pw = [both[n][c:] for n in n_it]
        else:
            p = [p[n] + pw[n] + _bdot(p[n], _block_diag(pw[n], bd)) for n in n_it]
    vb = [vc[n] * bc[n] for n in n_it]
    kbg = [kb[n] * eg[n] for n in n_it]
    uw = [_bdot(p[n], jnp.concatenate([_block_diag(vb[n], bd), _block_diag(kbg[n], bd)], axis=1)) for n in n_it]
    u = [vb[n] + uw[n][:, :hw] for n in n_it]
    wq = [jnp.concatenate([kbg[n] + uw[n][:, hw:], qc[n] * eg[n]], axis=0) for n in n_it]
    g_last = [gi[c - 1:c, :] for gi in gcum]
    kd = [kc[n] * jnp.exp(g_last[n] - gcum[n]) for n in n_it]
    eg_last = [jnp.exp(gl) for gl in g_last]

    s = [s_scr[bi] for bi in range(nb)]
    outs = [[] for _ in range(nb)]
    for n, (ci, bi) in enumerate(items):
        ws = _bdot(wq[n], _block_diag(s[bi], bd))
        r = u[n] - ws[:c]
        outs[bi].append(ws[c:] + _bdot(qk[n], _block_diag(r, bd)))
        s[bi] = eg_last[n] * s[bi] + _diag_blocks(_bdot_tn(kd[n], r), A_DK, A_DV)

    for bi in range(nb):
        s_scr[bi] = s[bi]
        o = jnp.concatenate(outs[bi], axis=0)
        o = o * lax.rsqrt(_seg_sum(o * o, ones_bd) * (1.0 / A_DV) + EPS) * anorm_ref[...]
        o_ref[bi] = o * _silu(zs[bi])

    @pl.when(t == pl.num_programs(1) - 1)
    def _():
        for bi in range(nb):
            conv_ref[bi] = xs[bi][tblk - (CONV_W - 1):tblk, :]
            s_out_ref[bi] = s[bi]


def _gdn_prompt(x, nw, w_a, cw, alog, dtb, anorm, *, tblk, nb):
    b, t, d = x.shape
    hw = A_HEADS * A_DV
    return pl.pallas_call(
        functools.partial(_gdn_prompt_kernel, tblk=tblk, nb=nb),
        grid=(b // nb, t // tblk),
        in_specs=[pl.BlockSpec((nb, tblk, d), lambda i, j: (i, j, 0)),
                  _const_spec((1, d)), _const_spec((d, UA_W)),
                  _const_spec((CONV_W, A_CONV_CH)), _const_spec((1, LANE)), _const_spec((1, LANE)),
                  _const_spec((1, hw))],
        out_specs=[pl.BlockSpec((nb, tblk, hw), lambda i, j: (i, j, 0)),
                   pl.BlockSpec((nb, CONV_W - 1, A_CONV_CH), lambda i, j: (i, 0, 0)),
                   pl.BlockSpec((nb, A_DK, hw), lambda i, j: (i, 0, 0))],
        out_shape=[jax.ShapeDtypeStruct((b, t, hw), F32),
                   jax.ShapeDtypeStruct((b, CONV_W - 1, A_CONV_CH), F32),
                   jax.ShapeDtypeStruct((b, A_DK, hw), F32)],
        scratch_shapes=[pltpu.VMEM((nb, tblk + 8, A_CONV_CH), F32), pltpu.VMEM((nb, A_DK, hw), F32)],
        compiler_params=_cparams(("parallel", "arbitrary")),
        name="gdn_prompt",
    )(x, nw, w_a, cw, alog, dtb, anorm)


def _gla_prompt_kernel(x_ref, nw_ref, win_ref, w2_ref, gb_ref, bnorm_ref, o_ref, s_out_ref, s_scr, *, tblk):
    t = pl.program_id(1)
    vw = B_HEADS * B_DV

    @pl.when(t == 0)
    def _():
        s_scr[...] = jnp.zeros_like(s_scr)

    u = _project(x_ref[0], nw_ref, win_ref)
    q = u[:, 0:B_QK] * (B_DK ** -0.5)
    k = u[:, B_QK:2 * B_QK]
    v = u[:, 2 * B_QK:2 * B_QK + B_V]
    rg = u[:, 2 * B_QK + B_V:2 * B_QK + 2 * B_V]
    glr = u[:, 2 * B_QK + 2 * B_V:]
    log_a = jax.nn.log_sigmoid(_bdot(glr, w2_ref[...]) + gb_ref[...]) / B_GATE_TAU

    c = CHUNK
    kmask = _block_mask(vw, B_QK, c, B_DK)
    vmask = _block_mask(vw, vw, c, B_DV)
    smask = _block_mask(vw, B_QK, B_DV, B_DK)
    row_k = _iota((c, B_QK), 0)
    row_s = _iota((SUB, vw), 0)
    col_s = _iota((SUB, vw), 1) % c
    st = s_scr[...]
    outs = []
    for ci in range(tblk // c):
        sl = slice(ci * c, (ci + 1) * c)
        qc, kc, vc = q[sl], k[sl], v[sl]
        b = _cumsum_rows(log_a[sl])
        o = _bdot_nt(qc * jnp.exp(b), st)
        att = []
        for i in range(c // SUB):
            r0 = i * SUB
            bref = b[r0:r0 + 1, :]
            qs = qc[r0:r0 + SUB] * jnp.exp(b[r0:r0 + SUB] - bref)
            ks = kc * jnp.exp(jnp.where(row_k < r0 + SUB, bref - b, -jnp.inf))
            a_i = _bdot_nt(qs, _block_diag(ks, kmask))
            att.append(jnp.where(row_s + r0 >= col_s, a_i, 0.0))
        o = o + _bdot(jnp.concatenate(att, axis=0), _block_diag(vc, vmask))
        outs.append(o)
        b_last = b[c - 1:c, :]
        kd = kc * jnp.exp(b_last - b)
        st = jnp.exp(b_last) * st + jnp.where(smask, _bdot_tn(vc, kd), 0.0)
    s_scr[...] = st

    o = jnp.concatenate(outs, axis=0)
    ones_bd = _block_mask(vw, vw, B_DV, B_DV).astype(BF16)
    o = o * lax.rsqrt(_seg_sum(o * o, ones_bd) * (1.0 / B_DV) + EPS) * bnorm_ref[...]
    o_ref[0] = o * _silu(rg)

    @pl.when(t == pl.num_programs(1) - 1)
    def _():
        s_out_ref[0] = st


def _gla_prompt(x, nw, w_b, w2, gb, bnorm, *, tblk):
    b, t, d = x.shape
    vw = B_HEADS * B_DV
    return pl.pallas_call(
        functools.partial(_gla_prompt_kernel, tblk=tblk),
        grid=(b, t // tblk),
        in_specs=[pl.BlockSpec((1, tblk, d), lambda i, j: (i, j, 0)),
                  _const_spec((1, d)), _const_spec((d, UB_W)),
                  _const_spec((LANE, B_QK)), _const_spec((1, B_QK)), _const_spec((1, vw))],
        out_specs=[pl.BlockSpec((1, tblk, vw), lambda i, j: (i, j, 0)),
                   pl.BlockSpec((1, vw, B_QK), lambda i, j: (i, 0, 0))],
        out_shape=[jax.ShapeDtypeStruct((b, t, vw), F32),
                   jax.ShapeDtypeStruct((b, vw, B_QK), F32)],
        scratch_shapes=[pltpu.VMEM((vw, B_QK), F32)],
        compiler_params=_cparams(("parallel", "arbitrary")),
        name="gla_prompt",
    )(x, nw, w_b, w2, gb, bnorm)


def _mla_prep_kernel(x_ref, nw_ref, win_ref, cqn_ref, ckvn_ref, wuq_ref, wuk_ref, ck_ref, sk_ref, cq_ref, sq_ref,
                     ckv_o, kpe_o, ckvb_o, kpeb_o, qlat_o, qpe_o):
    u = _project(x_ref[0], nw_ref, win_ref)
    cq = _rms(u[:, 0:C_Q_LORA], cqn_ref[...]).astype(BF16)
    ckv = _rms(u[:, C_Q_LORA:C_Q_LORA + C_KV_LORA], ckvn_ref[...])
    half = C_ROPE // 2
    kx = u[:, C_Q_LORA + C_KV_LORA:]
    lane = _iota(kx.shape, 1)
    kswap = jnp.where(lane < half, pltpu.roll(kx, LANE - half, axis=1), pltpu.roll(kx, half, axis=1))
    kpe = (kx * ck_ref[...] + kswap * sk_ref[...])[:, 0:C_ROPE]
    ckv_o[0] = ckv
    kpe_o[0] = kpe
    ckvb_o[0] = ckv.astype(BF16)
    kpeb_o[0] = kpe.astype(BF16)

    qf = jnp.dot(cq, wuq_ref[...], preferred_element_type=F32)
    nope_w = C_HEADS * C_NOPE
    rope_w = C_HEADS * C_ROPE
    qr = qf[:, nope_w:]
    lane_r = _iota(qr.shape, 1) % C_ROPE
    qswap = jnp.where(lane_r < half, pltpu.roll(qr, rope_w - half, axis=1), pltpu.roll(qr, half, axis=1))
    qpe = qr * cq_ref[...] + qswap * sq_ref[...]
    for h in range(C_HEADS):
        qlat_o[0, h] = jnp.dot(qf[:, h * C_NOPE:(h + 1) * C_NOPE].astype(BF16), wuk_ref[h],
                               preferred_element_type=F32).astype(BF16)
        qpe_o[0, h] = qpe[:, h * C_ROPE:(h + 1) * C_ROPE].astype(BF16)


def _mla_prep(x, nw, w_c, cqn, ckvn, wuq, wuk, tabs, *, tm):
    b, t, d = x.shape
    ck, sk, cq, sq = tabs
    per_pos = ck.shape[0] != 1
    rope_w = C_HEADS * C_ROPE

    def tab_spec(w):
        if per_pos:
            return pl.BlockSpec((tm, w), lambda i, j: (j, 0))
        return _const_spec((1, w))

    def tok_spec(w):
        return pl.BlockSpec((1, tm, w), lambda i, j: (i, j, 0))

    def head_spec(w):
        return pl.BlockSpec((1, C_HEADS, tm, w), lambda i, j: (i, 0, j, 0))

    return pl.pallas_call(
        _mla_prep_kernel,
        grid=(b, t // tm),
        in_specs=[tok_spec(d), _const_spec((1, d)), _const_spec((d, UC_W)),
                  _const_spec((1, C_Q_LORA)), _const_spec((1, C_KV_LORA)),
                  _const_spec(wuq.shape), _const_spec(wuk.shape),
                  tab_spec(LANE), tab_spec(LANE), tab_spec(rope_w), tab_spec(rope_w)],
        out_specs=[tok_spec(C_KV_LORA), tok_spec(C_ROPE), tok_spec(C_KV_LORA), tok_spec(C_ROPE),
                   head_spec(C_KV_LORA), head_spec(C_ROPE)],
        out_shape=[jax.ShapeDtypeStruct((b, t, C_KV_LORA), F32),
                   jax.ShapeDtypeStruct((b, t, C_ROPE), F32),
                   jax.ShapeDtypeStruct((b, t, C_KV_LORA), BF16),
                   jax.ShapeDtypeStruct((b, t, C_ROPE), BF16),
                   jax.ShapeDtypeStruct((b, C_HEADS, t, C_KV_LORA), BF16),
                   jax.ShapeDtypeStruct((b, C_HEADS, t, C_ROPE), BF16)],
        compiler_params=_cparams(("parallel", "parallel")),
        name="mla_prep",
    )(x, nw, w_c, cqn, ckvn, wuq, wuk, ck, sk, cq, sq)


def _mla_flash_kernel(qi_ref, kj_ref, ql_ref, qp_ref, k_ref, p_ref, wuv_ref, o_ref,
                      m_scr, l_scr, acc_scr, *, tq, tk):
    g = pl.program_id(1)
    i = qi_ref[g]
    j = kj_ref[g]
    last_j = (i * tq + tq - 1) // tk
    c2 = MLA_SCALE * LOG2E

    @pl.when(j == 0)
    def _():
        m_scr[...] = jnp.full_like(m_scr, -jnp.inf)
        l_scr[...] = jnp.zeros_like(l_scr)
        acc_scr[...] = jnp.zeros_like(acc_scr)

    def step(masked):
        kv = k_ref[0]
        pe = p_ref[0]
        if masked and tq == tk:
            parts = [(0, tq // 2, tk // 2), (tq // 2, tq, tk)]
        else:
            parts = [(0, tq, tk)]
        for h in range(C_HEADS):
            for r0, r1, nk in parts:
                s = (lax.dot_general(ql_ref[0, h, r0:r1, :], kv[:nk], (((1,), (1,)), ((), ())),
                                     preferred_element_type=F32)
                     + lax.dot_general(qp_ref[0, h, r0:r1, :], pe[:nk], (((1,), (1,)), ((), ())),
                                       preferred_element_type=F32))
                if masked:
                    visible = j * tk + _iota((r1 - r0, nk), 1) <= i * tq + r0 + _iota((r1 - r0, nk), 0)
                    s = jnp.where(visible, s, -jnp.inf)
                m_old = m_scr[h, r0:r1, :]
                m_new = jnp.maximum(m_old, jnp.max(s, axis=-1, keepdims=True))
                alpha = jnp.exp2((m_old - m_new) * c2)
                p = jnp.exp2(s * c2 - m_new * c2)
                l_scr[h, r0:r1, :] = alpha * l_scr[h, r0:r1, :] + jnp.sum(p, axis=-1, keepdims=True)
                m_scr[h, r0:r1, :] = m_new
                acc_scr[h, r0:r1, :] = alpha * acc_scr[h, r0:r1, :] + jnp.dot(
                    p.astype(BF16), kv[:nk], preferred_element_type=F32)

    has_masked = j * tk + tk - 1 > i * tq

    @pl.when(has_masked)
    def _():
        step(True)

    @pl.when(jnp.logical_not(has_masked))
    def _():
        step(False)

    @pl.when(j == last_j)
    def _():
        o_ref[0] = jnp.concatenate(
            [jnp.dot((acc_scr[h] / l_scr[h]).astype(BF16), wuv_ref[h], preferred_element_type=F32)
             for h in range(C_HEADS)], axis=-1)


def _mla_flash(qlat, qpe, ckvb, kpeb, wuv, *, tq, tk):
    b, _, t, _ = qlat.shape
    pairs = [(i, j) for i in range(t // tq) for j in range((i * tq + tq - 1) // tk + 1)]
    qi = jnp.asarray([p[0] for p in pairs], jnp.int32)
    kj = jnp.asarray([p[1] for p in pairs], jnp.int32)
    grid_spec = pltpu.PrefetchScalarGridSpec(
        num_scalar_prefetch=2,
        grid=(b, len(pairs)),
        in_specs=[pl.BlockSpec((1, C_HEADS, tq, C_KV_LORA), lambda bi, g, qi, kj: (bi, 0, qi[g], 0)),
                  pl.BlockSpec((1, C_HEADS, tq, C_ROPE), lambda bi, g, qi, kj: (bi, 0, qi[g], 0)),
                  pl.BlockSpec((1, tk, C_KV_LORA), lambda bi, g, qi, kj: (bi, kj[g], 0)),
                  pl.BlockSpec((1, tk, C_ROPE), lambda bi, g, qi, kj: (bi, kj[g], 0)),
                  _const_spec(wuv.shape)],
        out_specs=pl.BlockSpec((1, tq, C_HEADS * C_VDIM), lambda bi, g, qi, kj: (bi, qi[g], 0)),
        scratch_shapes=[pltpu.VMEM((C_HEADS, tq, 1), F32), pltpu.VMEM((C_HEADS, tq, 1), F32),
                        pltpu.VMEM((C_HEADS, tq, C_KV_LORA), F32)],
    )
    return pl.pallas_call(
        functools.partial(_mla_flash_kernel, tq=tq, tk=tk),
        grid_spec=grid_spec,
        out_shape=jax.ShapeDtypeStruct((b, t, C_HEADS * C_VDIM), F32),
        compiler_params=_cparams(("parallel", "arbitrary")),
        name="mla_flash",
    )(qi, kj, qlat, qpe, ckvb, kpeb, wuv)


def _mla_oproj_kernel(ol_ref, wuv_ref, o_ref):
    o_ref[0] = jnp.concatenate(
        [jnp.dot(ol_ref[0, h].astype(BF16), wuv_ref[h], preferred_element_type=F32)
         for h in range(C_HEADS)], axis=-1)


def _mla_oproj(olat, wuv, *, tm):
    b, _, t, _ = olat.shape
    return pl.pallas_call(
        _mla_oproj_kernel,
        grid=(b, t // tm),
        in_specs=[pl.BlockSpec((1, C_HEADS, tm, C_KV_LORA), lambda i, j: (i, 0, j, 0)),
                  _const_spec(wuv.shape)],
        out_specs=pl.BlockSpec((1, tm, C_HEADS * C_VDIM), lambda i, j: (i, j, 0)),
        out_shape=jax.ShapeDtypeStruct((b, t, C_HEADS * C_VDIM), F32),
        compiler_params=_cparams(("parallel", "parallel")),
        name="mla_oproj",
    )(olat, wuv)


def _mla_paged_kernel(pt_ref, ql_ref, qp_ref, cn_ref, pn_ref, ckv_hbm, kpe_hbm, o_ref,
                      kbuf, pbuf, sem, m_scr, l_scr, acc_scr, *, layer, ppc, nch, page):
    b = pl.program_id(0)
    c = pl.program_id(1)
    g = b * nch + c
    total = pl.num_programs(0) * nch
    slot = g % PAGED_SLOTS

    def chunk_of(step):
        st = jnp.minimum(step, total - 1)
        return st // nch, st % nch

    def copies(pid, sl, p):
        dst = pl.ds(p * page, page)
        return (pltpu.make_async_copy(ckv_hbm.at[layer, pid], kbuf.at[sl, dst, :], sem.at[0, sl]),
                pltpu.make_async_copy(kpe_hbm.at[layer, pid], pbuf.at[sl, :, dst], sem.at[1, sl]))

    def issue(bb, cc, sl):
        for p in range(ppc):
            for prio, cp in enumerate(copies(pt_ref[bb, cc * ppc + p], sl, p)):
                cp.start(priority=prio)

    def wait_all(sl):
        for p in range(ppc):
            for cp in copies(0, sl, p):
                cp.wait()

    @pl.when(g == 0)
    def _():
        for ahead in range(PAGED_SLOTS - 1):
            issue(*chunk_of(ahead), ahead)

    @pl.when(c == 0)
    def _():
        m_scr[...] = jnp.full_like(m_scr, -jnp.inf)
        l_scr[...] = jnp.zeros_like(l_scr)
        acc_scr[...] = jnp.zeros_like(acc_scr)

    wait_all(slot)
    issue(*chunk_of(g + PAGED_SLOTS - 1), (g + PAGED_SLOTS - 1) % PAGED_SLOTS)

    ql = ql_ref[0]
    qp = qp_ref[0]
    kv = kbuf[slot].astype(BF16)
    pe = pbuf[slot].astype(BF16)
    s = (lax.dot_general(ql, kv, (((1,), (1,)), ((), ())), preferred_element_type=F32)
         + jnp.dot(qp, pe, preferred_element_type=F32)) * MLA_SCALE
    m_old = m_scr[...]
    m_new = jnp.maximum(m_old, jnp.max(s, axis=-1, keepdims=True))
    alpha = jnp.exp(m_old - m_new)
    p = jnp.exp(s - m_new)
    l_new = alpha * l_scr[...] + jnp.sum(p, axis=-1, keepdims=True)
    acc_new = alpha * acc_scr[...] + jnp.dot(p.astype(BF16), kv, preferred_element_type=F32)
    m_scr[...] = m_new
    l_scr[...] = l_new
    acc_scr[...] = acc_new

    @pl.when(c == nch - 1)
    def _():
        cn = cn_ref[0].astype(BF16).astype(F32)
        pn = pn_ref[0].astype(BF16).astype(F32)
        s_n = (jnp.sum(ql.astype(F32) * cn, axis=-1, keepdims=True)
               + jnp.sum(qp.astype(F32) * pn, axis=-1, keepdims=True)) * MLA_SCALE
        m_f = jnp.maximum(m_new, s_n)
        a_f = jnp.exp(m_new - m_f)
        p_n = jnp.exp(s_n - m_f)
        l_f = a_f * l_new + p_n
        acc_f = a_f * acc_new + p_n.astype(BF16).astype(F32) * cn
        o_ref[0] = acc_f / l_f

    @pl.when(g == total - 1)
    def _():
        for ahead in range(1, PAGED_SLOTS):
            wait_all((g + ahead) % PAGED_SLOTS)


def _mla_paged(page_table, qlat, qpe, ckv_new, kpe_new, cache_ckv, cache_kpe, *, layer, ppc):
    b, hp, _ = qlat.shape
    n_pages = page_table.shape[1]
    page = cache_ckv.shape[2]
    nch = n_pages // ppc
    grid_spec = pltpu.PrefetchScalarGridSpec(
        num_scalar_prefetch=1,
        grid=(b, nch),
        in_specs=[pl.BlockSpec((1, hp, C_KV_LORA), lambda i, j, pt: (i, 0, 0)),
                  pl.BlockSpec((1, hp, C_ROPE), lambda i, j, pt: (i, 0, 0)),
                  pl.BlockSpec((1, 1, C_KV_LORA), lambda i, j, pt: (i, 0, 0)),
                  pl.BlockSpec((1, 1, C_ROPE), lambda i, j, pt: (i, 0, 0)),
                  pl.BlockSpec(memory_space=pl.ANY),
                  pl.BlockSpec(memory_space=pl.ANY)],
        out_specs=pl.BlockSpec((1, hp, C_KV_LORA), lambda i, j, pt: (i, 0, 0)),
        scratch_shapes=[pltpu.VMEM((PAGED_SLOTS, ppc * page, C_KV_LORA), F32),
                        pltpu.VMEM((PAGED_SLOTS, C_ROPE, ppc * page), F32),
                        pltpu.SemaphoreType.DMA((2, PAGED_SLOTS)),
                        pltpu.VMEM((hp, 1), F32), pltpu.VMEM((hp, 1), F32),
                        pltpu.VMEM((hp, C_KV_LORA), F32)],
    )
    return pl.pallas_call(
        functools.partial(_mla_paged_kernel, layer=layer, ppc=ppc, nch=nch, page=page),
        grid_spec=grid_spec,
        out_shape=jax.ShapeDtypeStruct((b, hp, C_KV_LORA), F32),
        compiler_params=_cparams(("arbitrary", "arbitrary")),
        name="mla_paged",
    )(page_table, qlat, qpe, ckv_new, kpe_new, cache_ckv, cache_kpe)


def _xattn_prompt_kernel(x_ref, oa_ref, ob_ref, oc_ref, wa_ref, wb_ref, wc_ref,
                         nw_ref, wq_ref, wo_ref, mk_ref, mv_ref, o_ref):
    x = x_ref[0]
    for a_ref, w_ref in ((oa_ref, wa_ref), (ob_ref, wb_ref), (oc_ref, wc_ref)):
        x = x + jnp.dot(a_ref[0].astype(BF16), w_ref[...], preferred_element_type=F32)
    xn = _rms(x, nw_ref[...]).astype(BF16)
    q = jnp.dot(xn, wq_ref[...], preferred_element_type=F32).astype(BF16)
    heads = []
    for h in range(X_HEADS):
        sl = slice(h * X_HDIM, (h + 1) * X_HDIM)
        s = lax.dot_general(q[:, sl], mk_ref[0, :, sl], (((1,), (1,)), ((), ())),
                            preferred_element_type=F32) * (X_HDIM ** -0.5)
        e = jnp.exp(s - jnp.max(s, axis=-1, keepdims=True))
        p = (e / jnp.sum(e, axis=-1, keepdims=True)).astype(BF16)
        heads.append(jnp.dot(p, mv_ref[0, :, sl], preferred_element_type=F32).astype(BF16))
    o_ref[0] = x + jnp.dot(jnp.concatenate(heads, axis=-1), wo_ref[...], preferred_element_type=F32)


def _xattn_prompt(x, o_list, w_out_list, nw, wq, wo, mk, mv, *, tm):
    b, t, d = x.shape
    n_mem = mk.shape[1]
    return pl.pallas_call(
        _xattn_prompt_kernel,
        grid=(b, t // tm),
        in_specs=[pl.BlockSpec((1, tm, d), lambda i, j: (i, j, 0))]
                 + [pl.BlockSpec((1, tm, o.shape[2]), lambda i, j: (i, j, 0)) for o in o_list]
                 + [_w_spec(w) for w in w_out_list]
                 + [_const_spec((1, d)), _w_spec(wq), _w_spec(wo),
                  pl.BlockSpec((1, n_mem, d), lambda i, j: (i, 0, 0)),
                  pl.BlockSpec((1, n_mem, d), lambda i, j: (i, 0, 0))],
        out_specs=pl.BlockSpec((1, tm, d), lambda i, j: (i, j, 0)),
        out_shape=jax.ShapeDtypeStruct((b, t, d), F32),
        compiler_params=_cparams(("parallel", "parallel")),
        name="xattn_prompt",
    )(x, *o_list, *[_w_arr(w) for w in w_out_list], nw, _w_arr(wq), _w_arr(wo), mk, mv)


def _xattn_decode_kernel(q_ref, mk_ref, mv_ref, o_ref):
    for bi in range(q_ref.shape[0]):
        _xattn_decode_one(q_ref, mk_ref, mv_ref, o_ref, bi)


def _xattn_decode_one(q_ref, mk_ref, mv_ref, o_ref, bi):
    nt = X_HDIM // LANE
    grp = nt * X_HEADS
    n_rows = mk_ref.shape[2]
    qrow = q_ref[bi]
    qm = jnp.concatenate([qrow[:, h * X_HDIM + t * LANE:h * X_HDIM + (t + 1) * LANE]
                          for t in range(nt) for h in range(X_HEADS)], axis=0)
    sel = _iota((grp, n_rows), 0) == _iota((grp, n_rows), 1) % grp
    s_all = _bdot_nt(qm, mk_ref[0, bi])
    part = jnp.sum(jnp.where(sel, s_all, 0.0), axis=0, keepdims=True)
    lane = _iota((1, n_rows), 1)
    s = part
    for t in range(1, nt):
        s = s + jnp.where(lane % grp < X_HEADS, pltpu.roll(part, n_rows - t * X_HEADS, axis=1),
                          pltpu.roll(part, t * X_HEADS, axis=1))
    s = s * (X_HDIM ** -0.5)
    p = jnp.zeros_like(s)
    for h in range(X_HEADS):
        mine = lane % X_HEADS == h
        e = jnp.exp(s - jnp.max(jnp.where(mine, s, -jnp.inf), axis=-1, keepdims=True))
        den = jnp.sum(jnp.where(lane % grp == h, e, 0.0), axis=-1, keepdims=True)
        p = jnp.where(mine, e / den, p)
    o = _bdot(jnp.where(sel, p, 0.0), mv_ref[0, bi])
    o_ref[bi] = jnp.concatenate([o[t * X_HEADS + h:t * X_HEADS + h + 1]
                                for h in range(X_HEADS) for t in range(nt)], axis=-1)


def _mem_rows(mem):
    dep, b, n_mem, nh, hd = mem.shape
    nt = hd // LANE
    return mem.reshape(dep, b, n_mem, nh, nt, LANE).transpose(0, 1, 2, 4, 3, 5).reshape(dep, b, n_mem * nt * nh, LANE)


def _xattn_decode(q, mk_rows, mv_rows, *, layer, nb):
    _, b, n_rows, _ = mk_rows.shape
    d = X_HEADS * X_HDIM
    assert X_HDIM == 2 * LANE
    mem_spec = pl.BlockSpec((1, nb, n_rows, LANE), lambda i: (layer, i, 0, 0))
    return pl.pallas_call(
        _xattn_decode_kernel,
        grid=(b // nb,),
        in_specs=[pl.BlockSpec((nb, 1, d), lambda i: (i, 0, 0)), mem_spec, mem_spec],
        out_specs=pl.BlockSpec((nb, 1, d), lambda i: (i, 0, 0)),
        out_shape=jax.ShapeDtypeStruct((b, 1, d), F32),
        compiler_params=_cparams(("parallel",)),
        name="xattn_decode",
    )(q, mk_rows, mv_rows)


def _columns(rows):
    w = rows[0].shape[1]
    assert len(rows) <= LANE
    pad = [jnp.zeros((LANE - len(rows), w), F32)] if len(rows) < LANE else []
    return jnp.concatenate(rows + pad, axis=0).T


def _gdn_decode_kernel(u_ref, cs_ref, s_ref, cw_ref, alog_ref, dtb_ref, anorm_ref,
                       o_ref, cs_out_ref, s_out_ref):
    seqs = range(u_ref.shape[0])
    heads = range(A_HEADS)
    items = [(bi, h) for bi in seqs for h in heads]
    cw = cw_ref[...]
    x = [u_ref[bi, :, 0:A_CONV_CH] for bi in seqs]
    z = [u_ref[bi, :, A_CONV_CH:A_CONV_CH + A_V] for bi in seqs]
    ba = [u_ref[bi, :, A_CONV_CH + A_V:UA_W] for bi in seqs]
    cs = [cs_ref[0, bi] for bi in seqs]
    qkv = [_silu(cw[0:1] * cs[bi][0:1] + cw[1:2] * cs[bi][1:2] + cw[2:3] * cs[bi][2:3] + cw[3:4] * x[bi])
           for bi in seqs]
    for bi in seqs:
        cs_out_ref[bi] = jnp.concatenate([cs[bi][1:CONV_W - 1], x[bi]], axis=0)
    beta = [jax.nn.sigmoid(b) for b in ba]
    eg_all = [jnp.exp(-jnp.exp(alog_ref[...]) * jax.nn.softplus(b + dtb_ref[...])) for b in ba]
    q_raw = [qkv[bi][:, h * A_DK:(h + 1) * A_DK] for bi, h in items]
    k_raw = [qkv[bi][:, A_QK + h * A_DK:A_QK + (h + 1) * A_DK] for bi, h in items]
    v = [qkv[bi][:, 2 * A_QK + h * A_DV:2 * A_QK + (h + 1) * A_DV] for bi, h in items]
    q = [a * lax.rsqrt(jnp.sum(a * a, axis=-1, keepdims=True) + EPS) * (A_DK ** -0.5) for a in q_raw]
    k = [a * lax.rsqrt(jnp.sum(a * a, axis=-1, keepdims=True) + EPS) for a in k_raw]
    cols = _columns(k + q)
    kcol = [cols[:, n:n + 1] for n in range(len(items))]
    qcol = [cols[:, len(items) + n:len(items) + n + 1] for n in range(len(items))]
    s = [s_ref[0, bi, h] for bi, h in items]
    bh = [beta[bi][:, h:h + 1] for bi, h in items]
    eg = [eg_all[bi][:, A_HEADS + h:A_HEADS + h + 1] for bi, h in items]
    ks_row = [jnp.sum(kcol[n] * s[n], axis=0, keepdims=True) for n in range(len(items))]
    qs_row = [jnp.sum(qcol[n] * s[n], axis=0, keepdims=True) for n in range(len(items))]
    r = [v[n] * bh[n] - (bh[n] * eg[n]) * ks_row[n] for n in range(len(items))]
    qk = [jnp.sum(q[n] * k[n], axis=-1, keepdims=True) for n in range(len(items))]
    o = [eg[n] * qs_row[n] + qk[n] * r[n] for n in range(len(items))]
    for n, (bi, h) in enumerate(items):
        s_out_ref[bi, h] = eg[n] * s[n] + kcol[n] * r[n]
    o = [a * lax.rsqrt(jnp.mean(a * a, axis=-1, keepdims=True) + EPS) * anorm_ref[...] for a in o]
    for bi in seqs:
        o_ref[bi] = jnp.concatenate(o[bi * A_HEADS:(bi + 1) * A_HEADS], axis=-1) * _silu(z[bi])


def _gdn_decode(u_a, cs, s, cw, alog, dtb, anorm, *, layer, nb):
    b = u_a.shape[0]
    return pl.pallas_call(
        _gdn_decode_kernel,
        grid=(b // nb,),
        in_specs=[pl.BlockSpec((nb, 1, UA_W), lambda i: (i, 0, 0)),
                  pl.BlockSpec((1, nb, CONV_W - 1, A_CONV_CH), lambda i: (layer, i, 0, 0)),
                  pl.BlockSpec((1, nb, A_HEADS, A_DK, A_DV), lambda i: (layer, i, 0, 0, 0)),
                  _const_spec((CONV_W, A_CONV_CH)), _const_spec((1, LANE)), _const_spec((1, LANE)),
                  _const_spec((1, A_DV))],
        out_specs=[pl.BlockSpec((nb, 1, A_V), lambda i: (i, 0, 0)),
                   pl.BlockSpec((nb, CONV_W - 1, A_CONV_CH), lambda i: (i, 0, 0)),
                   pl.BlockSpec((nb, A_HEADS, A_DK, A_DV), lambda i: (i, 0, 0, 0))],
        out_shape=[jax.ShapeDtypeStruct((b, 1, A_V), F32),
                   jax.ShapeDtypeStruct((b, CONV_W - 1, A_CONV_CH), F32),
                   jax.ShapeDtypeStruct((b, A_HEADS, A_DK, A_DV), F32)],
        compiler_params=_cparams(("parallel",)),
        name="gdn_decode",
    )(u_a, cs, s, cw, alog, dtb, anorm)


def _gla_decode_kernel(u_ref, s_ref, w2_ref, gb_ref, bnorm_ref, o_ref, s_out_ref):
    nb = u_ref.shape[0]
    seqs = range(nb)
    items = [(bi, h) for bi in seqs for h in range(B_HEADS)]
    u = [u_ref[bi] for bi in seqs]
    q = [a[:, 0:B_QK] * (B_DK ** -0.5) for a in u]
    k = [a[:, B_QK:2 * B_QK] for a in u]
    rg = [a[:, 2 * B_QK + B_V:2 * B_QK + 2 * B_V] for a in u]
    pad = [jnp.zeros(((-nb) % 8, LANE), F32)] if nb % 8 else []
    gate = _bdot(jnp.concatenate([a[:, 2 * B_QK + 2 * B_V:] for a in u] + pad, axis=0), w2_ref[...])
    dec = [jnp.exp(jax.nn.log_sigmoid(gate[bi:bi + 1] + gb_ref[...]) / B_GATE_TAU) for bi in seqs]
    cols = _columns(k + [q[bi] * dec[bi] for bi in seqs] + dec)
    v = [u[bi][:, 2 * B_QK + h * B_DV:2 * B_QK + (h + 1) * B_DV] for bi, h in items]
    s = [s_ref[0, bi, h] for bi, h in items]
    kcol = [cols[h * B_DK:(h + 1) * B_DK, bi:bi + 1] for bi, h in items]
    qacol = [cols[h * B_DK:(h + 1) * B_DK, nb + bi:nb + bi + 1] for bi, h in items]
    acol = [cols[h * B_DK:(h + 1) * B_DK, 2 * nb + bi:2 * nb + bi + 1] for bi, h in items]
    qk = [jnp.sum(q[bi][:, h * B_DK:(h + 1) * B_DK] * k[bi][:, h * B_DK:(h + 1) * B_DK], axis=-1, keepdims=True)
          for bi, h in items]
    o = [jnp.sum(qacol[n] * s[n], axis=0, keepdims=True) + qk[n] * v[n] for n in range(len(items))]
    for n, (bi, h) in enumerate(items):
        s_out_ref[bi, h] = acol[n] * s[n] + kcol[n] * v[n]
    o = [a * lax.rsqrt(jnp.mean(a * a, axis=-1, keepdims=True) + EPS) * bnorm_ref[...] for a in o]
    for bi in seqs:
        o_ref[bi] = jnp.concatenate(o[bi * B_HEADS:(bi + 1) * B_HEADS], axis=-1) * _silu(rg[bi])


def _gla_decode(u_b, s, w2, gb, bnorm, *, layer, nb):
    b = u_b.shape[0]
    return pl.pallas_call(
        _gla_decode_kernel,
        grid=(b // nb,),
        in_specs=[pl.BlockSpec((nb, 1, UB_W), lambda i: (i, 0, 0)),
                  pl.BlockSpec((1, nb, B_HEADS, B_DK, B_DV), lambda i: (layer, i, 0, 0, 0)),
                  _const_spec((LANE, B_QK)), _const_spec((1, B_QK)), _const_spec((1, B_DV))],
        out_specs=[pl.BlockSpec((nb, 1, B_V), lambda i: (i, 0, 0)),
                   pl.BlockSpec((nb, B_HEADS, B_DK, B_DV), lambda i: (i, 0, 0, 0))],
        out_shape=[jax.ShapeDtypeStruct((b, 1, B_V), F32),
                   jax.ShapeDtypeStruct((b, B_HEADS, B_DK, B_DV), F32)],
        compiler_params=_cparams(("parallel",)),
        name="gla_decode",
    )(u_b, s, w2, gb, bnorm)


def _pad_cols(w, n):
    return jnp.pad(w, ((0, 0), (0, n - w.shape[1])))


def _rope_tables(pos):
    half = C_ROPE // 2
    inv = ROPE_THETA ** (-jnp.arange(half, dtype=F32) / half)
    ang = pos.astype(F32)[:, None] * inv[None, :]
    cos, sin = jnp.cos(ang), jnp.sin(ang)
    zero = jnp.zeros_like(cos)
    ck = jnp.concatenate([cos, cos, zero, zero], axis=-1)
    sk = jnp.concatenate([-sin, sin, zero, zero], axis=-1)
    cq = jnp.tile(jnp.concatenate([cos, cos], axis=-1), (1, C_HEADS))
    sq = jnp.tile(jnp.concatenate([-sin, sin], axis=-1), (1, C_HEADS))
    return ck, sk, cq, sq


STACKED_BF16 = ('w_ffn1_gate', 'w_ffn1_up', 'w_ffn1_down', 'w_ffn2_gate', 'w_ffn2_up', 'w_ffn2_down',
                'w_xq', 'w_xo', 'w_xk', 'w_xv')


def _layer_weights(l, P, Pb):
    w_in = P['w_in'][l]
    o_b, o_c = A_IN, A_IN + B_IN
    w_a = jnp.concatenate([w_in[:, :A_CONV_CH + A_V], _pad_cols(w_in[:, A_CONV_CH + A_V:A_IN], LANE)], axis=1)
    w_b = jnp.concatenate([w_in[:, o_b:o_b + 2 * B_QK + 2 * B_V],
                           _pad_cols(w_in[:, o_b + 2 * B_QK + 2 * B_V:o_c], LANE)], axis=1)
    w_c = jnp.concatenate([w_in[:, o_c:o_c + C_Q_LORA + C_KV_LORA],
                           _pad_cols(w_in[:, o_c + C_Q_LORA + C_KV_LORA:], LANE)], axis=1)
    wuq = P['c_w_uq'][l].reshape(C_Q_LORA, C_HEADS, C_NOPE + C_ROPE)
    wuq = jnp.concatenate([wuq[:, :, :C_NOPE].reshape(C_Q_LORA, -1),
                           wuq[:, :, C_NOPE:].reshape(C_Q_LORA, -1)], axis=1)
    w_out = P['w_out'][l]
    row = lambda a: a.reshape(1, -1).astype(F32)
    head_lanes = lambda a: jnp.pad(a.reshape(1, -1).astype(F32), ((0, 0), (A_HEADS, LANE - 2 * A_HEADS)))
    return dict(
        norm_ffn1=row(P['norm_ffn1'][l]), norm_ffn2=row(P['norm_ffn2'][l]),
        ffn1=tuple((Pb[n], l) for n in ('w_ffn1_gate', 'w_ffn1_up', 'w_ffn1_down')),
        ffn2=tuple((Pb[n], l) for n in ('w_ffn2_gate', 'w_ffn2_up', 'w_ffn2_down')),
        norm_mix=row(P['norm_mix'][l]),
        w_in=(w_a.astype(BF16), w_b.astype(BF16), w_c.astype(BF16)),
        a_conv_w=P['a_conv_w'][l].astype(F32),
        a_log=head_lanes(P['a_log'][l]), a_dt_bias=head_lanes(P['a_dt_bias'][l]),
        a_norm=row(P['a_norm'][l]), a_norm_t=row(jnp.tile(P['a_norm'][l], A_HEADS)),
        b_w2=jnp.pad(P['b_gate_w2'][l], ((0, LANE - B_GATE_RANK), (0, 0))).astype(BF16),
        b_gate_bias=row(P['b_gate_bias'][l]),
        b_norm=row(P['b_norm'][l]), b_norm_t=row(jnp.tile(P['b_norm'][l], B_HEADS)),
        c_q_norm=row(P['c_q_norm'][l]), c_kv_norm=row(P['c_kv_norm'][l]),
        c_w_uq=wuq.astype(BF16),
        c_w_uk=jnp.transpose(P['c_w_uk'][l], (1, 2, 0)).astype(BF16),
        c_w_uv=jnp.transpose(P['c_w_uv'][l], (1, 0, 2)).astype(BF16),
        w_out=(w_out[:A_V].astype(BF16), w_out[A_V:A_V + B_V].astype(BF16), w_out[A_V + B_V:].astype(BF16)),
        norm_x=row(P['norm_x'][l]), norm_mem=row(P['norm_mem'][l]),
        w_xq=(Pb['w_xq'], l), w_xo=(Pb['w_xo'], l), w_xk=(Pb['w_xk'], l), w_xv=(Pb['w_xv'], l),
    )


def _pick(n, pref):
    return pref if n % pref == 0 else n


def kernel(x_prompt, x_sample, mem_prompt, cache_ckv, cache_kpe, page_table, state_conv_a, state_delta, state_gla, cache_mem_k, cache_mem_v, norm_ffn1, w_ffn1_gate, w_ffn1_up, w_ffn1_down, norm_mix, w_in, a_conv_w, a_log, a_dt_bias, a_norm, b_gate_w2, b_gate_bias, b_norm, c_q_norm, c_w_uq, c_kv_norm, c_w_uk, c_w_uv, w_out, norm_x, norm_mem, w_xq, w_xk, w_xv, w_xo, norm_ffn2, w_ffn2_gate, w_ffn2_up, w_ffn2_down, final_norm):
    P = dict(norm_ffn1=norm_ffn1, w_ffn1_gate=w_ffn1_gate, w_ffn1_up=w_ffn1_up, w_ffn1_down=w_ffn1_down,
             norm_mix=norm_mix, w_in=w_in, a_conv_w=a_conv_w, a_log=a_log, a_dt_bias=a_dt_bias,
             a_norm=a_norm, b_gate_w2=b_gate_w2, b_gate_bias=b_gate_bias, b_norm=b_norm,
             c_q_norm=c_q_norm, c_w_uq=c_w_uq, c_kv_norm=c_kv_norm, c_w_uk=c_w_uk, c_w_uv=c_w_uv,
             w_out=w_out, norm_x=norm_x, norm_mem=norm_mem, w_xq=w_xq, w_xk=w_xk, w_xv=w_xv, w_xo=w_xo,
             norm_ffn2=norm_ffn2, w_ffn2_gate=w_ffn2_gate, w_ffn2_up=w_ffn2_up, w_ffn2_down=w_ffn2_down)
    depth = w_in.shape[0]
    Pb = {n: P[n].astype(BF16) for n in STACKED_BF16}
    W = [_layer_weights(l, P, Pb) for l in range(depth)]
    fnorm = final_norm.reshape(1, -1).astype(F32)

    bp, tp, d = x_prompt.shape
    mp = bp * tp
    n_mem = mem_prompt.shape[1]
    tm = _pick(tp, 512)
    tabs_p = _rope_tables(jnp.arange(tp, dtype=jnp.int32))
    x = x_prompt.reshape(mp, d)
    mem = mem_prompt.reshape(bp * n_mem, d)
    p_ckv, p_kpe, p_conv, p_delta, p_gla, p_mk, p_mv = [], [], [], [], [], [], []
    for l in range(depth):
        w = W[l]
        mk, mv = _norm_mm(mem, w['norm_mem'], [w['w_xk'], w['w_xv']], tm=_pick(bp * n_mem, 512))
        x = _ffn(x, w['norm_ffn1'], *w['ffn1'], fnorm, final=False, tm=tm)
        x3 = x.reshape(bp, tp, d)
        w_a, w_b, w_c = w['w_in']
        o_a, conv_new, sd = _gdn_prompt(x3, w['norm_mix'], w_a, w['a_conv_w'], w['a_log'], w['a_dt_bias'],
                                        w['a_norm_t'], tblk=_pick(tp, 256), nb=_pick(bp, 2))
        o_b, sg = _gla_prompt(x3, w['norm_mix'], w_b, w['b_w2'], w['b_gate_bias'], w['b_norm_t'],
                              tblk=_pick(tp, 512))
        ckv, kpe, ckvb, kpeb, qlat, qpe = _mla_prep(x3, w['norm_mix'], w_c, w['c_q_norm'], w['c_kv_norm'],
                                                    w['c_w_uq'], w['c_w_uk'], tabs_p, tm=tm)
        o_c = _mla_flash(qlat, qpe, ckvb, kpeb, w['c_w_uv'], tq=_pick(tp, 512), tk=_pick(tp, 512))
        x = _xattn_prompt(x.reshape(bp, tp, d), [o_a, o_b, o_c], list(w['w_out']),
                          w['norm_x'], w['w_xq'], w['w_xo'],
                          mk.reshape(bp, n_mem, d).astype(BF16), mv.reshape(bp, n_mem, d).astype(BF16),
                          tm=tm).reshape(mp, d)
        x = _ffn(x, w['norm_ffn2'], *w['ffn2'], fnorm, final=(l == depth - 1), tm=tm)
        p_ckv.append(ckv)
        p_kpe.append(kpe)
        p_conv.append(conv_new)
        p_delta.append(sd.reshape(bp, A_DK, A_HEADS, A_DV).transpose(0, 2, 1, 3))
        sg = sg.reshape(bp, B_HEADS, B_DV, B_HEADS, B_DK)
        p_gla.append(jnp.stack([sg[:, h, :, h, :] for h in range(B_HEADS)], axis=1).transpose(0, 1, 3, 2))
        p_mk.append(mk.reshape(bp, n_mem, X_HEADS, X_HDIM))
        p_mv.append(mv.reshape(bp, n_mem, X_HEADS, X_HDIM))
    y_prompt = x.reshape(bp, tp, d)

    bs, ts, _ = x_sample.shape
    n_pages, page = page_table.shape[1], cache_ckv.shape[2]
    past_len = n_pages * page
    tabs_s = _rope_tables(past_len + jnp.arange(ts, dtype=jnp.int32))
    cache_kpe_t = jnp.swapaxes(cache_kpe, 2, 3)
    mem_k_rows, mem_v_rows = _mem_rows(cache_mem_k), _mem_rows(cache_mem_v)
    nb_dec, nb_mem = _pick(bs, 8), _pick(bs, 4)
    x = x_sample.reshape(bs, d)
    s_ckv, s_kpe, s_conv, s_delta, s_gla = [], [], [], [], []
    for l in range(depth):
        w = W[l]
        x = _ffn(x, w['norm_ffn1'], *w['ffn1'], fnorm, final=False, tm=bs)
        w_a, w_b, w_c = w['w_in']
        u_a, u_b = _norm_mm(x, w['norm_mix'], [w_a, w_b], tm=bs)
        o_a, conv_new, sd = _gdn_decode(u_a.reshape(bs, 1, UA_W), state_conv_a, state_delta,
                                        w['a_conv_w'], w['a_log'], w['a_dt_bias'], w['a_norm'], layer=l,
                                        nb=nb_dec)
        o_b, sg = _gla_decode(u_b.reshape(bs, 1, UB_W), state_gla, w['b_w2'], w['b_gate_bias'], w['b_norm'],
                              layer=l, nb=nb_dec)
        ckv, kpe, _, _, qlat, qpe = _mla_prep(x.reshape(1, bs, d), w['norm_mix'], w_c, w['c_q_norm'],
                                              w['c_kv_norm'], w['c_w_uq'], w['c_w_uk'], tabs_s, tm=bs)
        pad_heads = lambda a: jnp.pad(a[0].transpose(1, 0, 2), ((0, 0), (0, 8 - C_HEADS), (0, 0)))
        olat = _mla_paged(page_table, pad_heads(qlat), pad_heads(qpe), ckv.reshape(bs, 1, C_KV_LORA),
                          kpe.reshape(bs, 1, C_ROPE), cache_ckv, cache_kpe_t, layer=l, ppc=_pick(n_pages, 64))
        o_c = _mla_oproj(olat[:, :C_HEADS].transpose(1, 0, 2)[None], w['c_w_uv'], tm=bs)
        x = _mm_res(x, [o_a.reshape(bs, A_V), o_b.reshape(bs, B_V), o_c.reshape(bs, C_HEADS * C_VDIM)],
                    list(w['w_out']), tm=bs)
        (q,) = _norm_mm(x, w['norm_x'], [w['w_xq']], tm=bs)
        att = _xattn_decode(q.reshape(bs, 1, d), mem_k_rows, mem_v_rows, layer=l, nb=nb_mem)
        x = _mm_res(x, [att.reshape(bs, d)], [w['w_xo']], tm=bs)
        x = _ffn(x, w['norm_ffn2'], *w['ffn2'], fnorm, final=(l == depth - 1), tm=bs)
        s_ckv.append(ckv.reshape(bs, ts, C_KV_LORA))
        s_kpe.append(kpe.reshape(bs, ts, C_ROPE))
        s_conv.append(conv_new)
        s_delta.append(sd)
        s_gla.append(sg)
    y_sample = x.reshape(bs, ts, d)

    return (y_prompt, y_sample,
            jnp.stack(p_ckv), jnp.stack(p_kpe), jnp.stack(p_conv), jnp.stack(p_delta), jnp.stack(p_gla),
            jnp.stack(p_mk), jnp.stack(p_mv),
            jnp.stack(s_ckv), jnp.stack(s_kpe), jnp.stack(s_conv), jnp.stack(s_delta), jnp.stack(s_gla))
```

```python
import functools

import jax
import jax.numpy as jnp
import numpy as np
from jax import lax
from jax.experimental import pallas as pl
from jax.experimental.pallas import tpu as pltpu

F32 = jnp.float32
BF16 = jnp.bfloat16

D_MODEL = 1024
A_HEADS, A_DK, A_DV, CONV_W = 4, 64, 64, 4
B_HEADS, B_DK, B_DV, B_GATE_RANK, B_GATE_TAU = 4, 32, 64, 16, 16.0
C_HEADS, C_NOPE, C_ROPE, C_VDIM, C_Q_LORA, C_KV_LORA = 4, 128, 64, 128, 384, 256
ROPE_THETA = 10000.0
X_HEADS, X_HDIM = 4, 256
D_FF = 2816
CHUNK = 64
EPS = 1e-6

A_QK = A_HEADS * A_DK
A_V = A_HEADS * A_DV
A_CONV_CH = 2 * A_QK + A_V
A_IN = A_CONV_CH + A_V + 2 * A_HEADS
B_QK = B_HEADS * B_DK
B_V = B_HEADS * B_DV
B_IN = 2 * B_QK + 2 * B_V + B_GATE_RANK
C_IN = C_Q_LORA + C_KV_LORA + C_ROPE
MLA_SCALE = (C_NOPE + C_ROPE) ** -0.5

LANE = 128
UA_W = A_CONV_CH + A_V + LANE
UB_W = 2 * B_QK + 2 * B_V + LANE
UC_W = C_Q_LORA + C_KV_LORA + LANE
SUB = 16
LOG2E = 1.4426950408889634
PAGED_SLOTS = 3
VMEM_LIMIT = 56 * 1024 * 1024


def _cparams(sem):
    return pltpu.CompilerParams(dimension_semantics=sem, vmem_limit_bytes=VMEM_LIMIT)


def _const_spec(shape):
    nd = len(shape)
    return pl.BlockSpec(shape, lambda *_: (0,) * nd, pipeline_mode=pl.Buffered(1))


def _w_arr(w):
    return w[0] if isinstance(w, tuple) else w


def _w_shape(w):
    return w[0].shape[1:] if isinstance(w, tuple) else w.shape


def _w_spec(w):
    if not isinstance(w, tuple):
        return _const_spec(w.shape)
    arr, layer = w
    return pl.BlockSpec((None,) + arr.shape[1:], lambda *_: (layer,) + (0,) * (arr.ndim - 1),
                        pipeline_mode=pl.Buffered(1))


def _rms(x, w):
    return x * lax.rsqrt(jnp.mean(x * x, axis=-1, keepdims=True) + EPS) * w


def _silu(x):
    return x * jax.nn.sigmoid(x)


def _bdot(a, b):
    return jnp.dot(a.astype(BF16), b.astype(BF16), preferred_element_type=F32)


def _bdot_nt(a, b):
    return lax.dot_general(a.astype(BF16), b.astype(BF16), (((1,), (1,)), ((), ())),
                           preferred_element_type=F32)


def _bdot_tn(a, b):
    return lax.dot_general(a.astype(BF16), b.astype(BF16), (((0,), (0,)), ((), ())),
                           preferred_element_type=F32)


def _iota(shape, dim):
    return lax.broadcasted_iota(jnp.int32, shape, dim)


def _block_mask(rows, cols, rb, cb):
    return (_iota((rows, cols), 0) // rb) == (_iota((rows, cols), 1) // cb)


def _block_diag(x, mask):
    n = mask.shape[0] // x.shape[0]
    xb = x.astype(BF16)
    return jnp.where(mask, jnp.concatenate([xb] * n, axis=0), jnp.zeros_like(xb[:1, :1]))


def _diag_blocks(m, rb, cb):
    n = m.shape[0] // rb
    lane_blk = _iota((rb, m.shape[1]), 1) // cb
    out = jnp.zeros((rb, m.shape[1]), m.dtype)
    for h in range(n):
        out = jnp.where(lane_blk == h, m[h * rb:(h + 1) * rb, :], out)
    return out


def _seg_sum(x, ones_bd):
    hi = x.astype(BF16)
    lo = (x - hi.astype(F32)).astype(BF16)
    return (jnp.dot(hi, ones_bd, preferred_element_type=F32)
            + jnp.dot(lo, ones_bd, preferred_element_type=F32))


def _expand_heads(x, off, nh, w):
    lane_blk = _iota((x.shape[0], nh * w), 1) // w
    out = jnp.zeros((x.shape[0], nh * w), x.dtype)
    for h in range(nh):
        out = jnp.where(lane_blk == h, x[:, off + h:off + h + 1], out)
    return out


def _cumsum_rows(x):
    n = x.shape[0]
    row = _iota(x.shape, 0)
    s = 1
    while s < n:
        x = x + jnp.where(row >= s, pltpu.roll(x, s, axis=0), 0.0)
        s *= 2
    return x


def _ffn_kernel(x_ref, nw_ref, wg_ref, wu_ref, wd_ref, fn_ref, o_ref, *, final):
    x = x_ref[...]
    xn = _rms(x, nw_ref[...]).astype(BF16)
    g = jnp.dot(xn, wg_ref[...], preferred_element_type=F32)
    u = jnp.dot(xn, wu_ref[...], preferred_element_type=F32)
    h = (_silu(g) * u).astype(BF16)
    y = x + 0.5 * jnp.dot(h, wd_ref[...], preferred_element_type=F32)
    if final:
        y = _rms(y, fn_ref[...])
    o_ref[...] = y


def _ffn(x, nw, wg, wu, wd, fn, *, final, tm):
    m = x.shape[0]
    return pl.pallas_call(
        functools.partial(_ffn_kernel, final=final),
        grid=(m // tm,),
        in_specs=[pl.BlockSpec((tm, D_MODEL), lambda i: (i, 0)),
                  _const_spec((1, D_MODEL)),
                  _w_spec(wg), _w_spec(wu), _w_spec(wd), _const_spec((1, D_MODEL))],
        out_specs=pl.BlockSpec((tm, D_MODEL), lambda i: (i, 0)),
        out_shape=jax.ShapeDtypeStruct((m, D_MODEL), F32),
        compiler_params=_cparams(("parallel",)),
        name="ffn",
    )(x, nw, _w_arr(wg), _w_arr(wu), _w_arr(wd), fn)


def _norm_mm_kernel(x_ref, nw_ref, *refs, n_w):
    xn = _rms(x_ref[...], nw_ref[...]).astype(BF16)
    for w_ref, o_ref in zip(refs[:n_w], refs[n_w:]):
        o_ref[...] = jnp.dot(xn, w_ref[...], preferred_element_type=F32)


def _norm_mm(x, nw, ws, *, tm):
    m, d = x.shape
    return pl.pallas_call(
        functools.partial(_norm_mm_kernel, n_w=len(ws)),
        grid=(m // tm,),
        in_specs=[pl.BlockSpec((tm, d), lambda i: (i, 0)), _const_spec((1, d))]
                 + [_w_spec(w) for w in ws],
        out_specs=[pl.BlockSpec((tm, _w_shape(w)[1]), lambda i: (i, 0)) for w in ws],
        out_shape=[jax.ShapeDtypeStruct((m, _w_shape(w)[1]), F32) for w in ws],
        compiler_params=_cparams(("parallel",)),
        name="norm_mm",
    )(x, nw, *[_w_arr(w) for w in ws])


def _mm_res_kernel(x_ref, *refs, n_a):
    acc = x_ref[...]
    for a_ref, w_ref in zip(refs[:n_a], refs[n_a:2 * n_a]):
        acc = acc + jnp.dot(a_ref[...].astype(BF16), w_ref[...], preferred_element_type=F32)
    refs[2 * n_a][...] = acc


def _mm_res(x, a_list, w_list, *, tm):
    m, d = x.shape
    n_a = len(a_list)
    return pl.pallas_call(
        functools.partial(_mm_res_kernel, n_a=n_a),
        grid=(m // tm,),
        in_specs=[pl.BlockSpec((tm, d), lambda i: (i, 0))]
                 + [pl.BlockSpec((tm, a.shape[1]), lambda i: (i, 0)) for a in a_list]
                 + [_w_spec(w) for w in w_list],
        out_specs=pl.BlockSpec((tm, d), lambda i: (i, 0)),
        out_shape=jax.ShapeDtypeStruct((m, d), F32),
        compiler_params=_cparams(("parallel",)),
        name="mm_res",
    )(x, *a_list, *[_w_arr(w) for w in w_list])


def _project(x, nw_ref, win_ref):
    return jnp.dot(_rms(x, nw_ref[...]).astype(BF16), win_ref[...], preferred_element_type=F32)


def _gdn_prompt_kernel(x_ref, nw_ref, win_ref, cw_ref, alog_ref, dtb_ref, anorm_ref, o_ref, conv_ref, s_out_ref,
                       xbuf, s_scr, *, tblk, nb):
    t = pl.program_id(1)
    hw = A_HEADS * A_DV
    c = CHUNK
    nc = tblk // c

    @pl.when(t == 0)
    def _():
        for bi in range(nb):
            xbuf[bi, 0:8, :] = jnp.zeros((8, A_CONV_CH), F32)
        s_scr[...] = jnp.zeros_like(s_scr)

    cw = cw_ref[...]
    ones_bd = _block_mask(hw, hw, A_DK, A_DK).astype(BF16)
    xs, zs, qs, ks, vs, betas, gs = [], [], [], [], [], [], []
    for bi in range(nb):
        u = _project(x_ref[bi], nw_ref, win_ref)
        x = u[:, 0:A_CONV_CH]
        xbuf[bi, 8:8 + tblk, :] = x
        y = (cw[0:1] * xbuf[bi, 5:5 + tblk, :] + cw[1:2] * xbuf[bi, 6:6 + tblk, :]
             + cw[2:3] * xbuf[bi, 7:7 + tblk, :] + cw[3:4] * x)
        xbuf[bi, 0:8, :] = x[tblk - 8:tblk, :]
        qkv = _silu(y)
        ba = u[:, A_CONV_CH + A_V:UA_W]
        q = qkv[:, 0:A_QK]
        k = qkv[:, A_QK:2 * A_QK]
        xs.append(x)
        zs.append(u[:, A_CONV_CH:A_CONV_CH + A_V])
        qs.append(q * lax.rsqrt(_seg_sum(q * q, ones_bd) + EPS) * (A_DK ** -0.5))
        ks.append(k * lax.rsqrt(_seg_sum(k * k, ones_bd) + EPS))
        vs.append(qkv[:, 2 * A_QK:])
        betas.append(_expand_heads(jax.nn.sigmoid(ba), 0, A_HEADS, A_DV))
        gs.append(_expand_heads(-jnp.exp(alog_ref[...]) * jax.nn.softplus(ba + dtb_ref[...]),
                                A_HEADS, A_HEADS, A_DV))

    bd = _block_mask(hw, hw, c, c)
    row = _iota((c, hw), 0)
    col = _iota((c, hw), 1) % c
    incl = row >= col
    strict = row > col
    items = [(ci, bi) for ci in range(nc) for bi in range(nb)]
    n_it = range(len(items))

    def chunks(arrs):
        return [arrs[bi][ci * c:(ci + 1) * c] for ci, bi in items]

    qc, kc, vc, bc = chunks(qs), chunks(ks), chunks(vs), chunks(betas)
    gcum = [_cumsum_rows(gi) for gi in chunks(gs)]
    grow = [jnp.sum(jnp.where(row == col, gi, 0.0), axis=0, keepdims=True) for gi in gcum]
    decay = [jnp.exp(jnp.where(incl, gcum[n] - grow[n], -jnp.inf)) for n in n_it]
    eg = [jnp.exp(gi) for gi in gcum]
    kb = [kc[n] * bc[n] for n in n_it]
    aq = [_bdot_nt(jnp.concatenate([kb[n], qc[n]], axis=0), _block_diag(kc[n], bd)) for n in n_it]
    a = [jnp.where(strict, aq[n][:c] * decay[n], 0.0) for n in n_it]
    qk = [aq[n][c:] * decay[n] for n in n_it]
    p = [-ai for ai in a]
    pw = [_bdot(ai, _block_diag(ai, bd)) for ai in a]
    n_sq = int(np.log2(c)) - 1
    for r in range(n_sq):
        if r < n_sq - 1:
            both = [_bdot(jnp.concatenate([p[n], pw[n]], axis=0), _block_diag(pw[n], bd)) for n in n_it]
            p = [p[n] + pw[n] + both[n][:c] for n in n_it]
            pw = [both[n][c:] for n in n_it]
        else:
            p = [p[n] + pw[n] + _bdot(p[n], _block_diag(pw[n], bd)) for n in n_it]
    vb = [vc[n] * bc[n] for n in n_it]
    kbg = [kb[n] * eg[n] for n in n_it]
    uw = [_bdot(p[n], jnp.concatenate([_block_diag(vb[n], bd), _block_diag(kbg[n], bd)], axis=1)) for n in n_it]
    u = [vb[n] + uw[n][:, :hw] for n in n_it]
    wq = [jnp.concatenate([kbg[n] + uw[n][:, hw:], qc[n] * eg[n]], axis=0) for n in n_it]
    g_last = [gi[c - 1:c, :] for gi in gcum]
    kd = [kc[n] * jnp.exp(g_last[n] - gcum[n]) for n in n_it]
    eg_last = [jnp.exp(gl) for gl in g_last]

    s = [s_scr[bi] for bi in range(nb)]
    outs = [[] for _ in range(nb)]
    for n, (ci, bi) in enumerate(items):
        ws = _bdot(wq[n], _block_diag(s[bi], bd))
        r = u[n] - ws[:c]
        outs[bi].append(ws[c:] + _bdot(qk[n], _block_diag(r, bd)))
        s[bi] = eg_last[n] * s[bi] + _diag_blocks(_bdot_tn(kd[n], r), A_DK, A_DV)

    for bi in range(nb):
        s_scr[bi] = s[bi]
        o = jnp.concatenate(outs[bi], axis=0)
        o = o * lax.rsqrt(_seg_sum(o * o, ones_bd) * (1.0 / A_DV) + EPS) * anorm_ref[...]
        o_ref[bi] = o * _silu(zs[bi])

    @pl.when(t == pl.num_programs(1) - 1)
    def _():
        for bi in range(nb):
            conv_ref[bi] = xs[bi][tblk - (CONV_W - 1):tblk, :]
            s_out_ref[bi] = s[bi]


def _gdn_prompt(x, nw, w_a, cw, alog, dtb, anorm, *, tblk, nb):
    b, t, d = x.shape
    hw = A_HEADS * A_DV
    return pl.pallas_call(
        functools.partial(_gdn_prompt_kernel, tblk=tblk, nb=nb),
        grid=(b // nb, t // tblk),
        in_specs=[pl.BlockSpec((nb, tblk, d), lambda i, j: (i, j, 0)),
                  _const_spec((1, d)), _const_spec((d, UA_W)),
                  _const_spec((CONV_W, A_CONV_CH)), _const_spec((1, LANE)), _const_spec((1, LANE)),
                  _const_spec((1, hw))],
        out_specs=[pl.BlockSpec((nb, tblk, hw), lambda i, j: (i, j, 0)),
                   pl.BlockSpec((nb, CONV_W - 1, A_CONV_CH), lambda i, j: (i, 0, 0)),
                   pl.BlockSpec((nb, A_DK, hw), lambda i, j: (i, 0, 0))],
        out_shape=[jax.ShapeDtypeStruct((b, t, hw), F32),
                   jax.ShapeDtypeStruct((b, CONV_W - 1, A_CONV_CH), F32),
                   jax.ShapeDtypeStruct((b, A_DK, hw), F32)],
        scratch_shapes=[pltpu.VMEM((nb, tblk + 8, A_CONV_CH), F32), pltpu.VMEM((nb, A_DK, hw), F32)],
        compiler_params=_cparams(("parallel", "arbitrary")),
        name="gdn_prompt",
    )(x, nw, w_a, cw, alog, dtb, anorm)


def _gla_prompt_kernel(x_ref, nw_ref, win_ref, w2_ref, gb_ref, bnorm_ref, o_ref, s_out_ref, s_scr, *, tblk):
    t = pl.program_id(1)
    vw = B_HEADS * B_DV

    @pl.when(t == 0)
    def _():
        s_scr[...] = jnp.zeros_like(s_scr)

    u = _project(x_ref[0], nw_ref, win_ref)
    q = u[:, 0:B_QK] * (B_DK ** -0.5)
    k = u[:, B_QK:2 * B_QK]
    v = u[:, 2 * B_QK:2 * B_QK + B_V]
    rg = u[:, 2 * B_QK + B_V:2 * B_QK + 2 * B_V]
    glr = u[:, 2 * B_QK + 2 * B_V:]
    log_a = jax.nn.log_sigmoid(_bdot(glr, w2_ref[...]) + gb_ref[...]) / B_GATE_TAU

    c = CHUNK
    kmask = _block_mask(vw, B_QK, c, B_DK)
    vmask = _block_mask(vw, vw, c, B_DV)
    smask = _block_mask(vw, B_QK, B_DV, B_DK)
    row_k = _iota((c, B_QK), 0)
    row_s = _iota((SUB, vw), 0)
    col_s = _iota((SUB, vw), 1) % c
    st = s_scr[...]
    outs = []
    for ci in range(tblk // c):
        sl = slice(ci * c, (ci + 1) * c)
        qc, kc, vc = q[sl], k[sl], v[sl]
        b = _cumsum_rows(log_a[sl])
        o = _bdot_nt(qc * jnp.exp(b), st)
        att = []
        for i in range(c // SUB):
            r0 = i * SUB
            bref = b[r0:r0 + 1, :]
            qs = qc[r0:r0 + SUB] * jnp.exp(b[r0:r0 + SUB] - bref)
            ks = kc * jnp.exp(jnp.where(row_k < r0 + SUB, bref - b, -jnp.inf))
            a_i = _bdot_nt(qs, _block_diag(ks, kmask))
            att.append(jnp.where(row_s + r0 >= col_s, a_i, 0.0))
        o = o + _bdot(jnp.concatenate(att, axis=0), _block_diag(vc, vmask))
        outs.append(o)
        b_last = b[c - 1:c, :]
        kd = kc * jnp.exp(b_last - b)
        st = jnp.exp(b_last) * st + jnp.where(smask, _bdot_tn(vc, kd), 0.0)
    s_scr[...] = st

    o = jnp.concatenate(outs, axis=0)
    ones_bd = _block_mask(vw, vw, B_DV, B_DV).astype(BF16)
    o = o * lax.rsqrt(_seg_sum(o * o, ones_bd) * (1.0 / B_DV) + EPS) * bnorm_ref[...]
    o_ref[0] = o * _silu(rg)

    @pl.when(t == pl.num_programs(1) - 1)
    def _():
        s_out_ref[0] = st


def _gla_prompt(x, nw, w_b, w2, gb, bnorm, *, tblk):
    b, t, d = x.shape
    vw = B_HEADS * B_DV
    return pl.pallas_call(
        functools.partial(_gla_prompt_kernel, tblk=tblk),
        grid=(b, t // tblk),
        in_specs=[pl.BlockSpec((1, tblk, d), lambda i, j: (i, j, 0)),
                  _const_spec((1, d)), _const_spec((d, UB_W)),
                  _const_spec((LANE, B_QK)), _const_spec((1, B_QK)), _const_spec((1, vw))],
        out_specs=[pl.BlockSpec((1, tblk, vw), lambda i, j: (i, j, 0)),
                   pl.BlockSpec((1, vw, B_QK), lambda i, j: (i, 0, 0))],
        out_shape=[jax.ShapeDtypeStruct((b, t, vw), F32),
                   jax.ShapeDtypeStruct((b, vw, B_QK), F32)],
        scratch_shapes=[pltpu.VMEM((vw, B_QK), F32)],
        compiler_params=_cparams(("parallel", "arbitrary")),
        name="gla_prompt",
    )(x, nw, w_b, w2, gb, bnorm)


def _mla_prep_kernel(x_ref, nw_ref, win_ref, cqn_ref, ckvn_ref, wuq_ref, wuk_ref, ck_ref, sk_ref, cq_ref, sq_ref,
                     ckv_o, kpe_o, ckvb_o, kpeb_o, qlat_o, qpe_o):
    u = _project(x_ref[0], nw_ref, win_ref)
    cq = _rms(u[:, 0:C_Q_LORA], cqn_ref[...]).astype(BF16)
    ckv = _rms(u[:, C_Q_LORA:C_Q_LORA + C_KV_LORA], ckvn_ref[...])
    half = C_ROPE // 2
    kx = u[:, C_Q_LORA + C_KV_LORA:]
    lane = _iota(kx.shape, 1)
    kswap = jnp.where(lane < half, pltpu.roll(kx, LANE - half, axis=1), pltpu.roll(kx, half, axis=1))
    kpe = (kx * ck_ref[...] + kswap * sk_ref[...])[:, 0:C_ROPE]
    ckv_o[0] = ckv
    kpe_o[0] = kpe
    ckvb_o[0] = ckv.astype(BF16)
    kpeb_o[0] = kpe.astype(BF16)

    qf = jnp.dot(cq, wuq_ref[...], preferred_element_type=F32)
    nope_w = C_HEADS * C_NOPE
    rope_w = C_HEADS * C_ROPE
    qr = qf[:, nope_w:]
    lane_r = _iota(qr.shape, 1) % C_ROPE
    qswap = jnp.where(lane_r < half, pltpu.roll(qr, rope_w - half, axis=1), pltpu.roll(qr, half, axis=1))
    qpe = qr * cq_ref[...] + qswap * sq_ref[...]
    for h in range(C_HEADS):
        qlat_o[0, h] = jnp.dot(qf[:, h * C_NOPE:(h + 1) * C_NOPE].astype(BF16), wuk_ref[h],
                               preferred_element_type=F32).astype(BF16)
        qpe_o[0, h] = qpe[:, h * C_ROPE:(h + 1) * C_ROPE].astype(BF16)


def _mla_prep(x, nw, w_c, cqn, ckvn, wuq, wuk, tabs, *, tm):
    b, t, d = x.shape
    ck, sk, cq, sq = tabs
    per_pos = ck.shape[0] != 1
    rope_w = C_HEADS * C_ROPE

    def tab_spec(w):
        if per_pos:
            return pl.BlockSpec((tm, w), lambda i, j: (j, 0))
        return _const_spec((1, w))

    def tok_spec(w):
        return pl.BlockSpec((1, tm, w), lambda i, j: (i, j, 0))

    def head_spec(w):
        return pl.BlockSpec((1, C_HEADS, tm, w), lambda i, j: (i, 0, j, 0))

    return pl.pallas_call(
        _mla_prep_kernel,
        grid=(b, t // tm),
        in_specs=[tok_spec(d), _const_spec((1, d)), _const_spec((d, UC_W)),
                  _const_spec((1, C_Q_LORA)), _const_spec((1, C_KV_LORA)),
                  _const_spec(wuq.shape), _const_spec(wuk.shape),
                  tab_spec(LANE), tab_spec(LANE), tab_spec(rope_w), tab_spec(rope_w)],
        out_specs=[tok_spec(C_KV_LORA), tok_spec(C_ROPE), tok_spec(C_KV_LORA), tok_spec(C_ROPE),
                   head_spec(C_KV_LORA), head_spec(C_ROPE)],
        out_shape=[jax.ShapeDtypeStruct((b, t, C_KV_LORA), F32),
                   jax.ShapeDtypeStruct((b, t, C_ROPE), F32),
                   jax.ShapeDtypeStruct((b, t, C_KV_LORA), BF16),
                   jax.ShapeDtypeStruct((b, t, C_ROPE), BF16),
                   jax.ShapeDtypeStruct((b, C_HEADS, t, C_KV_LORA), BF16),
                   jax.ShapeDtypeStruct((b, C_HEADS, t, C_ROPE), BF16)],
        compiler_params=_cparams(("parallel", "parallel")),
        name="mla_prep",
    )(x, nw, w_c, cqn, ckvn, wuq, wuk, ck, sk, cq, sq)


def _mla_flash_kernel(qi_ref, kj_ref, ql_ref, qp_ref, k_ref, p_ref, wuv_ref, o_ref,
                      m_scr, l_scr, acc_scr, *, tq, tk):
    g = pl.program_id(1)
    i = qi_ref[g]
    j = kj_ref[g]
    last_j = (i * tq + tq - 1) // tk
    c2 = MLA_SCALE * LOG2E

    @pl.when(j == 0)
    def _():
        m_scr[...] = jnp.full_like(m_scr, -jnp.inf)
        l_scr[...] = jnp.zeros_like(l_scr)
        acc_scr[...] = jnp.zeros_like(acc_scr)

    def step(masked):
        kv = k_ref[0]
        pe = p_ref[0]
        if masked and tq == tk:
            parts = [(0, tq // 2, tk // 2), (tq // 2, tq, tk)]
        else:
            parts = [(0, tq, tk)]
        for h in range(C_HEADS):
            for r0, r1, nk in parts:
                s = (lax.dot_general(ql_ref[0, h, r0:r1, :], kv[:nk], (((1,), (1,)), ((), ())),
                                     preferred_element_type=F32)
                     + lax.dot_general(qp_ref[0, h, r0:r1, :], pe[:nk], (((1,), (1,)), ((), ())),
                                       preferred_element_type=F32))
                if masked:
                    visible = j * tk + _iota((r1 - r0, nk), 1) <= i * tq + r0 + _iota((r1 - r0, nk), 0)
                    s = jnp.where(visible, s, -jnp.inf)
                m_old = m_scr[h, r0:r1, :]
                m_new = jnp.maximum(m_old, jnp.max(s, axis=-1, keepdims=True))
                alpha = jnp.exp2((m_old - m_new) * c2)
                p = jnp.exp2(s * c2 - m_new * c2)
                l_scr[h, r0:r1, :] = alpha * l_scr[h, r0:r1, :] + jnp.sum(p, axis=-1, keepdims=True)
                m_scr[h, r0:r1, :] = m_new
                acc_scr[h, r0:r1, :] = alpha * acc_scr[h, r0:r1, :] + jnp.dot(
                    p.astype(BF16), kv[:nk], preferred_element_type=F32)

    has_masked = j * tk + tk - 1 > i * tq

    @pl.when(has_masked)
    def _():
        step(True)

    @pl.when(jnp.logical_not(has_masked))
    def _():
        step(False)

    @pl.when(j == last_j)
    def _():
        o_ref[0] = jnp.concatenate(
            [jnp.dot((acc_scr[h] / l_scr[h]).astype(BF16), wuv_ref[h], preferred_element_type=F32)
             for h in range(C_HEADS)], axis=-1)


def _mla_flash(qlat, qpe, ckvb, kpeb, wuv, *, tq, tk):
    b, _, t, _ = qlat.shape
    pairs = [(i, j) for i in range(t // tq) for j in range((i * tq + tq - 1) // tk + 1)]
    qi = jnp.asarray([p[0] for p in pairs], jnp.int32)
    kj = jnp.asarray([p[1] for p in pairs], jnp.int32)
    grid_spec = pltpu.PrefetchScalarGridSpec(
        num_scalar_prefetch=2,
        grid=(b, len(pairs)),
        in_specs=[pl.BlockSpec((1, C_HEADS, tq, C_KV_LORA), lambda bi, g, qi, kj: (bi, 0, qi[g], 0)),
                  pl.BlockSpec((1, C_HEADS, tq, C_ROPE), lambda bi, g, qi, kj: (bi, 0, qi[g], 0)),
                  pl.BlockSpec((1, tk, C_KV_LORA), lambda bi, g, qi, kj: (bi, kj[g], 0)),
                  pl.BlockSpec((1, tk, C_ROPE), lambda bi, g, qi, kj: (bi, kj[g], 0)),
                  _const_spec(wuv.shape)],
        out_specs=pl.BlockSpec((1, tq, C_HEADS * C_VDIM), lambda bi, g, qi, kj: (bi, qi[g], 0)),
        scratch_shapes=[pltpu.VMEM((C_HEADS, tq, 1), F32), pltpu.VMEM((C_HEADS, tq, 1), F32),
                        pltpu.VMEM((C_HEADS, tq, C_KV_LORA), F32)],
    )
    return pl.pallas_call(
        functools.partial(_mla_flash_kernel, tq=tq, tk=tk),
        grid_spec=grid_spec,
        out_shape=jax.ShapeDtypeStruct((b, t, C_HEADS * C_VDIM), F32),
        compiler_params=_cparams(("parallel", "arbitrary")),
        name="mla_flash",
    )(qi, kj, qlat, qpe, ckvb, kpeb, wuv)


def _mla_oproj_kernel(ol_ref, wuv_ref, o_ref):
    o_ref[0] = jnp.concatenate(
        [jnp.dot(ol_ref[0, h].astype(BF16), wuv_ref[h], preferred_element_type=F32)
         for h in range(C_HEADS)], axis=-1)


def _mla_oproj(olat, wuv, *, tm):
    b, _, t, _ = olat.shape
    return pl.pallas_call(
        _mla_oproj_kernel,
        grid=(b, t // tm),
        in_specs=[pl.BlockSpec((1, C_HEADS, tm, C_KV_LORA), lambda i, j: (i, 0, j, 0)),
                  _const_spec(wuv.shape)],
        out_specs=pl.BlockSpec((1, tm, C_HEADS * C_VDIM), lambda i, j: (i, j, 0)),
        out_shape=jax.ShapeDtypeStruct((b, t, C_HEADS * C_VDIM), F32),
        compiler_params=_cparams(("parallel", "parallel")),
        name="mla_oproj",
    )(olat, wuv)


def _mla_paged_kernel(pt_ref, ql_ref, qp_ref, cn_ref, pn_ref, ckv_hbm, kpe_hbm, o_ref,
                      kbuf, pbuf, sem, m_scr, l_scr, acc_scr, *, layer, ppc, nch, page):
    b = pl.program_id(0)
    c = pl.program_id(1)
    g = b * nch + c
    total = pl.num_programs(0) * nch
    slot = g % PAGED_SLOTS

    def chunk_of(step):
        st = jnp.minimum(step, total - 1)
        return st // nch, st % nch

    def copies(pid, sl, p):
        dst = pl.ds(p * page, page)
        return (pltpu.make_async_copy(ckv_hbm.at[layer, pid], kbuf.at[sl, dst, :], sem.at[0, sl]),
                pltpu.make_async_copy(kpe_hbm.at[layer, pid], pbuf.at[sl, :, dst], sem.at[1, sl]))

    def issue(bb, cc, sl):
        for p in range(ppc):
            for prio, cp in enumerate(copies(pt_ref[bb, cc * ppc + p], sl, p)):
                cp.start(priority=prio)

    def wait_all(sl):
        for p in range(ppc):
            for cp in copies(0, sl, p):
                cp.wait()

    @pl.when(g == 0)
    def _():
        for ahead in range(PAGED_SLOTS - 1):
            issue(*chunk_of(ahead), ahead)

    @pl.when(c == 0)
    def _():
        m_scr[...] = jnp.full_like(m_scr, -jnp.inf)
        l_scr[...] = jnp.zeros_like(l_scr)
        acc_scr[...] = jnp.zeros_like(acc_scr)

    wait_all(slot)
    issue(*chunk_of(g + PAGED_SLOTS - 1), (g + PAGED_SLOTS - 1) % PAGED_SLOTS)

    ql = ql_ref[0]
    qp = qp_ref[0]
    kv = kbuf[slot].astype(BF16)
    pe = pbuf[slot].astype(BF16)
    s = (lax.dot_general(ql, kv, (((1,), (1,)), ((), ())), preferred_element_type=F32)
         + jnp.dot(qp, pe, preferred_element_type=F32)) * MLA_SCALE
    m_old = m_scr[...]
    m_new = jnp.maximum(m_old, jnp.max(s, axis=-1, keepdims=True))
    alpha = jnp.exp(m_old - m_new)
    p = jnp.exp(s - m_new)
    l_new = alpha * l_scr[...] + jnp.sum(p, axis=-1, keepdims=True)
    acc_new = alpha * acc_scr[...] + jnp.dot(p.astype(BF16), kv, preferred_element_type=F32)
    m_scr[...] = m_new
    l_scr[...] = l_new
    acc_scr[...] = acc_new

    @pl.when(c == nch - 1)
    def _():
        cn = cn_ref[0].astype(BF16).astype(F32)
        pn = pn_ref[0].astype(BF16).astype(F32)
        s_n = (jnp.sum(ql.astype(F32) * cn, axis=-1, keepdims=True)
               + jnp.sum(qp.astype(F32) * pn, axis=-1, keepdims=True)) * MLA_SCALE
        m_f = jnp.maximum(m_new, s_n)
        a_f = jnp.exp(m_new - m_f)
        p_n = jnp.exp(s_n - m_f)
        l_f = a_f * l_new + p_n
        acc_f = a_f * acc_new + p_n.astype(BF16).astype(F32) * cn
        o_ref[0] = acc_f / l_f

    @pl.when(g == total - 1)
    def _():
        for ahead in range(1, PAGED_SLOTS):
            wait_all((g + ahead) % PAGED_SLOTS)


def _mla_paged(page_table, qlat, qpe, ckv_new, kpe_new, cache_ckv, cache_kpe, *, layer, ppc):
    b, hp, _ = qlat.shape
    n_pages = page_table.shape[1]
    page = cache_ckv.shape[2]
    nch = n_pages // ppc
    grid_spec = pltpu.PrefetchScalarGridSpec(
        num_scalar_prefetch=1,
        grid=(b, nch),
        in_specs=[pl.BlockSpec((1, hp, C_KV_LORA), lambda i, j, pt: (i, 0, 0)),
                  pl.BlockSpec((1, hp, C_ROPE), lambda i, j, pt: (i, 0, 0)),
                  pl.BlockSpec((1, 1, C_KV_LORA), lambda i, j, pt: (i, 0, 0)),
                  pl.BlockSpec((1, 1, C_ROPE), lambda i, j, pt: (i, 0, 0)),
                  pl.BlockSpec(memory_space=pl.ANY),
                  pl.BlockSpec(memory_space=pl.ANY)],
        out_specs=pl.BlockSpec((1, hp, C_KV_LORA), lambda i, j, pt: (i, 0, 0)),
        scratch_shapes=[pltpu.VMEM((PAGED_SLOTS, ppc * page, C_KV_LORA), F32),
                        pltpu.VMEM((PAGED_SLOTS, C_ROPE, ppc * page), F32),
                        pltpu.SemaphoreType.DMA((2, PAGED_SLOTS)),
                        pltpu.VMEM((hp, 1), F32), pltpu.VMEM((hp, 1), F32),
                        pltpu.VMEM((hp, C_KV_LORA), F32)],
    )
    return pl.pallas_call(
        functools.partial(_mla_paged_kernel, layer=layer, ppc=ppc, nch=nch, page=page),
        grid_spec=grid_spec,
        out_shape=jax.ShapeDtypeStruct((b, hp, C_KV_LORA), F32),
        compiler_params=_cparams(("arbitrary", "arbitrary")),
        name="mla_paged",
    )(page_table, qlat, qpe, ckv_new, kpe_new, cache_ckv, cache_kpe)


def _xattn_prompt_kernel(x_ref, oa_ref, ob_ref, oc_ref, wa_ref, wb_ref, wc_ref,
                         nw_ref, wq_ref, wo_ref, mk_ref, mv_ref, o_ref):
    x = x_ref[0]
    for a_ref, w_ref in ((oa_ref, wa_ref), (ob_ref, wb_ref), (oc_ref, wc_ref)):
        x = x + jnp.dot(a_ref[0].astype(BF16), w_ref[...], preferred_element_type=F32)
    xn = _rms(x, nw_ref[...]).astype(BF16)
    q = jnp.dot(xn, wq_ref[...], preferred_element_type=F32).astype(BF16)
    heads = []
    for h in range(X_HEADS):
        sl = slice(h * X_HDIM, (h + 1) * X_HDIM)
        s = lax.dot_general(q[:, sl], mk_ref[0, :, sl], (((1,), (1,)), ((), ())),
                            preferred_element_type=F32) * (X_HDIM ** -0.5)
        e = jnp.exp(s - jnp.max(s, axis=-1, keepdims=True))
        p = (e / jnp.sum(e, axis=-1, keepdims=True)).astype(BF16)
        heads.append(jnp.dot(p, mv_ref[0, :, sl], preferred_element_type=F32).astype(BF16))
    o_ref[0] = x + jnp.dot(jnp.concatenate(heads, axis=-1), wo_ref[...], preferred_element_type=F32)


def _xattn_prompt(x, o_list, w_out_list, nw, wq, wo, mk, mv, *, tm):
    b, t, d = x.shape
    n_mem = mk.shape[1]
    return pl.pallas_call(
        _xattn_prompt_kernel,
        grid=(b, t // tm),
        in_specs=[pl.BlockSpec((1, tm, d), lambda i, j: (i, j, 0))]
                 + [pl.BlockSpec((1, tm, o.shape[2]), lambda i, j: (i, j, 0)) for o in o_list]
                 + [_w_spec(w) for w in w_out_list]
                 + [_const_spec((1, d)), _w_spec(wq), _w_spec(wo),
                  pl.BlockSpec((1, n_mem, d), lambda i, j: (i, 0, 0)),
                  pl.BlockSpec((1, n_mem, d), lambda i, j: (i, 0, 0))],
        out_specs=pl.BlockSpec((1, tm, d), lambda i, j: (i, j, 0)),
        out_shape=jax.ShapeDtypeStruct((b, t, d), F32),
        compiler_params=_cparams(("parallel", "parallel")),
        name="xattn_prompt",
    )(x, *o_list, *[_w_arr(w) for w in w_out_list], nw, _w_arr(wq), _w_arr(wo), mk, mv)


def _xattn_decode_kernel(q_ref, mk_ref, mv_ref, o_ref):
    for bi in range(q_ref.shape[0]):
        _xattn_decode_one(q_ref, mk_ref, mv_ref, o_ref, bi)


def _xattn_decode_one(q_ref, mk_ref, mv_ref, o_ref, bi):
    nt = X_HDIM // LANE
    grp = nt * X_HEADS
    n_rows = mk_ref.shape[2]
    qrow = q_ref[bi]
    qm = jnp.concatenate([qrow[:, h * X_HDIM + t * LANE:h * X_HDIM + (t + 1) * LANE]
                          for t in range(nt) for h in range(X_HEADS)], axis=0)
    sel = _iota((grp, n_rows), 0) == _iota((grp, n_rows), 1) % grp
    s_all = _bdot_nt(qm, mk_ref[0, bi])
    part = jnp.sum(jnp.where(sel, s_all, 0.0), axis=0, keepdims=True)
    lane = _iota((1, n_rows), 1)
    s = part
    for t in range(1, nt):
        s = s + jnp.where(lane % grp < X_HEADS, pltpu.roll(part, n_rows - t * X_HEADS, axis=1),
                          pltpu.roll(part, t * X_HEADS, axis=1))
    s = s * (X_HDIM ** -0.5)
    p = jnp.zeros_like(s)
    for h in range(X_HEADS):
        mine = lane % X_HEADS == h
        e = jnp.exp(s - jnp.max(jnp.where(mine, s, -jnp.inf), axis=-1, keepdims=True))
        den = jnp.sum(jnp.where(lane % grp == h, e, 0.0), axis=-1, keepdims=True)
        p = jnp.where(mine, e / den, p)
    o = _bdot(jnp.where(sel, p, 0.0), mv_ref[0, bi])
    o_ref[bi] = jnp.concatenate([o[t * X_HEADS + h:t * X_HEADS + h + 1]
                                for h in range(X_HEADS) for t in range(nt)], axis=-1)


def _mem_rows(mem):
    dep, b, n_mem, nh, hd = mem.shape
    nt = hd // LANE
    return mem.reshape(dep, b, n_mem, nh, nt, LANE).transpose(0, 1, 2, 4, 3, 5).reshape(dep, b, n_mem * nt * nh, LANE)


def _xattn_decode(q, mk_rows, mv_rows, *, layer, nb):
    _, b, n_rows, _ = mk_rows.shape
    d = X_HEADS * X_HDIM
    assert X_HDIM == 2 * LANE
    mem_spec = pl.BlockSpec((1, nb, n_rows, LANE), lambda i: (layer, i, 0, 0))
    return pl.pallas_call(
        _xattn_decode_kernel,
        grid=(b // nb,),
        in_specs=[pl.BlockSpec((nb, 1, d), lambda i: (i, 0, 0)), mem_spec, mem_spec],
        out_specs=pl.BlockSpec((nb, 1, d), lambda i: (i, 0, 0)),
        out_shape=jax.ShapeDtypeStruct((b, 1, d), F32),
        compiler_params=_cparams(("parallel",)),
        name="xattn_decode",
    )(q, mk_rows, mv_rows)


def _columns(rows):
    w = rows[0].shape[1]
    assert len(rows) <= LANE
    pad = [jnp.zeros((LANE - len(rows), w), F32)] if len(rows) < LANE else []
    return jnp.concatenate(rows + pad, axis=0).T


def _gdn_decode_kernel(u_ref, cs_ref, s_ref, cw_ref, alog_ref, dtb_ref, anorm_ref,
                       o_ref, cs_out_ref, s_out_ref):
    seqs = range(u_ref.shape[0])
    heads = range(A_HEADS)
    items = [(bi, h) for bi in seqs for h in heads]
    cw = cw_ref[...]
    x = [u_ref[bi, :, 0:A_CONV_CH] for bi in seqs]
    z = [u_ref[bi, :, A_CONV_CH:A_CONV_CH + A_V] for bi in seqs]
    ba = [u_ref[bi, :, A_CONV_CH + A_V:UA_W] for bi in seqs]
    cs = [cs_ref[0, bi] for bi in seqs]
    qkv = [_silu(cw[0:1] * cs[bi][0:1] + cw[1:2] * cs[bi][1:2] + cw[2:3] * cs[bi][2:3] + cw[3:4] * x[bi])
           for bi in seqs]
    for bi in seqs:
        cs_out_ref[bi] = jnp.concatenate([cs[bi][1:CONV_W - 1], x[bi]], axis=0)
    beta = [jax.nn.sigmoid(b) for b in ba]
    eg_all = [jnp.exp(-jnp.exp(alog_ref[...]) * jax.nn.softplus(b + dtb_ref[...])) for b in ba]
    q_raw = [qkv[bi][:, h * A_DK:(h + 1) * A_DK] for bi, h in items]
    k_raw = [qkv[bi][:, A_QK + h * A_DK:A_QK + (h + 1) * A_DK] for bi, h in items]
    v = [qkv[bi][:, 2 * A_QK + h * A_DV:2 * A_QK + (h + 1) * A_DV] for bi, h in items]
    q = [a * lax.rsqrt(jnp.sum(a * a, axis=-1, keepdims=True) + EPS) * (A_DK ** -0.5) for a in q_raw]
    k = [a * lax.rsqrt(jnp.sum(a * a, axis=-1, keepdims=True) + EPS) for a in k_raw]
    cols = _columns(k + q)
    kcol = [cols[:, n:n + 1] for n in range(len(items))]
    qcol = [cols[:, len(items) + n:len(items) + n + 1] for n in range(len(items))]
    s = [s_ref[0, bi, h] for bi, h in items]
    bh = [beta[bi][:, h:h + 1] for bi, h in items]
    eg = [eg_all[bi][:, A_HEADS + h:A_HEADS + h + 1] for bi, h in items]
    ks_row = [jnp.sum(kcol[n] * s[n], axis=0, keepdims=True) for n in range(len(items))]
    qs_row = [jnp.sum(qcol[n] * s[n], axis=0, keepdims=True) for n in range(len(items))]
    r = [v[n] * bh[n] - (bh[n] * eg[n]) * ks_row[n] for n in range(len(items))]
    qk = [jnp.sum(q[n] * k[n], axis=-1, keepdims=True) for n in range(len(items))]
    o = [eg[n] * qs_row[n] + qk[n] * r[n] for n in range(len(items))]
    for n, (bi, h) in enumerate(items):
        s_out_ref[bi, h] = eg[n] * s[n] + kcol[n] * r[n]
    o = [a * lax.rsqrt(jnp.mean(a * a, axis=-1, keepdims=True) + EPS) * anorm_ref[...] for a in o]
    for bi in seqs:
        o_ref[bi] = jnp.concatenate(o[bi * A_HEADS:(bi + 1) * A_HEADS], axis=-1) * _silu(z[bi])


def _gdn_decode(u_a, cs, s, cw, alog, dtb, anorm, *, layer, nb):
    b = u_a.shape[0]
    return pl.pallas_call(
        _gdn_decode_kernel,
        grid=(b // nb,),
        in_specs=[pl.BlockSpec((nb, 1, UA_W), lambda i: (i, 0, 0)),
                  pl.BlockSpec((1, nb, CONV_W - 1, A_CONV_CH), lambda i: (layer, i, 0, 0)),
                  pl.BlockSpec((1, nb, A_HEADS, A_DK, A_DV), lambda i: (layer, i, 0, 0, 0)),
                  _const_spec((CONV_W, A_CONV_CH)), _const_spec((1, LANE)), _const_spec((1, LANE)),
                  _const_spec((1, A_DV))],
        out_specs=[pl.BlockSpec((nb, 1, A_V), lambda i: (i, 0, 0)),
                   pl.BlockSpec((nb, CONV_W - 1, A_CONV_CH), lambda i: (i, 0, 0)),
                   pl.BlockSpec((nb, A_HEADS, A_DK, A_DV), lambda i: (i, 0, 0, 0))],
        out_shape=[jax.ShapeDtypeStruct((b, 1, A_V), F32),
                   jax.ShapeDtypeStruct((b, CONV_W - 1, A_CONV_CH), F32),
                   jax.ShapeDtypeStruct((b, A_HEADS, A_DK, A_DV), F32)],
        compiler_params=_cparams(("parallel",)),
        name="gdn_decode",
    )(u_a, cs, s, cw, alog, dtb, anorm)


def _gla_decode_kernel(u_ref, s_ref, w2_ref, gb_ref, bnorm_ref, o_ref, s_out_ref):
    nb = u_ref.shape[0]
    seqs = range(nb)
    items = [(bi, h) for bi in seqs for h in range(B_HEADS)]
    u = [u_ref[bi] for bi in seqs]
    q = [a[:, 0:B_QK] * (B_DK ** -0.5) for a in u]
    k = [a[:, B_QK:2 * B_QK] for a in u]
    rg = [a[:, 2 * B_QK + B_V:2 * B_QK + 2 * B_V] for a in u]
    pad = [jnp.zeros(((-nb) % 8, LANE), F32)] if nb % 8 else []
    gate = _bdot(jnp.concatenate([a[:, 2 * B_QK + 2 * B_V:] for a in u] + pad, axis=0), w2_ref[...])
    dec = [jnp.exp(jax.nn.log_sigmoid(gate[bi:bi + 1] + gb_ref[...]) / B_GATE_TAU) for bi in seqs]
    cols = _columns(k + [q[bi] * dec[bi] for bi in seqs] + dec)
    v = [u[bi][:, 2 * B_QK + h * B_DV:2 * B_QK + (h + 1) * B_DV] for bi, h in items]
    s = [s_ref[0, bi, h] for bi, h in items]
    kcol = [cols[h * B_DK:(h + 1) * B_DK, bi:bi + 1] for bi, h in items]
    qacol = [cols[h * B_DK:(h + 1) * B_DK, nb + bi:nb + bi + 1] for bi, h in items]
    acol = [cols[h * B_DK:(h + 1) * B_DK, 2 * nb + bi:2 * nb + bi + 1] for bi, h in items]
    qk = [jnp.sum(q[bi][:, h * B_DK:(h + 1) * B_DK] * k[bi][:, h * B_DK:(h + 1) * B_DK], axis=-1, keepdims=True)
          for bi, h in items]
    o = [jnp.sum(qacol[n] * s[n], axis=0, keepdims=True) + qk[n] * v[n] for n in range(len(items))]
    for n, (bi, h) in enumerate(items):
        s_out_ref[bi, h] = acol[n] * s[n] + kcol[n] * v[n]
    o = [a * lax.rsqrt(jnp.mean(a * a, axis=-1, keepdims=True) + EPS) * bnorm_ref[...] for a in o]
    for bi in seqs:
        o_ref[bi] = jnp.concatenate(o[bi * B_HEADS:(bi + 1) * B_HEADS], axis=-1) * _silu(rg[bi])


def _gla_decode(u_b, s, w2, gb, bnorm, *, layer, nb):
    b = u_b.shape[0]
    return pl.pallas_call(
        _gla_decode_kernel,
        grid=(b // nb,),
        in_specs=[pl.BlockSpec((nb, 1, UB_W), lambda i: (i, 0, 0)),
                  pl.BlockSpec((1, nb, B_HEADS, B_DK, B_DV), lambda i: (layer, i, 0, 0, 0)),
                  _const_spec((LANE, B_QK)), _const_spec((1, B_QK)), _const_spec((1, B_DV))],
        out_specs=[pl.BlockSpec((nb, 1, B_V), lambda i: (i, 0, 0)),
                   pl.BlockSpec((nb, B_HEADS, B_DK, B_DV), lambda i: (i, 0, 0, 0))],
        out_shape=[jax.ShapeDtypeStruct((b, 1, B_V), F32),
                   jax.ShapeDtypeStruct((b, B_HEADS, B_DK, B_DV), F32)],
        compiler_params=_cparams(("parallel",)),
        name="gla_decode",
    )(u_b, s, w2, gb, bnorm)


def _pad_cols(w, n):
    return jnp.pad(w, ((0, 0), (0, n - w.shape[1])))


def _rope_tables(pos):
    half = C_ROPE // 2
    inv = ROPE_THETA ** (-jnp.arange(half, dtype=F32) / half)
    ang = pos.astype(F32)[:, None] * inv[None, :]
    cos, sin = jnp.cos(ang), jnp.sin(ang)
    zero = jnp.zeros_like(cos)
    ck = jnp.concatenate([cos, cos, zero, zero], axis=-1)
    sk = jnp.concatenate([-sin, sin, zero, zero], axis=-1)
    cq = jnp.tile(jnp.concatenate([cos, cos], axis=-1), (1, C_HEADS))
    sq = jnp.tile(jnp.concatenate([-sin, sin], axis=-1), (1, C_HEADS))
    return ck, sk, cq, sq


STACKED_BF16 = ('w_ffn1_gate', 'w_ffn1_up', 'w_ffn1_down', 'w_ffn2_gate', 'w_ffn2_up', 'w_ffn2_down',
                'w_xq', 'w_xo', 'w_xk', 'w_xv')


def _layer_weights(l, P, Pb):
    w_in = P['w_in'][l]
    o_b, o_c = A_IN, A_IN + B_IN
    w_a = jnp.concatenate([w_in[:, :A_CONV_CH + A_V], _pad_cols(w_in[:, A_CONV_CH + A_V:A_IN], LANE)], axis=1)
    w_b = jnp.concatenate([w_in[:, o_b:o_b + 2 * B_QK + 2 * B_V],
                           _pad_cols(w_in[:, o_b + 2 * B_QK + 2 * B_V:o_c], LANE)], axis=1)
    w_c = jnp.concatenate([w_in[:, o_c:o_c + C_Q_LORA + C_KV_LORA],
                           _pad_cols(w_in[:, o_c + C_Q_LORA + C_KV_LORA:], LANE)], axis=1)
    wuq = P['c_w_uq'][l].reshape(C_Q_LORA, C_HEADS, C_NOPE + C_ROPE)
    wuq = jnp.concatenate([wuq[:, :, :C_NOPE].reshape(C_Q_LORA, -1),
                           wuq[:, :, C_NOPE:].reshape(C_Q_LORA, -1)], axis=1)
    w_out = P['w_out'][l]
    row = lambda a: a.reshape(1, -1).astype(F32)
    head_lanes = lambda a: jnp.pad(a.reshape(1, -1).astype(F32), ((0, 0), (A_HEADS, LANE - 2 * A_HEADS)))
    return dict(
        norm_ffn1=row(P['norm_ffn1'][l]), norm_ffn2=row(P['norm_ffn2'][l]),
        ffn1=tuple((Pb[n], l) for n in ('w_ffn1_gate', 'w_ffn1_up', 'w_ffn1_down')),
        ffn2=tuple((Pb[n], l) for n in ('w_ffn2_gate', 'w_ffn2_up', 'w_ffn2_down')),
        norm_mix=row(P['norm_mix'][l]),
        w_in=(w_a.astype(BF16), w_b.astype(BF16), w_c.astype(BF16)),
        a_conv_w=P['a_conv_w'][l].astype(F32),
        a_log=head_lanes(P['a_log'][l]), a_dt_bias=head_lanes(P['a_dt_bias'][l]),
        a_norm=row(P['a_norm'][l]), a_norm_t=row(jnp.tile(P['a_norm'][l], A_HEADS)),
        b_w2=jnp.pad(P['b_gate_w2'][l], ((0, LANE - B_GATE_RANK), (0, 0))).astype(BF16),
        b_gate_bias=row(P['b_gate_bias'][l]),
        b_norm=row(P['b_norm'][l]), b_norm_t=row(jnp.tile(P['b_norm'][l], B_HEADS)),
        c_q_norm=row(P['c_q_norm'][l]), c_kv_norm=row(P['c_kv_norm'][l]),
        c_w_uq=wuq.astype(BF16),
        c_w_uk=jnp.transpose(P['c_w_uk'][l], (1, 2, 0)).astype(BF16),
        c_w_uv=jnp.transpose(P['c_w_uv'][l], (1, 0, 2)).astype(BF16),
        w_out=(w_out[:A_V].astype(BF16), w_out[A_V:A_V + B_V].astype(BF16), w_out[A_V + B_V:].astype(BF16)),
        norm_x=row(P['norm_x'][l]), norm_mem=row(P['norm_mem'][l]),
        w_xq=(Pb['w_xq'], l), w_xo=(Pb['w_xo'], l), w_xk=(Pb['w_xk'], l), w_xv=(Pb['w_xv'], l),
    )


def _pick(n, pref):
    return pref if n % pref == 0 else n


def kernel(x_prompt, x_sample, mem_prompt, cache_ckv, cache_kpe, page_table, state_conv_a, state_delta, state_gla, cache_mem_k, cache_mem_v, norm_ffn1, w_ffn1_gate, w_ffn1_up, w_ffn1_down, norm_mix, w_in, a_conv_w, a_log, a_dt_bias, a_norm, b_gate_w2, b_gate_bias, b_norm, c_q_norm, c_w_uq, c_kv_norm, c_w_uk, c_w_uv, w_out, norm_x, norm_mem, w_xq, w_xk, w_xv, w_xo, norm_ffn2, w_ffn2_gate, w_ffn2_up, w_ffn2_down, final_norm):
    P = dict(norm_ffn1=norm_ffn1, w_ffn1_gate=w_ffn1_gate, w_ffn1_up=w_ffn1_up, w_ffn1_down=w_ffn1_down,
             norm_mix=norm_mix, w_in=w_in, a_conv_w=a_conv_w, a_log=a_log, a_dt_bias=a_dt_bias,
             a_norm=a_norm, b_gate_w2=b_gate_w2, b_gate_bias=b_gate_bias, b_norm=b_norm,
             c_q_norm=c_q_norm, c_w_uq=c_w_uq, c_kv_norm=c_kv_norm, c_w_uk=c_w_uk, c_w_uv=c_w_uv,
             w_out=w_out, norm_x=norm_x, norm_mem=norm_mem, w_xq=w_xq, w_xk=w_xk, w_xv=w_xv, w_xo=w_xo,
             norm_ffn2=norm_ffn2, w_ffn2_gate=w_ffn2_gate, w_ffn2_up=w_ffn2_up, w_ffn2_down=w_ffn2_down)
    depth = w_in.shape[0]
    Pb = {n: P[n].astype(BF16) for n in STACKED_BF16}
    W = [_layer_weights(l, P, Pb) for l in range(depth)]
    fnorm = final_norm.reshape(1, -1).astype(F32)

    bp, tp, d = x_prompt.shape
    mp = bp * tp
    n_mem = mem_prompt.shape[1]
    tm = _pick(tp, 512)
    tabs_p = _rope_tables(jnp.arange(tp, dtype=jnp.int32))
    x = x_prompt.reshape(mp, d)
    mem = mem_prompt.reshape(bp * n_mem, d)
    p_ckv, p_kpe, p_conv, p_delta, p_gla, p_mk, p_mv = [], [], [], [], [], [], []
    for l in range(depth):
        w = W[l]
        mk, mv = _norm_mm(mem, w['norm_mem'], [w['w_xk'], w['w_xv']], tm=_pick(bp * n_mem, 512))
        x = _ffn(x, w['norm_ffn1'], *w['ffn1'], fnorm, final=False, tm=tm)
        x3 = x.reshape(bp, tp, d)
        w_a, w_b, w_c = w['w_in']
        o_a, conv_new, sd = _gdn_prompt(x3, w['norm_mix'], w_a, w['a_conv_w'], w['a_log'], w['a_dt_bias'],
                                        w['a_norm_t'], tblk=_pick(tp, 256), nb=_pick(bp, 2))
        o_b, sg = _gla_prompt(x3, w['norm_mix'], w_b, w['b_w2'], w['b_gate_bias'], w['b_norm_t'],
                              tblk=_pick(tp, 512))
        ckv, kpe, ckvb, kpeb, qlat, qpe = _mla_prep(x3, w['norm_mix'], w_c, w['c_q_norm'], w['c_kv_norm'],
                                                    w['c_w_uq'], w['c_w_uk'], tabs_p, tm=tm)
        o_c = _mla_flash(qlat, qpe, ckvb, kpeb, w['c_w_uv'], tq=_pick(tp, 512), tk=_pick(tp, 1024))
        x = _xattn_prompt(x.reshape(bp, tp, d), [o_a, o_b, o_c], list(w['w_out']),
                          w['norm_x'], w['w_xq'], w['w_xo'],
                          mk.reshape(bp, n_mem, d).astype(BF16), mv.reshape(bp, n_mem, d).astype(BF16),
                          tm=tm).reshape(mp, d)
        x = _ffn(x, w['norm_ffn2'], *w['ffn2'], fnorm, final=(l == depth - 1), tm=tm)
        p_ckv.append(ckv)
        p_kpe.append(kpe)
        p_conv.append(conv_new)
        p_delta.append(sd.reshape(bp, A_DK, A_HEADS, A_DV).transpose(0, 2, 1, 3))
        sg = sg.reshape(bp, B_HEADS, B_DV, B_HEADS, B_DK)
        p_gla.append(jnp.stack([sg[:, h, :, h, :] for h in range(B_HEADS)], axis=1).transpose(0, 1, 3, 2))
        p_mk.append(mk.reshape(bp, n_mem, X_HEADS, X_HDIM))
        p_mv.append(mv.reshape(bp, n_mem, X_HEADS, X_HDIM))
    y_prompt = x.reshape(bp, tp, d)

    bs, ts, _ = x_sample.shape
    n_pages, page = page_table.shape[1], cache_ckv.shape[2]
    past_len = n_pages * page
    tabs_s = _rope_tables(past_len + jnp.arange(ts, dtype=jnp.int32))
    cache_kpe_t = jnp.swapaxes(cache_kpe, 2, 3)
    mem_k_rows, mem_v_rows = _mem_rows(cache_mem_k), _mem_rows(cache_mem_v)
    nb_dec, nb_mem = _pick(bs, 8), _pick(bs, 4)
    x = x_sample.reshape(bs, d)
    s_ckv, s_kpe, s_conv, s_delta, s_gla = [], [], [], [], []
    for l in range(depth):
        w = W[l]
        x = _ffn(x, w['norm_ffn1'], *w['ffn1'], fnorm, final=False, tm=bs)
        w_a, w_b, w_c = w['w_in']
        u_a, u_b = _norm_mm(x, w['norm_mix'], [w_a, w_b], tm=bs)
        o_a, conv_new, sd = _gdn_decode(u_a.reshape(bs, 1, UA_W), state_conv_a, state_delta,
                                        w['a_conv_w'], w['a_log'], w['a_dt_bias'], w['a_norm'], layer=l,
                                        nb=nb_dec)
        o_b, sg = _gla_decode(u_b.reshape(bs, 1, UB_W), state_gla, w['b_w2'], w['b_gate_bias'], w['b_norm'],
                              layer=l, nb=nb_dec)
        ckv, kpe, _, _, qlat, qpe = _mla_prep(x.reshape(1, bs, d), w['norm_mix'], w_c, w['c_q_norm'],
                                              w['c_kv_norm'], w['c_w_uq'], w['c_w_uk'], tabs_s, tm=bs)
        pad_heads = lambda a: jnp.pad(a[0].transpose(1, 0, 2), ((0, 0), (0, 8 - C_HEADS), (0, 0)))
        olat = _mla_paged(page_table, pad_heads(qlat), pad_heads(qpe), ckv.reshape(bs, 1, C_KV_LORA),
                          kpe.reshape(bs, 1, C_ROPE), cache_ckv, cache_kpe_t, layer=l, ppc=_pick(n_pages, 64))
        o_c = _mla_oproj(olat[:, :C_HEADS].transpose(1, 0, 2)[None], w['c_w_uv'], tm=bs)
        x = _mm_res(x, [o_a.reshape(bs, A_V), o_b.reshape(bs, B_V), o_c.reshape(bs, C_HEADS * C_VDIM)],
                    list(w['w_out']), tm=bs)
        (q,) = _norm_mm(x, w['norm_x'], [w['w_xq']], tm=bs)
        att = _xattn_decode(q.reshape(bs, 1, d), mem_k_rows, mem_v_rows, layer=l, nb=nb_mem)
        x = _mm_res(x, [att.reshape(bs, d)], [w['w_xo']], tm=bs)
        x = _ffn(x, w['norm_ffn2'], *w['ffn2'], fnorm, final=(l == depth - 1), tm=bs)
        s_ckv.append(ckv.reshape(bs, ts, C_KV_LORA))
        s_kpe.append(kpe.reshape(bs, ts, C_ROPE))
        s_conv.append(conv_new)
        s_delta.append(sd)
        s_gla.append(sg)
    y_sample = x.reshape(bs, ts, d)

    return (y_prompt, y_sample,
            jnp.stack(p_ckv), jnp.stack(p_kpe), jnp.stack(p_conv), jnp.stack(p_delta), jnp.stack(p_gla),
            jnp.stack(p_mk), jnp.stack(p_mv),
            jnp.stack(s_ckv), jnp.stack(s_kpe), jnp.stack(s_conv), jnp.stack(s_delta), jnp.stack(s_gla))
```
